```python
import math
import jax, jax.numpy as jnp
from jax import lax
import numpy as np

D_MODEL = 1024
BATCH = 4
SEQ = 8192
DEPTH = 2

HEAD_DIM = 64
BRANCH_WIDTH = D_MODEL // 2
N_BRANCH = 3
BLOCK = 128
A_HEADS = BRANCH_WIDTH // HEAD_DIM
A_KV_HEADS = 2
WINDOW = 128
B_HEADS = BRANCH_WIDTH // HEAD_DIM
B_KV_HEADS = 2
IDX_HEADS = 4
IDX_DIM = 32
TOPK_MAX = 256
C_HEADS = BRANCH_WIDTH // HEAD_DIM
FORGET_BIAS_INIT = 4.0
N_BUCKETS = 32
MAX_DISTANCE = 512
EPS = 1e-6
ATTN_SCALE = HEAD_DIM ** -0.5

A_SIZES = (A_HEADS * HEAD_DIM, A_KV_HEADS * HEAD_DIM, A_KV_HEADS * HEAD_DIM, BRANCH_WIDTH)
B_SIZES = (B_HEADS * HEAD_DIM, B_KV_HEADS * HEAD_DIM, B_KV_HEADS * HEAD_DIM,
           IDX_HEADS * IDX_DIM, IDX_DIM, IDX_HEADS, BRANCH_WIDTH)
C_SIZES = (C_HEADS * HEAD_DIM, C_HEADS * HEAD_DIM, C_HEADS * HEAD_DIM, C_HEADS, BRANCH_WIDTH)
MERGE_SIZES = (N_BRANCH * D_MODEL,)
IN_SIZES = A_SIZES + B_SIZES + C_SIZES + MERGE_SIZES
IN_COLS = sum(IN_SIZES)

kernel_name = "hybrid_swa_dsa_fox_gated_merge"


def _rms_norm(x, g):
    x32 = x.astype(jnp.float32)
    y = x32 * lax.rsqrt(jnp.mean(x32 * x32, axis=-1, keepdims=True) + EPS)
    return (y * g.astype(jnp.float32)).astype(x.dtype)


def _split(z, sizes):
    outs, start = [], 0
    for s in sizes:
        outs.append(z[..., start:start + s])
        start += s
    return outs


def _t5_bucket(delta):
    n = jnp.maximum(delta, 0)
    max_exact = N_BUCKETS // 2
    nf = jnp.maximum(n, 1).astype(jnp.float32)
    large = max_exact + (jnp.log(nf / max_exact) / math.log(MAX_DISTANCE / max_exact)
                         * (N_BUCKETS - max_exact)).astype(jnp.int32)
    large = jnp.minimum(large, N_BUCKETS - 1)
    return jnp.where(n < max_exact, n, large)


def _to_blocks(a, nb):
    return a.reshape(a.shape[0], nb, BLOCK, *a.shape[2:]).swapaxes(0, 1)


def _sliding_window_attention(q, k, v, sinks, bias_table):
    b_, s_, h_, dh = q.shape
    g_ = h_ // A_KV_HEADS
    nb = s_ // BLOCK
    qb = q.reshape(b_, nb, BLOCK, A_KV_HEADS, g_, dh)
    kb = k.reshape(b_, nb, BLOCK, A_KV_HEADS, dh)
    vb = v.reshape(b_, nb, BLOCK, A_KV_HEADS, dh)
    pad = ((0, 0), (1, 0), (0, 0), (0, 0), (0, 0))
    kk = jnp.concatenate([jnp.pad(kb, pad)[:, :-1], kb], axis=2)
    vv = jnp.concatenate([jnp.pad(vb, pad)[:, :-1], vb], axis=2)
    logits = jnp.einsum('bnqhgd,bnkhd->bnhgqk', qb, kk).astype(jnp.float32) * ATTN_SCALE
    qi = jnp.arange(BLOCK)[:, None]
    ki = jnp.arange(2 * BLOCK)[None, :]
    delta = qi + BLOCK - ki
    band = (delta >= 0) & (delta < WINDOW)
    not_pad = (jnp.arange(nb)[:, None, None] > 0) | (ki >= BLOCK)[None]
    mask = band[None] & not_pad
    bias = bias_table[_t5_bucket(delta)].astype(jnp.float32)
    bias = bias.transpose(2, 0, 1).reshape(A_KV_HEADS, g_, BLOCK, 2 * BLOCK)
    logits = jnp.where(mask[None, :, None, None], logits + bias[None, None], -jnp.inf)
    sink = sinks.astype(jnp.float32).reshape(A_KV_HEADS, g_)[None, None, :, :, None, None]
    sink = jnp.broadcast_to(sink, logits.shape[:-1] + (1,))
    p = jax.nn.softmax(jnp.concatenate([logits, sink], axis=-1), axis=-1)[..., :-1]
    o = jnp.einsum('bnhgqk,bnkhd->bnqhgd', p.astype(v.dtype), vv)
    return o.reshape(b_, s_, h_, dh)


def _dsa_attention(q, k, v, iq, ik, iw, bias_table):
    b_, s_, h_, dh = q.shape
    g_ = h_ // B_KV_HEADS
    nb = s_ // BLOCK
    topk = min(TOPK_MAX, s_ // 4)
    kpos = jnp.arange(s_)

    def one_block(args):
        n, qb, iqb, iwb = args
        tpos = n * BLOCK + jnp.arange(BLOCK)
        causal = kpos[None, :] <= tpos[:, None]
        sc = jnp.einsum('bqhc,bsc->bqhs', iqb, ik).astype(jnp.float32) * (IDX_DIM ** -0.5)
        w = iwb.astype(jnp.float32) * (IDX_HEADS ** -0.5)
        score = jnp.einsum('bqh,bqhs->bqs', w, jax.nn.relu(sc))
        score = jnp.where(causal[None], score, -jnp.inf)
        _, idx = lax.top_k(score, topk)
        ksel = jax.vmap(lambda kb_, ib: kb_[ib])(k, idx)
        vsel = jax.vmap(lambda vb_, ib: vb_[ib])(v, idx)
        qg = qb.reshape(b_, BLOCK, B_KV_HEADS, g_, dh)
        logits = jnp.einsum('bqhgd,bqkhd->bqhgk', qg, ksel).astype(jnp.float32) * ATTN_SCALE
        delta = tpos[None, :, None] - idx
        bias = bias_table[_t5_bucket(delta)].astype(jnp.float32)
        bias = bias.reshape(b_, BLOCK, topk, B_KV_HEADS, g_).transpose(0, 1, 3, 4, 2)
        valid = (delta >= 0)[:, :, None, None, :]
        logits = jnp.where(valid, logits + bias, -jnp.inf)
        p = jax.nn.softmax(logits, axis=-1).astype(v.dtype)
        o = jnp.einsum('bqhgk,bqkhd->bqhgd', p, vsel)
        return o.reshape(b_, BLOCK, h_, dh)

    out = lax.map(one_block, (jnp.arange(nb), _to_blocks(q, nb), _to_blocks(iq, nb), _to_blocks(iw, nb)))
    return out.swapaxes(0, 1).reshape(b_, s_, h_, dh)


def _forgetting_attention(q, k, v, log_f):
    b_, s_, h_, dh = q.shape
    nb = s_ // BLOCK
    c = jnp.cumsum(log_f, axis=1)
    c_keys = c.transpose(0, 2, 1)
    kpos = jnp.arange(s_)

    def one_block(args):
        n, qb, cb = args
        tpos = n * BLOCK + jnp.arange(BLOCK)
        logits = jnp.einsum('bqhd,bshd->bhqs', qb, k).astype(jnp.float32) * ATTN_SCALE
        logits = logits + cb.transpose(0, 2, 1)[..., None] - c_keys[:, :, None, :]
        causal = kpos[None, :] <= tpos[:, None]
        logits = jnp.where(causal[None, None], logits, -jnp.inf)
        p = jax.nn.softmax(logits, axis=-1).astype(v.dtype)
        return jnp.einsum('bhqs,bshd->bqhd', p, v)

    out = lax.map(one_block, (jnp.arange(nb), _to_blocks(q, nb), _to_blocks(c, nb)))
    return out.swapaxes(0, 1).reshape(b_, s_, h_, dh)


def setup_inputs(seed: int = 0) -> dict:
    key = jax.random.key(seed)
    ks = jax.random.split(key, 9)
    x = jax.random.normal(ks[0], (BATCH, SEQ, D_MODEL), jnp.float32)
    norm_gain = 1.0 + 0.02 * jax.random.normal(ks[1], (DEPTH, D_MODEL), jnp.float32)
    w_in = jax.random.normal(ks[2], (DEPTH, D_MODEL, IN_COLS), jnp.float32) * D_MODEL ** -0.5
    b_forget = FORGET_BIAS_INIT + 0.1 * jax.random.normal(ks[3], (DEPTH, C_HEADS), jnp.float32)
    qk_gain = 1.0 + 0.02 * jax.random.normal(ks[4], (DEPTH, N_BRANCH, 2, HEAD_DIM), jnp.float32)
    sinks = 0.5 * jax.random.normal(ks[5], (DEPTH, A_HEADS), jnp.float32)
    w_branch = jax.random.normal(ks[6], (DEPTH, N_BRANCH, BRANCH_WIDTH, D_MODEL), jnp.float32) * BRANCH_WIDTH ** -0.5
    w_out = jax.random.normal(ks[7], (DEPTH, D_MODEL, D_MODEL), jnp.float32) * D_MODEL ** -0.5
    rel_bias = 0.1 * jax.random.normal(ks[8], (N_BUCKETS, A_HEADS + B_HEADS), jnp.float32)
    return {"x": x, "norm_gain": norm_gain, "w_in": w_in, "b_forget": b_forget,
            "qk_gain": qk_gain, "sinks": sinks, "w_branch": w_branch, "w_out": w_out,
            "rel_bias": rel_bias}


def reference(x, norm_gain, w_in, b_forget, qk_gain, sinks, w_branch, w_out, rel_bias):
    b_, s_, _ = x.shape
    bias_a = rel_bias[:, :A_HEADS]
    bias_b = rel_bias[:, A_HEADS:]
    for layer in range(DEPTH):
        h = _rms_norm(x, norm_gain[layer])
        z = jnp.einsum('bsd,dc->bsc', h, w_in[layer])
        (a_q, a_k, a_v, a_gate,
         b_q, b_k, b_v, b_iq, b_ik, b_iw, b_gate,
         c_q, c_k, c_v, c_f, c_gate,
         merge) = _split(z, IN_SIZES)
        g = qk_gain[layer]

        def heads(t, n):
            return t.reshape(b_, s_, n, HEAD_DIM)

        qa = _rms_norm(heads(a_q, A_HEADS), g[0, 0])
        ka = _rms_norm(heads(a_k, A_KV_HEADS), g[0, 1])
        oa = _sliding_window_attention(qa, ka, heads(a_v, A_KV_HEADS), sinks[layer], bias_a)
        qb = _rms_norm(heads(b_q, B_HEADS), g[1, 0])
        kb = _rms_norm(heads(b_k, B_KV_HEADS), g[1, 1])
        ob = _dsa_attention(qb, kb, heads(b_v, B_KV_HEADS),
                            b_iq.reshape(b_, s_, IDX_HEADS, IDX_DIM), b_ik, b_iw, bias_b)
        qc = _rms_norm(heads(c_q, C_HEADS), g[2, 0])
        kc = _rms_norm(heads(c_k, C_HEADS), g[2, 1])
        log_f = jax.nn.log_sigmoid((c_f + b_forget[layer]).astype(jnp.float32))
        oc = _forgetting_attention(qc, kc, heads(c_v, C_HEADS), log_f)

        branches = jnp.stack([oa.reshape(b_, s_, BRANCH_WIDTH) * jax.nn.silu(a_gate),
                              ob.reshape(b_, s_, BRANCH_WIDTH) * jax.nn.silu(b_gate),
                              oc.reshape(b_, s_, BRANCH_WIDTH) * jax.nn.silu(c_gate)], axis=2)
        y = jnp.einsum('bsnc,ncd->bsnd', branches, w_branch[layer])
        gates = jax.nn.sigmoid(merge.reshape(b_, s_, N_BRANCH, D_MODEL))
        merged = jnp.sum(gates * y, axis=2)
        x = x + jnp.einsum('bsd,de->bse', merged, w_out[layer])
    return x
```

```python
import functools
import math

import numpy as np
import jax
import jax.numpy as jnp
from jax import lax
from jax.experimental import pallas as pl
from jax.experimental.pallas import tpu as pltpu

F32 = jnp.float32
BF16 = jnp.bfloat16

HEAD_DIM = 64
LANES = 128
N_HEADS = 8
N_PAIRS = N_HEADS // 2
KV_GROUP = 4
BRANCH = N_HEADS * HEAD_DIM
WINDOW = 128
IDX_HEADS = 4
IDX_DIM = 32
TOPK_MAX = 256
N_BUCKETS = 32
MAX_DISTANCE = 512
EPS = 1e-6
ATTN_SCALE = HEAD_DIM ** -0.5
NEG = -1e30
T5_FAR = 413

_SIZES = dict(a_q=512, a_k=128, a_v=128, a_gate=512,
              b_q=512, b_k=128, b_v=128, b_iq=128, b_ik=32, b_iw=4, b_gate=512,
              c_q=512, c_k=512, c_v=512, c_f=8, c_gate=512, merge=3072)
_ORIG = {}
_o = 0
for _k, _v in _SIZES.items():
    _ORIG[_k] = _o
    _o += _v
IN_COLS = _o

OFF_AQ, OFF_AG, OFF_BQ, OFF_BG = 0, 512, 1024, 1536
OFF_CQ, OFF_CK, OFF_CV, OFF_CG = 2048, 2560, 3072, 3584
OFF_MERGE = 4096
OFF_AK, OFF_AV, OFF_BK, OFF_BV = 7168, 7424, 7680, 7936
OFF_BIQ, OFF_BIK, OFF_MISC = 8192, 8320, 8448
NP_COLS = 8704
PROJ_TN = 2176
MISC_IW = 0
MISC_CF = 8


def _column_map():
    cols = np.full((NP_COLS,), -1, np.int64)

    def put(off, name, idx=None):
        src = np.arange(_SIZES[name]) + _ORIG[name]
        if idx is not None:
            src = src[idx]
        cols[off:off + len(src)] = src

    for off, name in ((OFF_AQ, "a_q"), (OFF_AG, "a_gate"), (OFF_BQ, "b_q"), (OFF_BG, "b_gate"),
                      (OFF_CQ, "c_q"), (OFF_CK, "c_k"), (OFF_CV, "c_v"), (OFF_CG, "c_gate"),
                      (OFF_MERGE, "merge"), (OFF_BIQ, "b_iq")):
        put(off, name)
    dup = np.concatenate([np.arange(64), np.arange(64), np.arange(64, 128), np.arange(64, 128)])
    for off, name in ((OFF_AK, "a_k"), (OFF_AV, "a_v"), (OFF_BK, "b_k"), (OFF_BV, "b_v")):
        put(off, name, dup)
    put(OFF_BIK, "b_ik", np.tile(np.arange(IDX_DIM), IDX_HEADS))
    put(OFF_MISC + MISC_IW, "b_iw")
    put(OFF_MISC + MISC_CF, "c_f")
    return cols


_COLS = _column_map()


def _params(sem, vmem_mb):
    return pltpu.CompilerParams(dimension_semantics=sem, vmem_limit_bytes=vmem_mb * 1024 * 1024)


def _dot(a, b):
    return jnp.dot(a, b, preferred_element_type=F32)


def _dot_nt(a, b):
    return lax.dot_general(a, b, (((1,), (1,)), ((), ())), preferred_element_type=F32)


def _t5_bucket(delta):
    n = jnp.maximum(delta, 0)
    max_exact = N_BUCKETS // 2
    nf = jnp.maximum(n, 1).astype(F32)
    large = max_exact + (jnp.log(nf / max_exact) / math.log(MAX_DISTANCE / max_exact)
                         * (N_BUCKETS - max_exact)).astype(jnp.int32)
    large = jnp.minimum(large, N_BUCKETS - 1)
    return jnp.where(n < max_exact, n, large)


def _proj_kernel(x_ref, g_ref, w_ref, z_ref, misc_ref, h_ref, *, misc_tile, misc_local):
    j = pl.program_id(1)

    @pl.when(j == 0)
    def _():
        x = x_ref[...]
        ms = jnp.mean(x * x, axis=-1, keepdims=True)
        h_ref[...] = (x * lax.rsqrt(ms + EPS) * g_ref[...]).astype(BF16)

    acc = _dot(h_ref[...], w_ref[...])
    z_ref[...] = acc.astype(BF16)

    @pl.when(j == misc_tile)
    def _():
        misc_ref[...] = acc[:, misc_local:misc_local + LANES]


def _proj(x2d, gain, w):
    n, d = x2d.shape
    tm = 1024
    return pl.pallas_call(
        functools.partial(_proj_kernel, misc_tile=OFF_MISC // PROJ_TN, misc_local=OFF_MISC % PROJ_TN),
        grid=(n // tm, NP_COLS // PROJ_TN),
        in_specs=[pl.BlockSpec((tm, d), lambda i, j: (i, 0)),
                  pl.BlockSpec((1, d), lambda i, j: (0, 0)),
                  pl.BlockSpec((d, PROJ_TN), lambda i, j: (0, j))],
        out_specs=[pl.BlockSpec((tm, PROJ_TN), lambda i, j: (i, j)),
                   pl.BlockSpec((tm, LANES), lambda i, j: (i, 0))],
        out_shape=[jax.ShapeDtypeStruct((n, NP_COLS), BF16),
                   jax.ShapeDtypeStruct((n, LANES), F32)],
        scratch_shapes=[pltpu.VMEM((tm, d), BF16)],
        compiler_params=_params(("arbitrary", "arbitrary"), 56),
        name="proj",
    )(x2d, gain, w)


def _group_rms(x_bf16, g_ref, gain_row):
    x = x_bf16.astype(F32)
    width = x.shape[-1]
    sq = x * x
    hi = sq.astype(BF16)
    lo = (sq - hi.astype(F32)).astype(BF16)
    g = g_ref[:width, :width]
    ss = _dot(hi, g) + _dot(lo, g)
    return (x * lax.rsqrt(ss * (1.0 / HEAD_DIM) + EPS) * gain_row).astype(BF16)


def _prep_kernel(aq, ak, bq, bk, cq, ck, misc, gains, gmat, lmat,
                 qa_o, ka_o, qb_o, kb_o, qc_o, kc_o, cum_o, carry):
    t = pl.program_id(1)
    qa_o[...] = _group_rms(aq[...], gmat, gains[0:1, :])
    ka_o[...] = _group_rms(ak[...], gmat, gains[1:2, :256])
    qb_o[...] = _group_rms(bq[...], gmat, gains[2:3, :])
    kb_o[...] = _group_rms(bk[...], gmat, gains[3:4, :256])
    qc_o[...] = _group_rms(cq[...], gmat, gains[4:5, :])
    kc_o[...] = _group_rms(ck[...], gmat, gains[5:6, :])

    @pl.when(t == 0)
    def _():
        carry[...] = jnp.zeros_like(carry)

    xm = misc[...] + gains[6:7, :LANES]
    logf = jnp.minimum(xm, 0.0) - jnp.log(1.0 + jnp.exp(-jnp.abs(xm)))
    hi = logf.astype(BF16)
    r1 = logf - hi.astype(F32)
    mid = r1.astype(BF16)
    lo = (r1 - mid.astype(F32)).astype(BF16)
    lm = lmat[...]
    c = _dot(lm, hi) + _dot(lm, mid) + _dot(lm, lo) + carry[...]
    cum_o[...] = c
    rows = c.shape[0]
    carry[...] = c[rows - 1:rows, :]


def _prep(z, misc, gains, gmat, lmat, batch, seq):
    n = z.shape[0]
    tp = lmat.shape[0]
    nt = seq // tp

    def zspec(width, off):
        return pl.BlockSpec((tp, width), lambda b, t, _c=off // width: (b * nt + t, _c))

    def ospec(width):
        return pl.BlockSpec((tp, width), lambda b, t: (b * nt + t, 0))

    return pl.pallas_call(
        _prep_kernel,
        grid=(batch, nt),
        in_specs=[zspec(512, OFF_AQ), zspec(256, OFF_AK), zspec(512, OFF_BQ), zspec(256, OFF_BK),
                  zspec(512, OFF_CQ), zspec(512, OFF_CK), ospec(LANES),
                  pl.BlockSpec((8, 512), lambda b, t: (0, 0)),
                  pl.BlockSpec((512, 512), lambda b, t: (0, 0)),
                  pl.BlockSpec((tp, tp), lambda b, t: (0, 0))],
        out_specs=[ospec(512), ospec(256), ospec(512), ospec(256), ospec(512), ospec(512), ospec(LANES)],
        out_shape=[jax.ShapeDtypeStruct((n, 512), BF16), jax.ShapeDtypeStruct((n, 256), BF16),
                   jax.ShapeDtypeStruct((n, 512), BF16), jax.ShapeDtypeStruct((n, 256), BF16),
                   jax.ShapeDtypeStruct((n, 512), BF16), jax.ShapeDtypeStruct((n, 512), BF16),
                   jax.ShapeDtypeStruct((n, LANES), F32)],
        scratch_shapes=[pltpu.VMEM((1, LANES), F32)],
        compiler_params=_params(("arbitrary", "arbitrary"), 48),
        name="prep",
    )(z, z, z, z, z, z, misc, gains, gmat, lmat)


def _half_mask(shape, half):
    lane = lax.broadcasted_iota(jnp.int32, shape, len(shape) - 1)
    return (lane < HEAD_DIM) if half == 0 else (lane >= HEAD_DIM)


def _head_q(q_ref, h):
    p = h // 2
    qp = q_ref[:, p * LANES:(p + 1) * LANES]
    return jnp.where(_half_mask(qp.shape, h % 2), qp, jnp.zeros_like(qp))


def _softmax_step(h, s, v, m_ref, l_ref, acc_ref):
    tk = s.shape[1]
    m_prev = m_ref[h]
    m_new = jnp.maximum(m_prev, jnp.max(s, axis=1, keepdims=True))
    alpha = jnp.exp(m_prev - m_new)
    p = jnp.exp(s - jnp.tile(m_new, (1, tk // LANES)))
    l_ref[h] = alpha * l_ref[h] + jnp.sum(p, axis=1, keepdims=True)
    acc_ref[h] = alpha * acc_ref[h] + _dot(p.astype(BF16), v)
    m_ref[h] = m_new


def _init_state(m_ref, l_ref, acc_ref):
    m_ref[...] = jnp.full(m_ref.shape, NEG, F32)
    l_ref[...] = jnp.zeros(l_ref.shape, F32)
    acc_ref[...] = jnp.zeros(acc_ref.shape, F32)


def _write_gated(o_ref, gate_ref, l_ref, acc_ref):
    for p in range(N_PAIRS):
        o0 = acc_ref[2 * p] / l_ref[2 * p]
        o1 = acc_ref[2 * p + 1] / l_ref[2 * p + 1]
        o = jnp.where(_half_mask(o0.shape, 0), o0, o1)
        g = gate_ref[:, p * LANES:(p + 1) * LANES].astype(F32)
        o_ref[:, p * LANES:(p + 1) * LANES] = (o * (g * jax.nn.sigmoid(g))).astype(o_ref.dtype)


def _mix_a_kernel(sink_ref, q_ref, kp_ref, kc_ref, vp_ref, vc_ref, gate_ref, tbl_ref, o_ref,
                  m_ref, l_ref, acc_ref):
    n = pl.program_id(1)
    pad_pen = jnp.where(n > 0, 0.0, NEG).astype(F32)
    for h in range(N_HEADS):
        kv = h // KV_GROUP
        q = _head_q(q_ref, h)
        ksl = slice(kv * LANES, (kv + 1) * LANES)
        s_prev = _dot_nt(q, kp_ref[:, ksl]) + tbl_ref[h, :, :WINDOW] + pad_pen
        s_cur = _dot_nt(q, kc_ref[:, ksl]) + tbl_ref[h, :, WINDOW:]
        sink = sink_ref[h]
        m = jnp.maximum(jnp.maximum(jnp.max(s_prev, axis=1, keepdims=True),
                                    jnp.max(s_cur, axis=1, keepdims=True)), sink)
        p_prev = jnp.exp(s_prev - m)
        p_cur = jnp.exp(s_cur - m)
        l = (jnp.sum(p_prev, axis=1, keepdims=True) + jnp.sum(p_cur, axis=1, keepdims=True)
             + jnp.exp(sink - m))
        acc = _dot(p_prev.astype(BF16), vp_ref[:, ksl]) + _dot(p_cur.astype(BF16), vc_ref[:, ksl])
        acc_ref[h] = acc
        l_ref[h] = jnp.broadcast_to(l, acc.shape)
    _write_gated(o_ref, gate_ref, l_ref, acc_ref)


def _mix_a(qa, ka, z, tbl, sinks, batch, seq):
    n = qa.shape[0]
    t = WINDOW
    nb = seq // t
    return pl.pallas_call(
        _mix_a_kernel,
        grid=(batch, nb),
        in_specs=[pl.BlockSpec(memory_space=pltpu.SMEM),
                  pl.BlockSpec((t, 512), lambda b, i: (b * nb + i, 0)),
                  pl.BlockSpec((t, 256), lambda b, i: (b * nb + jnp.maximum(i - 1, 0), 0)),
                  pl.BlockSpec((t, 256), lambda b, i: (b * nb + i, 0)),
                  pl.BlockSpec((t, 256), lambda b, i: (b * nb + jnp.maximum(i - 1, 0), OFF_AV // 256)),
                  pl.BlockSpec((t, 256), lambda b, i: (b * nb + i, OFF_AV // 256)),
                  pl.BlockSpec((t, 512), lambda b, i: (b * nb + i, OFF_AG // 512)),
                  pl.BlockSpec((N_HEADS, t, 2 * t), lambda b, i: (0, 0, 0))],
        out_specs=pl.BlockSpec((t, 512), lambda b, i: (b * nb + i, 0)),
        out_shape=jax.ShapeDtypeStruct((n, 512), BF16),
        scratch_shapes=[pltpu.VMEM((N_HEADS, t, LANES), F32)] * 3,
        compiler_params=_params(("arbitrary", "arbitrary"), 32),
        name="mix_a",
    )(sinks, qa, ka, ka, z, z, z, tbl)


SEL_TQ = 128
SEL_SC = 512
SEL_MC = 256
SEL_FAST_ITERS = 22
SEL_MAX_ITERS = 600


def _sel_b_kernel(iq_ref, ik_ref, misc_ref, umat_ref, o_ref, sc_ref, *, topk):
    n = pl.program_id(1)
    t0 = n * SEL_TQ
    n_sc = (n + SEL_SC // SEL_TQ) // (SEL_SC // SEL_TQ)
    n_mc = (n + SEL_MC // SEL_TQ) // (SEL_MC // SEL_TQ)
    kf = float(topk)

    iq = iq_ref[...]
    lane = lax.broadcasted_iota(jnp.int32, iq.shape, 1)
    qs = jnp.concatenate(
        [jnp.where((lane >= h * IDX_DIM) & (lane < (h + 1) * IDX_DIM), iq, jnp.zeros_like(iq))
         for h in range(IDX_HEADS)], axis=0)
    wscale = (IDX_HEADS ** -0.5) * (IDX_DIM ** -0.5)
    w = [misc_ref[:, MISC_IW + h:MISC_IW + h + 1] * wscale for h in range(IDX_HEADS)]
    tpos = t0 + lax.broadcasted_iota(jnp.int32, (SEL_TQ, SEL_SC), 0)
    kiota = lax.broadcasted_iota(jnp.int32, (SEL_TQ, SEL_SC), 1)

    def score_body(c, carry):
        rmax, rmin = carry
        base = pl.multiple_of(c * SEL_SC, SEL_SC)
        raw = _dot_nt(qs, ik_ref[pl.ds(base, SEL_SC), :])
        score = w[0] * jnp.maximum(raw[0:SEL_TQ], 0.0)
        for h in range(1, IDX_HEADS):
            score = score + w[h] * jnp.maximum(raw[h * SEL_TQ:(h + 1) * SEL_TQ], 0.0)
        causal = (kiota + base) <= tpos
        sc_ref[:, pl.ds(base, SEL_SC)] = jnp.where(causal, score, -jnp.inf)
        rmax = jnp.maximum(rmax, jnp.max(jnp.where(causal, score, -jnp.inf), axis=1, keepdims=True))
        rmin = jnp.minimum(rmin, jnp.min(jnp.where(causal, score, jnp.inf), axis=1, keepdims=True))
        return rmax, rmin

    rmax, rmin = lax.fori_loop(
        0, n_sc, score_body,
        (jnp.full((SEL_TQ, 1), -jnp.inf, F32), jnp.full((SEL_TQ, 1), jnp.inf, F32)))

    navail = (t0 + 1 + lax.broadcasted_iota(jnp.int32, (SEL_TQ, 1), 0)).astype(F32)

    def sweep(mid, snap):
        midb = jnp.broadcast_to(mid, (SEL_TQ, LANES))

        def body(c, carry):
            base = pl.multiple_of(c * SEL_SC, SEL_SC)
            cnt, up, dn = carry
            for u in range(SEL_SC // LANES):
                x = sc_ref[:, pl.ds(base + u * LANES, LANES)]
                ge = x >= midb
                cnt = cnt + jnp.where(ge, 1.0, 0.0)
                if snap:
                    up = jnp.minimum(up, jnp.where(ge, x, jnp.inf))
                    dn = jnp.maximum(dn, jnp.where(ge, -jnp.inf, x))
            return cnt, up, dn

        init = (jnp.zeros((SEL_TQ, LANES), F32), jnp.full((SEL_TQ, LANES), jnp.inf, F32),
                jnp.full((SEL_TQ, LANES), -jnp.inf, F32))
        cnt, up, dn = lax.fori_loop(0, n_sc, body, init)
        return (jnp.sum(cnt, axis=1, keepdims=True), jnp.min(up, axis=1, keepdims=True),
                jnp.max(dn, axis=1, keepdims=True))

    def advance(st, mid, snap):
        it, lo, hi, hi_dn, c_hi, done = st
        cnt, up, dn = sweep(mid, snap)
        live = jnp.where(done > 0.5, 0.0, jnp.where(mid <= lo, 0.0, jnp.where(mid >= hi, 0.0, 1.0)))
        if it is None:
            live = jnp.where(done > 0.5, 0.0, 1.0)
        ge = jnp.where(cnt >= kf, live, 0.0) > 0.5
        lt = jnp.where(cnt >= kf, 0.0, live) > 0.5
        lo = jnp.where(ge, up if snap else mid, lo)
        hi = jnp.where(lt, mid, hi)
        hi_dn = jnp.where(lt, dn if snap else jnp.inf, hi_dn)
        c_hi = jnp.where(lt, cnt, c_hi)
        done = jnp.where(live < 0.5, 1.0, jnp.where(cnt == kf, 1.0, jnp.where(hi_dn <= lo, 1.0, 0.0)))
        return (jnp.int32(1) if it is None else it + 1), lo, hi, hi_dn, c_hi, done

    def make_step(snap):
        def step(st):
            return advance(st, 0.5 * st[1] + 0.5 * st[2], snap)
        return step

    def make_cond(limit):
        def cond(st):
            return jnp.logical_and(st[0] < limit, jnp.min(st[5]) < 0.5)
        return cond

    st = (None, rmin, jnp.full((SEL_TQ, 1), jnp.inf, F32), rmax,
          jnp.zeros((SEL_TQ, 1), F32), jnp.where(navail <= kf, 1.0, 0.0))
    st = advance(st, rmax, False)
    st = lax.while_loop(make_cond(SEL_FAST_ITERS), make_step(False), st)
    st = lax.while_loop(make_cond(SEL_MAX_ITERS), make_step(True), st)
    _, lo, hi, _, c_hi, _ = st
    need = kf - c_hi

    o_ref[...] = jnp.full(o_ref.shape, NEG, o_ref.dtype)
    umat = umat_ref[...]

    def mask_body(c, run):
        base = pl.multiple_of(c * SEL_MC, SEL_MC)
        x = sc_ref[:, pl.ds(base, SEL_MC)]
        bnd = jnp.where(x >= lo, jnp.where(x < hi, 1.0, 0.0), 0.0)
        incl = _dot(bnd.astype(BF16), umat) + run
        keep = jnp.where(x >= hi, 1.0, jnp.where(incl - bnd < need, bnd, 0.0))
        o_ref[0, :, pl.ds(base, SEL_MC)] = jnp.where(keep > 0.5, 0.0, NEG).astype(o_ref.dtype)
        return incl[:, SEL_MC - 1:SEL_MC]

    lax.fori_loop(0, n_mc, mask_body, jnp.zeros((SEL_TQ, 1), F32))


def _sel_b(z, misc, umat, batch, seq):
    nb = seq // SEL_TQ
    topk = min(TOPK_MAX, seq // 4)
    return pl.pallas_call(
        functools.partial(_sel_b_kernel, topk=topk),
        grid=(batch, nb),
        in_specs=[pl.BlockSpec((SEL_TQ, LANES), lambda b, i: (b * nb + i, OFF_BIQ // LANES)),
                  pl.BlockSpec((seq, LANES), lambda b, i: (b, OFF_BIK // LANES)),
                  pl.BlockSpec((SEL_TQ, LANES), lambda b, i: (b * nb + i, 0)),
                  pl.BlockSpec((SEL_MC, SEL_MC), lambda b, i: (0, 0))],
        out_specs=pl.BlockSpec((1, SEL_TQ, seq), lambda b, i: (b, i, 0)),
        out_shape=jax.ShapeDtypeStruct((batch, seq, seq), BF16),
        scratch_shapes=[pltpu.VMEM((SEL_TQ, seq), F32)],
        compiler_params=_params(("arbitrary", "arbitrary"), 48),
        name="sel_b",
    )(z, z, misc, umat)


ATT_T = 256
B_NEAR = 3


def _mix_b_kernel(q_ref, k_ref, v_ref, mask_ref, gate_ref, tbl_ref, o_ref, m_ref, l_ref, acc_ref):
    i = pl.program_id(1)
    _init_state(m_ref, l_ref, acc_ref)
    qs = [_head_q(q_ref, h) for h in range(N_HEADS)]

    def tile(j, near):
        base = pl.multiple_of(j * ATT_T, ATT_T)
        madd = mask_ref[0, :, pl.ds(base, ATT_T)].astype(F32)
        for h in range(N_HEADS):
            kv = h // KV_GROUP
            ksl = slice(kv * LANES, (kv + 1) * LANES)
            s = _dot_nt(qs[h], k_ref[pl.ds(base, ATT_T), ksl]) + madd
            if near is not None:
                s = s + tbl_ref[h, near]
            _softmax_step(h, s, v_ref[pl.ds(base, ATT_T), ksl], m_ref, l_ref, acc_ref)

    def far_body(j, carry):
        tile(j, None)
        return carry

    lax.fori_loop(0, jnp.maximum(i - (B_NEAR - 1), 0), far_body, 0)
    for d in range(B_NEAR - 1, -1, -1):
        @pl.when(i >= d)
        def _(d=d):
            tile(i - d, d)
    _write_gated(o_ref, gate_ref, l_ref, acc_ref)


def _mix_b(qb, kb, z, maskadd, tbl, batch, seq):
    n = qb.shape[0]
    t = ATT_T
    nq = seq // t
    return pl.pallas_call(
        _mix_b_kernel,
        grid=(batch, nq),
        in_specs=[pl.BlockSpec((t, 512), lambda b, i: (b * nq + i, 0)),
                  pl.BlockSpec((seq, 256), lambda b, i: (b, 0)),
                  pl.BlockSpec((seq, 256), lambda b, i: (b, OFF_BV // 256)),
                  pl.BlockSpec((1, t, seq), lambda b, i: (b, i, 0)),
                  pl.BlockSpec((t, 512), lambda b, i: (b * nq + i, OFF_BG // 512)),
                  pl.BlockSpec((N_HEADS, B_NEAR, t, t), lambda b, i: (0, 0, 0, 0))],
        out_specs=pl.BlockSpec((t, 512), lambda b, i: (b * nq + i, 0)),
        out_shape=jax.ShapeDtypeStruct((n, 512), BF16),
        scratch_shapes=[pltpu.VMEM((N_HEADS, t, LANES), F32)] * 3,
        compiler_params=_params(("arbitrary", "arbitrary"), 56),
        name="mix_b",
    )(qb, kb, z, maskadd, z, tbl)


def _mix_c_kernel(q_ref, k_ref, v_ref, cq_ref, ck_ref, gate_ref, o_ref, m_ref, l_ref, acc_ref):
    pair = pl.program_id(1)
    i = pl.program_id(2)
    _init_state(m_ref, l_ref, acc_ref)
    lane = lax.broadcasted_iota(jnp.int32, (ATT_T, LANES), 1)
    qp = q_ref[...]
    qs = [jnp.where(_half_mask(qp.shape, e), qp, jnp.zeros_like(qp)) for e in range(2)]
    cq_all = cq_ref[...]
    cqs = []
    for e in range(2):
        col = MISC_CF + 2 * pair + e
        picked = jnp.sum(jnp.where(lane == col, cq_all, 0.0), axis=1, keepdims=True)
        cqs.append(jnp.broadcast_to(picked, (ATT_T, ATT_T)))
    row = lax.broadcasted_iota(jnp.int32, (ATT_T, ATT_T), 0)
    colk = lax.broadcasted_iota(jnp.int32, (ATT_T, ATT_T), 1)

    def tile(j, diag):
        base = pl.multiple_of(j * ATT_T, ATT_T)
        k = k_ref[pl.ds(base, ATT_T), :]
        v = v_ref[pl.ds(base, ATT_T), :]
        for e in range(2):
            ck = ck_ref[0, pl.ds(2 * pair + e, 1), pl.ds(base, ATT_T)]
            s = _dot_nt(qs[e], k) + cqs[e] - ck
            if diag:
                s = jnp.where(colk <= row, s, NEG)
            _softmax_step(e, s, v, m_ref, l_ref, acc_ref)

    def body(j, carry):
        tile(j, False)
        return carry

    lax.fori_loop(0, i, body, 0)
    tile(i, True)
    o0 = acc_ref[0] / l_ref[0]
    o1 = acc_ref[1] / l_ref[1]
    o = jnp.where(_half_mask(o0.shape, 0), o0, o1)
    g = gate_ref[...].astype(F32)
    o_ref[...] = (o * (g * jax.nn.sigmoid(g))).astype(o_ref.dtype)


def _mix_c(qc, kc, z, cum, cum_t, batch, seq):
    n = qc.shape[0]
    t = ATT_T
    nq = seq // t
    return pl.pallas_call(
        _mix_c_kernel,
        grid=(batch, N_PAIRS, nq),
        in_specs=[pl.BlockSpec((t, LANES), lambda b, p, i: (b * nq + i, p)),
                  pl.BlockSpec((seq, LANES), lambda b, p, i: (b, p)),
                  pl.BlockSpec((seq, LANES), lambda b, p, i: (b, OFF_CV // LANES + p)),
                  pl.BlockSpec((t, LANES), lambda b, p, i: (b * nq + i, 0)),
                  pl.BlockSpec((1, N_HEADS, seq), lambda b, p, i: (b, 0, 0)),
                  pl.BlockSpec((t, LANES), lambda b, p, i: (b * nq + i, OFF_CG // LANES + p))],
        out_specs=pl.BlockSpec((t, LANES), lambda b, p, i: (b * nq + i, p)),
        out_shape=jax.ShapeDtypeStruct((n, 512), BF16),
        scratch_shapes=[pltpu.VMEM((2, t, LANES), F32)] * 3,
        compiler_params=_params(("arbitrary", "arbitrary", "arbitrary"), 32),
        name="mix_c",
    )(qc, kc, z, cum, cum_t, z)


def _merge_kernel(oa, ob, oc, ga, gb, gc, x_ref, wb_ref, wo_ref, o_ref):
    merged = None
    for br, (o, g) in enumerate(((oa, ga), (ob, gb), (oc, gc))):
        y = _dot(o[...], wb_ref[br])
        term = jax.nn.sigmoid(g[...].astype(F32)) * y
        merged = term if merged is None else merged + term
    o_ref[...] = x_ref[...] + _dot(merged.astype(BF16), wo_ref[...])


def _merge(oa, ob, oc, z, x2d, wb, wo):
    n, d = x2d.shape
    tm = 512

    def ospec():
        return pl.BlockSpec((tm, 512), lambda i: (i, 0))

    def gspec(k):
        return pl.BlockSpec((tm, d), lambda i, _k=k: (i, OFF_MERGE // d + _k))

    return pl.pallas_call(
        _merge_kernel,
        grid=(n // tm,),
        in_specs=[ospec(), ospec(), ospec(), gspec(0), gspec(1), gspec(2),
                  pl.BlockSpec((tm, d), lambda i: (i, 0)),
                  pl.BlockSpec((3, 512, d), lambda i: (0, 0, 0)),
                  pl.BlockSpec((d, d), lambda i: (0, 0))],
        out_specs=pl.BlockSpec((tm, d), lambda i: (i, 0)),
        out_shape=jax.ShapeDtypeStruct((n, d), F32),
        compiler_params=_params(("arbitrary",), 48),
        name="merge",
    )(oa, ob, oc, z, z, z, x2d, wb, wo)


def _window_table(bias_a):
    qi = jnp.arange(WINDOW)[:, None]
    ki = jnp.arange(2 * WINDOW)[None, :]
    delta = qi + WINDOW - ki
    band = (delta >= 0) & (delta < WINDOW)
    bias = bias_a[_t5_bucket(delta)].astype(F32).transpose(2, 0, 1)
    return jnp.where(band[None], bias, NEG)


def _near_table(bias_b):
    qi = jnp.arange(ATT_T)[:, None]
    ki = jnp.arange(ATT_T)[None, :]
    d = jnp.arange(B_NEAR)[:, None, None]
    delta = d * ATT_T + qi[None] - ki[None]
    bias = bias_b[_t5_bucket(delta)].astype(F32)
    far = bias_b[N_BUCKETS - 1].astype(F32)
    return (bias - far).transpose(3, 0, 1, 2)


def kernel(x, norm_gain, w_in, b_forget, qk_gain, sinks, w_branch, w_out, rel_bias):
    batch, seq, d = x.shape
    depth = norm_gain.shape[0]
    assert w_in.shape[-1] == IN_COLS and seq % 1024 == 0 and d == 1024
    assert (B_NEAR - 1) * ATT_T - (ATT_T - 1) < T5_FAR <= B_NEAR * ATT_T - (ATT_T - 1)

    cols = jnp.asarray(np.maximum(_COLS, 0), jnp.int32)
    live = jnp.asarray(_COLS >= 0)
    w_perm = jnp.where(live[None, None, :], jnp.take(w_in, cols, axis=2), 0.0).astype(BF16)
    wb = w_branch.astype(BF16)
    wo = w_out.astype(BF16)

    gmat = jnp.asarray(np.kron(np.eye(N_HEADS), np.ones((HEAD_DIM, HEAD_DIM))), BF16)
    prep_t = 512
    lmat = jnp.asarray(np.tril(np.ones((prep_t, prep_t))), BF16)
    umat = jnp.asarray(np.triu(np.ones((SEL_MC, SEL_MC))), BF16)
    tbl_a = _window_table(rel_bias[:, :N_HEADS])
    tbl_b = _near_table(rel_bias[:, N_HEADS:])

    x2d = x.reshape(batch * seq, d)
    for layer in range(depth):
        g = qk_gain[layer]
        gains = jnp.zeros((8, 512), F32)
        gains = gains.at[0].set(jnp.tile(g[0, 0], N_HEADS) * ATTN_SCALE)
        gains = gains.at[1, :256].set(jnp.tile(g[0, 1], 4))
        gains = gains.at[2].set(jnp.tile(g[1, 0], N_HEADS) * ATTN_SCALE)
        gains = gains.at[3, :256].set(jnp.tile(g[1, 1], 4))
        gains = gains.at[4].set(jnp.tile(g[2, 0], N_HEADS) * ATTN_SCALE)
        gains = gains.at[5].set(jnp.tile(g[2, 1], N_HEADS))
        gains = gains.at[6, MISC_CF:MISC_CF + N_HEADS].set(b_forget[layer])

        z, misc = _proj(x2d, norm_gain[layer][None, :], w_perm[layer])
        qa, ka, qb, kb, qc, kc, cum = _prep(z, misc, gains, gmat, lmat, batch, seq)
        cum_t = cum.reshape(batch, seq, LANES)[:, :, MISC_CF:MISC_CF + N_HEADS].transpose(0, 2, 1)
        oa = _mix_a(qa, ka, z, tbl_a, sinks[layer], batch, seq)
        maskadd = _sel_b(z, misc, umat, batch, seq)
        ob = _mix_b(qb, kb, z, maskadd, tbl_b, batch, seq)
        oc = _mix_c(qc, kc, z, cum, cum_t, batch, seq)
        x2d = _merge(oa, ob, oc, z, x2d, wb[layer], wo[layer])
    return x2d.reshape(batch, seq, d)
```

```python
import functools
import math

import numpy as np
import jax
import jax.numpy as jnp
from jax import lax
from jax.experimental import pallas as pl
from jax.experimental.pallas import tpu as pltpu

F32 = jnp.float32
BF16 = jnp.bfloat16

HEAD_DIM = 64
LANES = 128
N_HEADS = 8
N_PAIRS = N_HEADS // 2
KV_GROUP = 4
WINDOW = 128
IDX_HEADS = 4
IDX_DIM = 32
TOPK_MAX = 256
N_BUCKETS = 32
MAX_DISTANCE = 512
EPS = 1e-6
ATTN_SCALE = HEAD_DIM ** -0.5
LOG2E = math.log2(math.e)
NEG = -1e30
T5_FAR = 413

_SIZES = dict(a_q=512, a_k=128, a_v=128, a_gate=512,
              b_q=512, b_k=128, b_v=128, b_iq=128, b_ik=32, b_iw=4, b_gate=512,
              c_q=512, c_k=512, c_v=512, c_f=8, c_gate=512, merge=3072)
_ORIG = {}
_o = 0
for _k, _v in _SIZES.items():
    _ORIG[_k] = _o
    _o += _v
IN_COLS = _o

OFF_MERGE = 0
OFF_AQ, OFF_AG, OFF_BQ, OFF_BG = 3072, 3584, 4096, 4608
OFF_CQ, OFF_CK, OFF_CG = 5120, 5632, 6144
OFF_AK, OFF_BK = 6656, 6912
OFF_BIQ, OFF_BIK, OFF_MISC = 7168, 7296, 7424
NP_COLS = 7680
PROJ_TN = 1536
MISC_CF = 8
VT_C, VT_A, VT_B, VT_ROWS = 0, 512, 640, 768


def _orig(name, idx=None):
    src = np.arange(_SIZES[name]) + _ORIG[name]
    return src if idx is None else src[idx]


def _column_map():
    cols = np.full((NP_COLS,), -1, np.int64)

    def put(off, src):
        cols[off:off + len(src)] = src

    for off, name in ((OFF_MERGE, "merge"), (OFF_AQ, "a_q"), (OFF_AG, "a_gate"), (OFF_BQ, "b_q"),
                      (OFF_BG, "b_gate"), (OFF_CQ, "c_q"), (OFF_CK, "c_k"), (OFF_CG, "c_gate"),
                      (OFF_BIQ, "b_iq")):
        put(off, _orig(name))
    dup = np.concatenate([np.arange(64), np.arange(64), np.arange(64, 128), np.arange(64, 128)])
    put(OFF_AK, _orig("a_k", dup))
    put(OFF_BK, _orig("b_k", dup))
    put(OFF_BIK, _orig("b_ik", np.tile(np.arange(IDX_DIM), IDX_HEADS)))
    put(OFF_MISC + MISC_CF, _orig("c_f"))
    return cols


def _gather_cols(w, cols):
    parts = []
    start = 0
    while start < len(cols):
        end = start + 1
        if cols[start] < 0:
            while end < len(cols) and cols[end] < 0:
                end += 1
            parts.append(jnp.zeros(w.shape[:-1] + (end - start,), w.dtype))
        else:
            while end < len(cols) and cols[end] == cols[end - 1] + 1:
                end += 1
            parts.append(w[..., int(cols[start]):int(cols[end - 1]) + 1])
        start = end
    return jnp.concatenate(parts, axis=-1)


_COLS = _column_map()
_VT_COLS = np.concatenate([_orig("c_v"), _orig("a_v"), _orig("b_v")])
_ST_COLS = np.concatenate([_orig("b_iw"), np.full((4,), -1, np.int64)])


def _params(sem, vmem_mb):
    return pltpu.CompilerParams(dimension_semantics=sem, vmem_limit_bytes=vmem_mb * 1024 * 1024)


def _dot(a, b):
    return jnp.dot(a, b, preferred_element_type=F32)


def _dot_nt(a, b):
    return lax.dot_general(a, b, (((1,), (1,)), ((), ())), preferred_element_type=F32)


def _t5_bucket(delta):
    n = jnp.maximum(delta, 0)
    max_exact = N_BUCKETS // 2
    nf = jnp.maximum(n, 1).astype(F32)
    large = max_exact + (jnp.log(nf / max_exact) / math.log(MAX_DISTANCE / max_exact)
                         * (N_BUCKETS - max_exact)).astype(jnp.int32)
    large = jnp.minimum(large, N_BUCKETS - 1)
    return jnp.where(n < max_exact, n, large)


def _proj_kernel(x_ref, g_ref, w_ref, wvt_ref, wst_ref, z_ref, misc_ref, vt_ref, st_ref, h_ref,
                 *, misc_tile, misc_local):
    j = pl.program_id(1)

    @pl.when(j == 0)
    def _():
        x = x_ref[...]
        ms = jnp.mean(x * x, axis=-1, keepdims=True)
        h = (x * lax.rsqrt(ms + EPS) * g_ref[...]).astype(BF16)
        h_ref[...] = h
        vt_ref[...] = _dot_nt(wvt_ref[...], h).astype(BF16)
        st_ref[...] = _dot_nt(wst_ref[...], h)

    acc = _dot(h_ref[...], w_ref[...])
    z_ref[...] = acc.astype(BF16)

    @pl.when(j == misc_tile)
    def _():
        misc_ref[...] = acc[:, misc_local:misc_local + LANES]


def _proj(x2d, gain, w, wvt, wst):
    n, d = x2d.shape
    tm = 1024
    return pl.pallas_call(
        functools.partial(_proj_kernel, misc_tile=OFF_MISC // PROJ_TN, misc_local=OFF_MISC % PROJ_TN),
        grid=(n // tm, NP_COLS // PROJ_TN),
        in_specs=[pl.BlockSpec((tm, d), lambda i, j: (i, 0)),
                  pl.BlockSpec((1, d), lambda i, j: (0, 0)),
                  pl.BlockSpec((d, PROJ_TN), lambda i, j: (0, j)),
                  pl.BlockSpec((VT_ROWS, d), lambda i, j: (0, 0)),
                  pl.BlockSpec((8, d), lambda i, j: (0, 0))],
        out_specs=[pl.BlockSpec((tm, PROJ_TN), lambda i, j: (i, j)),
                   pl.BlockSpec((tm, LANES), lambda i, j: (i, 0)),
                   pl.BlockSpec((VT_ROWS, tm), lambda i, j: (0, i)),
                   pl.BlockSpec((8, tm), lambda i, j: (0, i))],
        out_shape=[jax.ShapeDtypeStruct((n, NP_COLS), BF16),
                   jax.ShapeDtypeStruct((n, LANES), F32),
                   jax.ShapeDtypeStruct((VT_ROWS, n), BF16),
                   jax.ShapeDtypeStruct((8, n), F32)],
        scratch_shapes=[pltpu.VMEM((tm, d), BF16)],
        compiler_params=_params(("arbitrary", "arbitrary"), 56),
        name="proj",
    )(x2d, gain, w, wvt, wst)


def _group_rms(x_bf16, g_ref, gain_row):
    x = x_bf16.astype(F32)
    width = x.shape[-1]
    sq = x * x
    hi = sq.astype(BF16)
    lo = (sq - hi.astype(F32)).astype(BF16)
    g = g_ref[:width, :width]
    ss = _dot(hi, g) + _dot(lo, g)
    return (x * lax.rsqrt(ss * (1.0 / HEAD_DIM) + EPS) * gain_row).astype(BF16)


def _prep_kernel(aq, ak, bq, bk, cq, ck, misc, gains, gmat, lmat,
                 qa_o, ka_o, qb_o, kb_o, qc_o, kc_o, cum_o, carry):
    t = pl.program_id(1)
    qa_o[...] = _group_rms(aq[...], gmat, gains[0:1, :])
    ka_o[...] = _group_rms(ak[...], gmat, gains[1:2, :256])
    qb_o[...] = _group_rms(bq[...], gmat, gains[2:3, :])
    kb_o[...] = _group_rms(bk[...], gmat, gains[3:4, :256])
    qc_o[...] = _group_rms(cq[...], gmat, gains[4:5, :])
    kc_o[...] = _group_rms(ck[...], gmat, gains[5:6, :])

    @pl.when(t == 0)
    def _():
        carry[...] = jnp.zeros_like(carry)

    xm = misc[...] + gains[6:7, :LANES]
    logf = jnp.minimum(xm, 0.0) - jnp.log(1.0 + jnp.exp(-jnp.abs(xm)))
    hi = logf.astype(BF16)
    r1 = logf - hi.astype(F32)
    mid = r1.astype(BF16)
    lo = (r1 - mid.astype(F32)).astype(BF16)
    lm = lmat[...]
    c = _dot(lm, hi) + _dot(lm, mid) + _dot(lm, lo) + carry[...]
    cum_o[...] = c * LOG2E
    rows = c.shape[0]
    carry[...] = c[rows - 1:rows, :]


def _prep(z, misc, gains, gmat, lmat, batch, seq):
    n = z.shape[0]
    tp = lmat.shape[0]
    nt = seq // tp

    def zspec(width, off):
        return pl.BlockSpec((tp, width), lambda b, t, _c=off // width: (b * nt + t, _c))

    def ospec(width):
        return pl.BlockSpec((tp, width), lambda b, t: (b * nt + t, 0))

    return pl.pallas_call(
        _prep_kernel,
        grid=(batch, nt),
        in_specs=[zspec(512, OFF_AQ), zspec(256, OFF_AK), zspec(512, OFF_BQ), zspec(256, OFF_BK),
                  zspec(512, OFF_CQ), zspec(512, OFF_CK), ospec(LANES),
                  pl.BlockSpec((8, 512), lambda b, t: (0, 0)),
                  pl.BlockSpec((512, 512), lambda b, t: (0, 0)),
                  pl.BlockSpec((tp, tp), lambda b, t: (0, 0))],
        out_specs=[ospec(512), ospec(256), ospec(512), ospec(256), ospec(512), ospec(512), ospec(LANES)],
        out_shape=[jax.ShapeDtypeStruct((n, 512), BF16), jax.ShapeDtypeStruct((n, 256), BF16),
                   jax.ShapeDtypeStruct((n, 512), BF16), jax.ShapeDtypeStruct((n, 256), BF16),
                   jax.ShapeDtypeStruct((n, 512), BF16), jax.ShapeDtypeStruct((n, 512), BF16),
                   jax.ShapeDtypeStruct((n, LANES), F32)],
        scratch_shapes=[pltpu.VMEM((1, LANES), F32)],
        compiler_params=_params(("arbitrary", "arbitrary"), 48),
        name="prep",
    )(z, z, z, z, z, z, misc, gains, gmat, lmat)


def _half_mask(shape, half):
    lane = lax.broadcasted_iota(jnp.int32, shape, len(shape) - 1)
    return (lane < HEAD_DIM) if half == 0 else (lane >= HEAD_DIM)


def _head_q(q_ref, h):
    p = h // 2
    qp = q_ref[:, p * LANES:(p + 1) * LANES]
    return jnp.where(_half_mask(qp.shape, h % 2), qp, jnp.zeros_like(qp))


def _softmax_step(h, s, vt, m_ref, l_ref, acc_ref):
    m_prev = m_ref[h]
    m_new = jnp.maximum(m_prev, jnp.max(s, axis=0, keepdims=True))
    alpha = jnp.exp2(m_prev - m_new)
    p = jnp.exp2(s - m_new[0:1, :])
    l_ref[h] = alpha * l_ref[h] + jnp.sum(p, axis=0, keepdims=True)
    acc_ref[h] = alpha[0:1, :] * acc_ref[h] + _dot(vt, p.astype(BF16))
    m_ref[h] = m_new


HEADS_AHEAD = 4


def _heads_pipelined(logits, values, m_ref, l_ref, acc_ref):
    ahead = [logits(h) for h in range(HEADS_AHEAD)]
    for h in range(N_HEADS):
        if h + HEADS_AHEAD < N_HEADS:
            ahead.append(logits(h + HEADS_AHEAD))
        _softmax_step(h, ahead.pop(0), values(h), m_ref, l_ref, acc_ref)


def _init_state(m_ref, l_ref, acc_ref):
    m_ref[...] = jnp.full(m_ref.shape, NEG, F32)
    l_ref[...] = jnp.zeros(l_ref.shape, F32)
    acc_ref[...] = jnp.zeros(acc_ref.shape, F32)


def _gated_pair(p, gate, l_ref, acc_ref):
    ot = jnp.concatenate([acc_ref[2 * p] / l_ref[2 * p][0:1, :],
                          acc_ref[2 * p + 1] / l_ref[2 * p + 1][0:1, :]], axis=0)
    g = gate.astype(F32)
    return (ot.T * (g * jax.nn.sigmoid(g))).astype(BF16)


def _mix_a_kernel(sink_ref, q_ref, kp_ref, kc_ref, vp_ref, vc_ref, gate_ref, tbl_ref, o_ref,
                  l_ref, acc_ref):
    n = pl.program_id(1)
    pad_pen = jnp.where(n > 0, 0.0, NEG).astype(F32)
    for h in range(N_HEADS):
        kv = h // KV_GROUP
        q = _head_q(q_ref, h)
        ksl = slice(kv * LANES, (kv + 1) * LANES)
        vsl = slice(kv * HEAD_DIM, (kv + 1) * HEAD_DIM)
        s_prev = _dot_nt(kp_ref[:, ksl], q) + tbl_ref[h, :WINDOW, :] + pad_pen
        s_cur = _dot_nt(kc_ref[:, ksl], q) + tbl_ref[h, WINDOW:, :]
        sink = sink_ref[h] * LOG2E
        m = jnp.maximum(jnp.maximum(jnp.max(s_prev, axis=0, keepdims=True),
                                    jnp.max(s_cur, axis=0, keepdims=True)), sink)
        p_prev = jnp.exp2(s_prev - m)
        p_cur = jnp.exp2(s_cur - m)
        l = (jnp.sum(p_prev, axis=0, keepdims=True) + jnp.sum(p_cur, axis=0, keepdims=True)
             + jnp.exp2(sink - m))
        l_ref[h] = jnp.broadcast_to(l, l_ref.shape[1:])
        acc_ref[h] = _dot(vp_ref[vsl, :], p_prev.astype(BF16)) + _dot(vc_ref[vsl, :], p_cur.astype(BF16))
    for p in range(N_PAIRS):
        sl = slice(p * LANES, (p + 1) * LANES)
        o_ref[:, sl] = _gated_pair(p, gate_ref[:, sl], l_ref, acc_ref)


def _mix_a(qa, ka, z, vt, tbl, sinks, batch, seq):
    n = qa.shape[0]
    t = WINDOW
    nb = seq // t
    return pl.pallas_call(
        _mix_a_kernel,
        grid=(batch, nb),
        in_specs=[pl.BlockSpec(memory_space=pltpu.SMEM),
                  pl.BlockSpec((t, 512), lambda b, i: (b * nb + i, 0)),
                  pl.BlockSpec((t, 256), lambda b, i: (b * nb + jnp.maximum(i - 1, 0), 0)),
                  pl.BlockSpec((t, 256), lambda b, i: (b * nb + i, 0)),
                  pl.BlockSpec((LANES, t), lambda b, i: (VT_A // LANES, b * nb + jnp.maximum(i - 1, 0))),
                  pl.BlockSpec((LANES, t), lambda b, i: (VT_A // LANES, b * nb + i)),
                  pl.BlockSpec((t, 512), lambda b, i: (b * nb + i, OFF_AG // 512)),
                  pl.BlockSpec((N_HEADS, 2 * t, t), lambda b, i: (0, 0, 0))],
        out_specs=pl.BlockSpec((t, 512), lambda b, i: (b * nb + i, 0)),
        out_shape=jax.ShapeDtypeStruct((n, 512), BF16),
        scratch_shapes=[pltpu.VMEM((N_HEADS, 8, t), F32), pltpu.VMEM((N_HEADS, HEAD_DIM, t), F32)],
        compiler_params=_params(("arbitrary", "arbitrary"), 32),
        name="mix_a",
    )(sinks, qa, ka, ka, vt, vt, z, tbl)


SEL_TQ = 128
SEL_SC = 512
SEL_SUB = 64
SEL_MC = 256
SEL_FAST_ITERS = 22
SEL_MAX_ITERS = 600


def _sel_b_kernel(iq_ref, ik_ref, wt_ref, lmat_ref, o_ref, sc_ref, *, topk):
    n = pl.program_id(1)
    t0 = n * SEL_TQ
    n_sc = (n + SEL_SC // SEL_TQ) // (SEL_SC // SEL_TQ)
    n_mc = (n + SEL_MC // SEL_TQ) // (SEL_MC // SEL_TQ)
    kf = float(topk)

    iq = iq_ref[...]
    lane = lax.broadcasted_iota(jnp.int32, iq.shape, 1)
    qs = jnp.concatenate(
        [jnp.where((lane >= h * IDX_DIM) & (lane < (h + 1) * IDX_DIM), iq, jnp.zeros_like(iq))
         for h in range(IDX_HEADS)], axis=0)
    wscale = (IDX_HEADS ** -0.5) * (IDX_DIM ** -0.5)
    w = [wt_ref[h:h + 1, :] * wscale for h in range(IDX_HEADS)]
    qpos = t0 + lax.broadcasted_iota(jnp.int32, (SEL_SC, SEL_TQ), 1)
    kiota = lax.broadcasted_iota(jnp.int32, (SEL_SC, SEL_TQ), 0)

    def score_body(c, carry):
        rmax, rmin = carry
        base = pl.multiple_of(c * SEL_SC, SEL_SC)
        raw = _dot_nt(ik_ref[pl.ds(base, SEL_SC), :], qs)
        score = w[0] * jnp.maximum(raw[:, 0:SEL_TQ], 0.0)
        for h in range(1, IDX_HEADS):
            score = score + w[h] * jnp.maximum(raw[:, h * SEL_TQ:(h + 1) * SEL_TQ], 0.0)
        causal = (kiota + base) <= qpos
        sc_ref[pl.ds(base, SEL_SC), :] = jnp.where(causal, score, -jnp.inf)
        rmax = jnp.maximum(rmax, jnp.max(jnp.where(causal, score, -jnp.inf), axis=0, keepdims=True))
        rmin = jnp.minimum(rmin, jnp.min(jnp.where(causal, score, jnp.inf), axis=0, keepdims=True))
        return rmax, rmin

    rmax, rmin = lax.fori_loop(
        0, n_sc, score_body,
        (jnp.full((1, SEL_TQ), -jnp.inf, F32), jnp.full((1, SEL_TQ), jnp.inf, F32)))

    navail = (t0 + 1 + lax.broadcasted_iota(jnp.int32, (1, SEL_TQ), 1)).astype(F32)

    def sweep(mid, snap):
        midb = jnp.broadcast_to(mid, (SEL_SUB, SEL_TQ))

        def body(c, carry):
            base = pl.multiple_of(c * SEL_SC, SEL_SC)
            cnt, up, dn = carry
            for u in range(SEL_SC // SEL_SUB):
                x = sc_ref[pl.ds(base + u * SEL_SUB, SEL_SUB), :]
                ge = x >= midb
                cnt = cnt + jnp.where(ge, 1.0, 0.0)
                if snap:
                    up = jnp.minimum(up, jnp.where(ge, x, jnp.inf))
                    dn = jnp.maximum(dn, jnp.where(ge, -jnp.inf, x))
            return cnt, up, dn

        init = (jnp.zeros((SEL_SUB, SEL_TQ), F32), jnp.full((SEL_SUB, SEL_TQ), jnp.inf, F32),
                jnp.full((SEL_SUB, SEL_TQ), -jnp.inf, F32))
        cnt, up, dn = lax.fori_loop(0, n_sc, body, init)
        return (jnp.sum(cnt, axis=0, keepdims=True), jnp.min(up, axis=0, keepdims=True),
                jnp.max(dn, axis=0, keepdims=True))

    def advance(st, mid, snap, first=False):
        it, lo, hi, hi_dn, c_lo, c_hi, done = st
        cnt, up, dn = sweep(mid, snap)
        live = jnp.where(done > 0.5, 0.0, 1.0)
        if not first:
            live = jnp.where(mid <= lo, 0.0, jnp.where(mid >= hi, 0.0, live))
        ge = jnp.where(cnt >= kf, live, 0.0) > 0.5
        lt = jnp.where(cnt >= kf, 0.0, live) > 0.5
        lo = jnp.where(ge, up if snap else mid, lo)
        c_lo = jnp.where(ge, cnt, c_lo)
        hi = jnp.where(lt, mid, hi)
        hi_dn = jnp.where(lt, dn if snap else jnp.inf, hi_dn)
        c_hi = jnp.where(lt, cnt, c_hi)
        done = jnp.where(live < 0.5, 1.0, jnp.where(cnt == kf, 1.0, jnp.where(hi_dn <= lo, 1.0, 0.0)))
        return it + 1, lo, hi, hi_dn, c_lo, c_hi, done

    def make_step(snap):
        def step(st):
            return advance(st, 0.5 * st[1] + 0.5 * st[2], snap)
        return step

    def make_cond(limit):
        def cond(st):
            return jnp.logical_and(st[0] < limit, jnp.min(st[6]) < 0.5)
        return cond

    st = (jnp.int32(0), rmin, jnp.full((1, SEL_TQ), jnp.inf, F32), rmax, navail,
          jnp.zeros((1, SEL_TQ), F32), jnp.where(navail <= kf, 1.0, 0.0))
    st = advance(st, rmax, False, first=True)
    st = lax.while_loop(make_cond(SEL_FAST_ITERS), make_step(False), st)
    st = lax.while_loop(make_cond(SEL_MAX_ITERS), make_step(True), st)
    _, lo, hi, _, c_lo, c_hi, _ = st
    need = kf - c_hi
    ties = jnp.max(jnp.where(navail > kf, c_lo, kf)) > kf

    o_ref[...] = jnp.full(o_ref.shape, NEG, o_ref.dtype)

    @pl.when(jnp.logical_not(ties))
    def _():
        def mask_body(c, carry):
            base = pl.multiple_of(c * SEL_MC, SEL_MC)
            x = sc_ref[pl.ds(base, SEL_MC), :]
            o_ref[0, pl.ds(base, SEL_MC), :] = jnp.where(x >= lo, 0.0, NEG).astype(o_ref.dtype)
            return carry
        lax.fori_loop(0, n_mc, mask_body, 0)

    @pl.when(ties)
    def _():
        lmat = lmat_ref[...]

        def mask_body(c, run):
            base = pl.multiple_of(c * SEL_MC, SEL_MC)
            x = sc_ref[pl.ds(base, SEL_MC), :]
            bnd = jnp.where(x >= lo, jnp.where(x < hi, 1.0, 0.0), 0.0)
            incl = _dot(lmat, bnd.astype(BF16)) + run
            keep = jnp.where(x >= hi, 1.0, jnp.where(incl - bnd < need, bnd, 0.0))
            o_ref[0, pl.ds(base, SEL_MC), :] = jnp.where(keep > 0.5, 0.0, NEG).astype(o_ref.dtype)
            return incl[SEL_MC - 1:SEL_MC, :]
        lax.fori_loop(0, n_mc, mask_body, jnp.zeros((1, SEL_TQ), F32))


def _sel_b(z, st, lmat, batch, seq):
    nb = seq // SEL_TQ
    topk = min(TOPK_MAX, seq // 4)
    return pl.pallas_call(
        functools.partial(_sel_b_kernel, topk=topk),
        grid=(batch, nb),
        in_specs=[pl.BlockSpec((SEL_TQ, LANES), lambda b, i: (b * nb + i, OFF_BIQ // LANES)),
                  pl.BlockSpec((seq, LANES), lambda b, i: (b, OFF_BIK // LANES)),
                  pl.BlockSpec((8, SEL_TQ), lambda b, i: (0, b * nb + i)),
                  pl.BlockSpec((SEL_MC, SEL_MC), lambda b, i: (0, 0))],
        out_specs=pl.BlockSpec((1, seq, SEL_TQ), lambda b, i: (b, 0, i)),
        out_shape=jax.ShapeDtypeStruct((batch, seq, seq), BF16),
        scratch_shapes=[pltpu.VMEM((seq, SEL_TQ), F32)],
        compiler_params=_params(("arbitrary", "arbitrary"), 48),
        name="sel_b",
    )(z, z, st, lmat)


ATT_T = 256
B_NEAR = 3


def _mix_b_kernel(q_ref, k_ref, vt_ref, mask_ref, gate_ref, tbl_ref, o_ref, m_ref, l_ref, acc_ref):
    i = pl.program_id(1)
    _init_state(m_ref, l_ref, acc_ref)
    qs = [_head_q(q_ref, h) for h in range(N_HEADS)]

    def tile(j, near):
        base = pl.multiple_of(j * ATT_T, ATT_T)
        madd = mask_ref[0, pl.ds(base, ATT_T), :].astype(F32)

        def logits(h):
            kv = h // KV_GROUP
            s = _dot_nt(k_ref[pl.ds(base, ATT_T), kv * LANES:(kv + 1) * LANES], qs[h]) + madd
            return s if near is None else s + tbl_ref[h, near]

        def values(h):
            kv = h // KV_GROUP
            return vt_ref[kv * HEAD_DIM:(kv + 1) * HEAD_DIM, pl.ds(base, ATT_T)]

        _heads_pipelined(logits, values, m_ref, l_ref, acc_ref)

    def far_body(j, carry):
        tile(j, None)
        return carry

    lax.fori_loop(0, jnp.maximum(i - (B_NEAR - 1), 0), far_body, 0)
    for d in range(B_NEAR - 1, -1, -1):
        @pl.when(i >= d)
        def _(d=d):
            tile(i - d, d)
    for p in range(N_PAIRS):
        sl = slice(p * LANES, (p + 1) * LANES)
        o_ref[:, sl] = _gated_pair(p, gate_ref[:, sl], l_ref, acc_ref)


def _mix_b(qb, kb, z, vt, maskadd, tbl, batch, seq):
    n = qb.shape[0]
    t = ATT_T
    nq = seq // t
    return pl.pallas_call(
        _mix_b_kernel,
        grid=(batch, nq),
        in_specs=[pl.BlockSpec((t, 512), lambda b, i: (b * nq + i, 0)),
                  pl.BlockSpec((seq, 256), lambda b, i: (b, 0)),
                  pl.BlockSpec((LANES, seq), lambda b, i: (VT_B // LANES, b)),
                  pl.BlockSpec((1, seq, t), lambda b, i: (b, 0, i)),
                  pl.BlockSpec((t, 512), lambda b, i: (b * nq + i, OFF_BG // 512)),
                  pl.BlockSpec((N_HEADS, B_NEAR, t, t), lambda b, i: (0, 0, 0, 0))],
        out_specs=pl.BlockSpec((t, 512), lambda b, i: (b * nq + i, 0)),
        out_shape=jax.ShapeDtypeStruct((n, 512), BF16),
        scratch_shapes=[pltpu.VMEM((N_HEADS, 8, t), F32), pltpu.VMEM((N_HEADS, 8, t), F32),
                        pltpu.VMEM((N_HEADS, HEAD_DIM, t), F32)],
        compiler_params=_params(("arbitrary", "arbitrary"), 56),
        name="mix_b",
    )(qb, kb, vt, maskadd, z, tbl)


def _mix_c_kernel(q_ref, k_ref, vt_ref, ck_ref, cq_ref, gate_ref, o_ref, m_ref, l_ref, acc_ref):
    i = pl.program_id(1)
    _init_state(m_ref, l_ref, acc_ref)
    qs = [_head_q(q_ref, h) for h in range(N_HEADS)]
    q0 = pl.multiple_of(i * ATT_T, ATT_T)
    cq = cq_ref[0, :, pl.ds(q0, ATT_T)]
    krow = lax.broadcasted_iota(jnp.int32, (ATT_T, ATT_T), 0)
    qcol = lax.broadcasted_iota(jnp.int32, (ATT_T, ATT_T), 1)

    def tile(j, diag):
        base = pl.multiple_of(j * ATT_T, ATT_T)
        ck_all = ck_ref[pl.ds(base, ATT_T), :]

        def logits(h):
            p = h // 2
            ck = ck_all[:, MISC_CF + h:MISC_CF + h + 1]
            s = _dot_nt(k_ref[pl.ds(base, ATT_T), p * LANES:(p + 1) * LANES], qs[h]) + cq[h:h + 1, :] - ck
            return jnp.where(krow <= qcol, s, NEG) if diag else s

        def values(h):
            return vt_ref[h * HEAD_DIM:(h + 1) * HEAD_DIM, pl.ds(base, ATT_T)]

        _heads_pipelined(logits, values, m_ref, l_ref, acc_ref)

    def body(j, carry):
        tile(j, False)
        return carry

    lax.fori_loop(0, i, body, 0)
    tile(i, True)
    for p in range(N_PAIRS):
        sl = slice(p * LANES, (p + 1) * LANES)
        o_ref[:, sl] = _gated_pair(p, gate_ref[:, sl], l_ref, acc_ref)


def _mix_c(qc, kc, z, vt, cum, cum_t, batch, seq):
    n = qc.shape[0]
    t = ATT_T
    nq = seq // t
    return pl.pallas_call(
        _mix_c_kernel,
        grid=(batch, nq),
        in_specs=[pl.BlockSpec((t, 512), lambda b, i: (b * nq + i, 0)),
                  pl.BlockSpec((seq, 512), lambda b, i: (b, 0)),
                  pl.BlockSpec((512, seq), lambda b, i: (VT_C // 512, b)),
                  pl.BlockSpec((seq, LANES), lambda b, i: (b, 0)),
                  pl.BlockSpec((1, N_HEADS, seq), lambda b, i: (b, 0, 0)),
                  pl.BlockSpec((t, 512), lambda b, i: (b * nq + i, OFF_CG // 512))],
        out_specs=pl.BlockSpec((t, 512), lambda b, i: (b * nq + i, 0)),
        out_shape=jax.ShapeDtypeStruct((n, 512), BF16),
        scratch_shapes=[pltpu.VMEM((N_HEADS, 8, t), F32), pltpu.VMEM((N_HEADS, 8, t), F32),
                        pltpu.VMEM((N_HEADS, HEAD_DIM, t), F32)],
        compiler_params=_params(("arbitrary", "arbitrary"), 56),
        name="mix_c",
    )(qc, kc, vt, cum, cum_t, z)


def _merge_kernel(oa, ob, oc, ga, gb, gc, x_ref, wb_ref, wo_ref, o_ref):
    merged = None
    for br, (o, g) in enumerate(((oa, ga), (ob, gb), (oc, gc))):
        y = _dot(o[...], wb_ref[br])
        term = jax.nn.sigmoid(g[...].astype(F32)) * y
        merged = term if merged is None else merged + term
    o_ref[...] = x_ref[...] + _dot(merged.astype(BF16), wo_ref[...])


def _merge(oa, ob, oc, z, x2d, wb, wo):
    n, d = x2d.shape
    tm = 512

    def ospec():
        return pl.BlockSpec((tm, 512), lambda i: (i, 0))

    def gspec(k):
        return pl.BlockSpec((tm, d), lambda i, _k=k: (i, OFF_MERGE // d + _k))

    return pl.pallas_call(
        _merge_kernel,
        grid=(n // tm,),
        in_specs=[ospec(), ospec(), ospec(), gspec(0), gspec(1), gspec(2),
                  pl.BlockSpec((tm, d), lambda i: (i, 0)),
                  pl.BlockSpec((3, 512, d), lambda i: (0, 0, 0)),
                  pl.BlockSpec((d, d), lambda i: (0, 0))],
        out_specs=pl.BlockSpec((tm, d), lambda i: (i, 0)),
        out_shape=jax.ShapeDtypeStruct((n, d), F32),
        compiler_params=_params(("arbitrary",), 48),
        name="merge",
    )(oa, ob, oc, z, z, z, x2d, wb, wo)


def _window_table(bias_a):
    ki = jnp.arange(2 * WINDOW)[:, None]
    qi = jnp.arange(WINDOW)[None, :]
    delta = qi + WINDOW - ki
    band = (delta >= 0) & (delta < WINDOW)
    bias = bias_a[_t5_bucket(delta)].astype(F32).transpose(2, 0, 1) * LOG2E
    return jnp.where(band[None], bias, NEG)


def _near_table(bias_b):
    ki = jnp.arange(ATT_T)[:, None]
    qi = jnp.arange(ATT_T)[None, :]
    d = jnp.arange(B_NEAR)[:, None, None]
    delta = d * ATT_T + qi[None] - ki[None]
    bias = bias_b[_t5_bucket(delta)].astype(F32)
    far = bias_b[N_BUCKETS - 1].astype(F32)
    return ((bias - far) * LOG2E).transpose(3, 0, 1, 2)


def kernel(x, norm_gain, w_in, b_forget, qk_gain, sinks, w_branch, w_out, rel_bias):
    batch, seq, d = x.shape
    depth = norm_gain.shape[0]
    assert w_in.shape[-1] == IN_COLS and seq % 1024 == 0 and d == 1024
    assert (B_NEAR - 1) * ATT_T - (ATT_T - 1) < T5_FAR <= B_NEAR * ATT_T - (ATT_T - 1)

    w_perm = _gather_cols(w_in, _COLS).astype(BF16)
    w_vt = jnp.swapaxes(_gather_cols(w_in, _VT_COLS), 1, 2).astype(BF16)
    w_st = jnp.swapaxes(_gather_cols(w_in, _ST_COLS), 1, 2).astype(BF16)
    wb = w_branch.astype(BF16)
    wo = w_out.astype(BF16)

    gmat = jnp.asarray(np.kron(np.eye(N_HEADS), np.ones((HEAD_DIM, HEAD_DIM))), BF16)
    prep_t = 512
    lmat = jnp.asarray(np.tril(np.ones((prep_t, prep_t))), BF16)
    lmat_sel = lmat[:SEL_MC, :SEL_MC]
    tbl_a = _window_table(rel_bias[:, :N_HEADS])
    tbl_b = _near_table(rel_bias[:, N_HEADS:])

    x2d = x.reshape(batch * seq, d)
    for layer in range(depth):
        g = qk_gain[layer]
        qscale = ATTN_SCALE * LOG2E
        gains = jnp.zeros((8, 512), F32)
        gains = gains.at[0].set(jnp.tile(g[0, 0], N_HEADS) * qscale)
        gains = gains.at[1, :256].set(jnp.tile(g[0, 1], 4))
        gains = gains.at[2].set(jnp.tile(g[1, 0], N_HEADS) * qscale)
        gains = gains.at[3, :256].set(jnp.tile(g[1, 1], 4))
        gains = gains.at[4].set(jnp.tile(g[2, 0], N_HEADS) * qscale)
        gains = gains.at[5].set(jnp.tile(g[2, 1], N_HEADS))
        gains = gains.at[6, MISC_CF:MISC_CF + N_HEADS].set(b_forget[layer])

        z, misc, vt, st = _proj(x2d, norm_gain[layer][None, :], w_perm[layer], w_vt[layer], w_st[layer])
        qa, ka, qb, kb, qc, kc, cum = _prep(z, misc, gains, gmat, lmat, batch, seq)
        cum_t = cum.reshape(batch, seq, LANES)[:, :, MISC_CF:MISC_CF + N_HEADS].transpose(0, 2, 1)
        oa = _mix_a(qa, ka, z, vt, tbl_a, sinks[layer], batch, seq)
        maskadd = _sel_b(z, st, lmat_sel, batch, seq)
        ob = _mix_b(qb, kb, z, vt, maskadd, tbl_b, batch, seq)
        oc = _mix_c(qc, kc, z, vt, cum, cum_t, batch, seq)
        x2d = _merge(oa, ob, oc, z, x2d, wb[layer], wo[layer])
    return x2d.reshape(batch, seq, d)
```

```python
import functools
import math

import numpy as np
import jax
import jax.numpy as jnp
from jax import lax
from jax.experimental import pallas as pl
from jax.experimental.pallas import tpu as pltpu

F32 = jnp.float32
BF16 = jnp.bfloat16

HEAD_DIM = 64
LANES = 128
N_HEADS = 8
N_PAIRS = N_HEADS // 2
KV_GROUP = 4
WINDOW = 128
IDX_HEADS = 4
IDX_DIM = 32
TOPK_MAX = 256
N_BUCKETS = 32
MAX_DISTANCE = 512
EPS = 1e-6
ATTN_SCALE = HEAD_DIM ** -0.5
LOG2E = math.log2(math.e)
NEG = -1e30
T5_FAR = 413

_SIZES = dict(a_q=512, a_k=128, a_v=128, a_gate=512,
              b_q=512, b_k=128, b_v=128, b_iq=128, b_ik=32, b_iw=4, b_gate=512,
              c_q=512, c_k=512, c_v=512, c_f=8, c_gate=512, merge=3072)
_ORIG = {}
_o = 0
for _k, _v in _SIZES.items():
    _ORIG[_k] = _o
    _o += _v
IN_COLS = _o

OFF_MERGE = 0
OFF_AQ, OFF_AG, OFF_BQ, OFF_BG = 3072, 3584, 4096, 4608
OFF_CQ, OFF_CK, OFF_CG = 5120, 5632, 6144
OFF_AK, OFF_BK = 6656, 6912
OFF_BIQ, OFF_BIK, OFF_MISC = 7168, 7296, 7424
NP_COLS = 7680
PROJ_TN = 1536
MISC_CF = 8
VT_C, VT_A, VT_B, VT_ROWS = 0, 512, 640, 768


def _orig(name, idx=None):
    src = np.arange(_SIZES[name]) + _ORIG[name]
    return src if idx is None else src[idx]


def _column_map():
    cols = np.full((NP_COLS,), -1, np.int64)

    def put(off, src):
        cols[off:off + len(src)] = src

    for off, name in ((OFF_MERGE, "merge"), (OFF_AQ, "a_q"), (OFF_AG, "a_gate"), (OFF_BQ, "b_q"),
                      (OFF_BG, "b_gate"), (OFF_CQ, "c_q"), (OFF_CK, "c_k"), (OFF_CG, "c_gate"),
                      (OFF_BIQ, "b_iq")):
        put(off, _orig(name))
    dup = np.concatenate([np.arange(64), np.arange(64), np.arange(64, 128), np.arange(64, 128)])
    put(OFF_AK, _orig("a_k", dup))
    put(OFF_BK, _orig("b_k", dup))
    put(OFF_BIK, _orig("b_ik", np.tile(np.arange(IDX_DIM), IDX_HEADS)))
    put(OFF_MISC + MISC_CF, _orig("c_f"))
    return cols


def _gather_cols(w, cols):
    parts = []
    start = 0
    while start < len(cols):
        end = start + 1
        if cols[start] < 0:
            while end < len(cols) and cols[end] < 0:
                end += 1
            parts.append(jnp.zeros(w.shape[:-1] + (end - start,), w.dtype))
        else:
            while end < len(cols) and cols[end] == cols[end - 1] + 1:
                end += 1
            parts.append(w[..., int(cols[start]):int(cols[end - 1]) + 1])
        start = end
    return jnp.concatenate(parts, axis=-1)


_COLS = _column_map()
_VT_COLS = np.concatenate([_orig("c_v"), _orig("a_v"), _orig("b_v")])
_ST_COLS = np.concatenate([_orig("b_iw"), np.full((4,), -1, np.int64)])


def _params(sem, vmem_mb):
    return pltpu.CompilerParams(dimension_semantics=sem, vmem_limit_bytes=vmem_mb * 1024 * 1024)


def _dot(a, b):
    return jnp.dot(a, b, preferred_element_type=F32)


def _dot_nt(a, b):
    return lax.dot_general(a, b, (((1,), (1,)), ((), ())), preferred_element_type=F32)


def _t5_bucket(delta):
    n = jnp.maximum(delta, 0)
    max_exact = N_BUCKETS // 2
    nf = jnp.maximum(n, 1).astype(F32)
    large = max_exact + (jnp.log(nf / max_exact) / math.log(MAX_DISTANCE / max_exact)
                         * (N_BUCKETS - max_exact)).astype(jnp.int32)
    large = jnp.minimum(large, N_BUCKETS - 1)
    return jnp.where(n < max_exact, n, large)


def _proj_kernel(x_ref, g_ref, w_ref, wvt_ref, wst_ref, z_ref, misc_ref, vt_ref, st_ref, h_ref,
                 *, misc_tile, misc_local):
    j = pl.program_id(1)

    @pl.when(j == 0)
    def _():
        x = x_ref[...]
        ms = jnp.mean(x * x, axis=-1, keepdims=True)
        h = (x * lax.rsqrt(ms + EPS) * g_ref[...]).astype(BF16)
        h_ref[...] = h
        vt_ref[...] = _dot_nt(wvt_ref[...], h).astype(BF16)
        st_ref[...] = _dot_nt(wst_ref[...], h)

    acc = _dot(h_ref[...], w_ref[...])
    z_ref[...] = acc.astype(BF16)

    @pl.when(j == misc_tile)
    def _():
        misc_ref[...] = acc[:, misc_local:misc_local + LANES]


def _proj(x2d, gain, w, wvt, wst):
    n, d = x2d.shape
    tm = 1024
    return pl.pallas_call(
        functools.partial(_proj_kernel, misc_tile=OFF_MISC // PROJ_TN, misc_local=OFF_MISC % PROJ_TN),
        grid=(n // tm, NP_COLS // PROJ_TN),
        in_specs=[pl.BlockSpec((tm, d), lambda i, j: (i, 0)),
                  pl.BlockSpec((1, d), lambda i, j: (0, 0)),
                  pl.BlockSpec((d, PROJ_TN), lambda i, j: (0, j)),
                  pl.BlockSpec((VT_ROWS, d), lambda i, j: (0, 0)),
                  pl.BlockSpec((8, d), lambda i, j: (0, 0))],
        out_specs=[pl.BlockSpec((tm, PROJ_TN), lambda i, j: (i, j)),
                   pl.BlockSpec((tm, LANES), lambda i, j: (i, 0)),
                   pl.BlockSpec((VT_ROWS, tm), lambda i, j: (0, i)),
                   pl.BlockSpec((8, tm), lambda i, j: (0, i))],
        out_shape=[jax.ShapeDtypeStruct((n, NP_COLS), BF16),
                   jax.ShapeDtypeStruct((n, LANES), F32),
                   jax.ShapeDtypeStruct((VT_ROWS, n), BF16),
                   jax.ShapeDtypeStruct((8, n), F32)],
        scratch_shapes=[pltpu.VMEM((tm, d), BF16)],
        compiler_params=_params(("arbitrary", "arbitrary"), 56),
        name="proj",
    )(x2d, gain, w, wvt, wst)


def _group_rms(x_bf16, g_ref, gain_row):
    x = x_bf16.astype(F32)
    width = x.shape[-1]
    sq = x * x
    hi = sq.astype(BF16)
    lo = (sq - hi.astype(F32)).astype(BF16)
    g = g_ref[:width, :width]
    ss = _dot(hi, g) + _dot(lo, g)
    return (x * lax.rsqrt(ss * (1.0 / HEAD_DIM) + EPS) * gain_row).astype(BF16)


def _prep_kernel(aq, ak, bq, bk, cq, ck, misc, gains, gmat, lmat,
                 qa_o, ka_o, qb_o, kb_o, qc_o, kc_o, cum_o, carry):
    t = pl.program_id(1)
    qa_o[...] = _group_rms(aq[...], gmat, gains[0:1, :])
    ka_o[...] = _group_rms(ak[...], gmat, gains[1:2, :256])
    qb_o[...] = _group_rms(bq[...], gmat, gains[2:3, :])
    kb_o[...] = _group_rms(bk[...], gmat, gains[3:4, :256])
    qc_o[...] = _group_rms(cq[...], gmat, gains[4:5, :])
    kc_o[...] = _group_rms(ck[...], gmat, gains[5:6, :])

    @pl.when(t == 0)
    def _():
        carry[...] = jnp.zeros_like(carry)

    xm = misc[...] + gains[6:7, :LANES]
    logf = jnp.minimum(xm, 0.0) - jnp.log(1.0 + jnp.exp(-jnp.abs(xm)))
    hi = logf.astype(BF16)
    r1 = logf - hi.astype(F32)
    mid = r1.astype(BF16)
    lo = (r1 - mid.astype(F32)).astype(BF16)
    lm = lmat[...]
    c = _dot(lm, hi) + _dot(lm, mid) + _dot(lm, lo) + carry[...]
    cum_o[...] = c * LOG2E
    rows = c.shape[0]
    carry[...] = c[rows - 1:rows, :]


def _prep(z, misc, gains, gmat, lmat, batch, seq):
    n = z.shape[0]
    tp = lmat.shape[0]
    nt = seq // tp

    def zspec(width, off):
        return pl.BlockSpec((tp, width), lambda b, t, _c=off // width: (b * nt + t, _c))

    def ospec(width):
        return pl.BlockSpec((tp, width), lambda b, t: (b * nt + t, 0))

    return pl.pallas_call(
        _prep_kernel,
        grid=(batch, nt),
        in_specs=[zspec(512, OFF_AQ), zspec(256, OFF_AK), zspec(512, OFF_BQ), zspec(256, OFF_BK),
                  zspec(512, OFF_CQ), zspec(512, OFF_CK), ospec(LANES),
                  pl.BlockSpec((8, 512), lambda b, t: (0, 0)),
                  pl.BlockSpec((512, 512), lambda b, t: (0, 0)),
                  pl.BlockSpec((tp, tp), lambda b, t: (0, 0))],
        out_specs=[ospec(512), ospec(256), ospec(512), ospec(256), ospec(512), ospec(512), ospec(LANES)],
        out_shape=[jax.ShapeDtypeStruct((n, 512), BF16), jax.ShapeDtypeStruct((n, 256), BF16),
                   jax.ShapeDtypeStruct((n, 512), BF16), jax.ShapeDtypeStruct((n, 256), BF16),
                   jax.ShapeDtypeStruct((n, 512), BF16), jax.ShapeDtypeStruct((n, 512), BF16),
                   jax.ShapeDtypeStruct((n, LANES), F32)],
        scratch_shapes=[pltpu.VMEM((1, LANES), F32)],
        compiler_params=_params(("arbitrary", "arbitrary"), 48),
        name="prep",
    )(z, z, z, z, z, z, misc, gains, gmat, lmat)


def _half_mask(shape, half):
    lane = lax.broadcasted_iota(jnp.int32, shape, len(shape) - 1)
    return (lane < HEAD_DIM) if half == 0 else (lane >= HEAD_DIM)


def _head_q(q_ref, h):
    p = h // 2
    qp = q_ref[:, p * LANES:(p + 1) * LANES]
    return jnp.where(_half_mask(qp.shape, h % 2), qp, jnp.zeros_like(qp))


def _softmax_step(h, s, vt, m_ref, l_ref, acc_ref):
    m_prev = m_ref[h]
    m_new = jnp.maximum(m_prev, jnp.max(s, axis=0, keepdims=True))
    alpha = jnp.exp2(m_prev - m_new)
    p = jnp.exp2(s - m_new[0:1, :])
    l_ref[h] = alpha * l_ref[h] + jnp.sum(p, axis=0, keepdims=True)
    acc_ref[h] = alpha[0:1, :] * acc_ref[h] + _dot(vt, p.astype(BF16))
    m_ref[h] = m_new


HEADS_AHEAD = 4


def _heads_pipelined(logits, values, m_ref, l_ref, acc_ref):
    ahead = [logits(h) for h in range(HEADS_AHEAD)]
    for h in range(N_HEADS):
        if h + HEADS_AHEAD < N_HEADS:
            ahead.append(logits(h + HEADS_AHEAD))
        _softmax_step(h, ahead.pop(0), values(h), m_ref, l_ref, acc_ref)


def _init_state(m_ref, l_ref, acc_ref):
    m_ref[...] = jnp.full(m_ref.shape, NEG, F32)
    l_ref[...] = jnp.zeros(l_ref.shape, F32)
    acc_ref[...] = jnp.zeros(acc_ref.shape, F32)


def _gated_pair(p, gate, l_ref, acc_ref):
    ot = jnp.concatenate([acc_ref[2 * p] / l_ref[2 * p][0:1, :],
                          acc_ref[2 * p + 1] / l_ref[2 * p + 1][0:1, :]], axis=0)
    g = gate.astype(F32)
    return (ot.T * (g * jax.nn.sigmoid(g))).astype(BF16)


def _mix_a_kernel(sink_ref, q_ref, kp_ref, kc_ref, vp_ref, vc_ref, gate_ref, tbl_ref, o_ref,
                  l_ref, acc_ref):
    n = pl.program_id(1)
    pad_pen = jnp.where(n > 0, 0.0, NEG).astype(F32)
    for h in range(N_HEADS):
        kv = h // KV_GROUP
        q = _head_q(q_ref, h)
        ksl = slice(kv * LANES, (kv + 1) * LANES)
        vsl = slice(kv * HEAD_DIM, (kv + 1) * HEAD_DIM)
        s_prev = _dot_nt(kp_ref[:, ksl], q) + tbl_ref[h, :WINDOW, :] + pad_pen
        s_cur = _dot_nt(kc_ref[:, ksl], q) + tbl_ref[h, WINDOW:, :]
        sink = sink_ref[h] * LOG2E
        m = jnp.maximum(jnp.maximum(jnp.max(s_prev, axis=0, keepdims=True),
                                    jnp.max(s_cur, axis=0, keepdims=True)), sink)
        p_prev = jnp.exp2(s_prev - m)
        p_cur = jnp.exp2(s_cur - m)
        l = (jnp.sum(p_prev, axis=0, keepdims=True) + jnp.sum(p_cur, axis=0, keepdims=True)
             + jnp.exp2(sink - m))
        l_ref[h] = jnp.broadcast_to(l, l_ref.shape[1:])
        acc_ref[h] = _dot(vp_ref[vsl, :], p_prev.astype(BF16)) + _dot(vc_ref[vsl, :], p_cur.astype(BF16))
    for p in range(N_PAIRS):
        sl = slice(p * LANES, (p + 1) * LANES)
        o_ref[:, sl] = _gated_pair(p, gate_ref[:, sl], l_ref, acc_ref)


def _mix_a(qa, ka, z, vt, tbl, sinks, batch, seq):
    n = qa.shape[0]
    t = WINDOW
    nb = seq // t
    return pl.pallas_call(
        _mix_a_kernel,
        grid=(batch, nb),
        in_specs=[pl.BlockSpec(memory_space=pltpu.SMEM),
                  pl.BlockSpec((t, 512), lambda b, i: (b * nb + i, 0)),
                  pl.BlockSpec((t, 256), lambda b, i: (b * nb + jnp.maximum(i - 1, 0), 0)),
                  pl.BlockSpec((t, 256), lambda b, i: (b * nb + i, 0)),
                  pl.BlockSpec((LANES, t), lambda b, i: (VT_A // LANES, b * nb + jnp.maximum(i - 1, 0))),
                  pl.BlockSpec((LANES, t), lambda b, i: (VT_A // LANES, b * nb + i)),
                  pl.BlockSpec((t, 512), lambda b, i: (b * nb + i, OFF_AG // 512)),
                  pl.BlockSpec((N_HEADS, 2 * t, t), lambda b, i: (0, 0, 0))],
        out_specs=pl.BlockSpec((t, 512), lambda b, i: (b * nb + i, 0)),
        out_shape=jax.ShapeDtypeStruct((n, 512), BF16),
        scratch_shapes=[pltpu.VMEM((N_HEADS, 8, t), F32), pltpu.VMEM((N_HEADS, HEAD_DIM, t), F32)],
        compiler_params=_params(("arbitrary", "arbitrary"), 32),
        name="mix_a",
    )(sinks, qa, ka, ka, vt, vt, z, tbl)


SEL_TQ = 256
SEL_SC = 512
SEL_SUB = 64
SEL_MC = 256
SEL_FAST_ITERS = 24
SEL_MAX_ITERS = 600


def _sel_b_kernel(iq_ref, ik_ref, wt_ref, lmat_ref, o_ref, sc_ref, *, topk):
    n = pl.program_id(1)
    t0 = n * SEL_TQ
    n_full = n // (SEL_SC // SEL_TQ)
    n_sc = n_full + 1
    n_mc = n + 1
    kf = float(topk)

    iq = iq_ref[...]
    lane = lax.broadcasted_iota(jnp.int32, iq.shape, 1)
    qs = jnp.concatenate(
        [jnp.where((lane >= h * IDX_DIM) & (lane < (h + 1) * IDX_DIM), iq, jnp.zeros_like(iq))
         for h in range(IDX_HEADS)], axis=0)
    wscale = (IDX_HEADS ** -0.5) * (IDX_DIM ** -0.5)
    w = [wt_ref[h:h + 1, :] * wscale for h in range(IDX_HEADS)]
    qpos = t0 + lax.broadcasted_iota(jnp.int32, (SEL_SUB, SEL_TQ), 1)
    kiota = lax.broadcasted_iota(jnp.int32, (SEL_SUB, SEL_TQ), 0)

    def score_chunk(c, carry, diagonal):
        rmax, rmin, ge0, gt0, minpos = carry
        base = pl.multiple_of(c * SEL_SC, SEL_SC)
        raw = _dot_nt(ik_ref[pl.ds(base, SEL_SC), :], qs)
        for u in range(SEL_SC // SEL_SUB):
            rows = slice(u * SEL_SUB, (u + 1) * SEL_SUB)
            score = w[0] * jnp.maximum(raw[rows, 0:SEL_TQ], 0.0)
            for h in range(1, IDX_HEADS):
                score = score + w[h] * jnp.maximum(raw[rows, h * SEL_TQ:(h + 1) * SEL_TQ], 0.0)
            rmin = jnp.minimum(rmin, score)
            if diagonal:
                score = jnp.where((kiota + (base + u * SEL_SUB)) <= qpos, score, -jnp.inf)
            sc_ref[pl.ds(base + u * SEL_SUB, SEL_SUB), :] = score
            rmax = jnp.maximum(rmax, score)
            pos = score > 0.0
            ge0 = ge0 + jnp.where(score >= 0.0, 1.0, 0.0)
            gt0 = gt0 + jnp.where(pos, 1.0, 0.0)
            minpos = jnp.minimum(minpos, jnp.where(pos, score, jnp.inf))
        return rmax, rmin, ge0, gt0, minpos

    def slab(v):
        return jnp.full((SEL_SUB, SEL_TQ), v, F32)

    carry = lax.fori_loop(0, n_full, lambda c, cr: score_chunk(c, cr, False),
                          (slab(-jnp.inf), slab(jnp.inf), slab(0.0), slab(0.0), slab(jnp.inf)))
    rmax, rmin, ge0, gt0, minpos = score_chunk(n_full, carry, True)
    rmax = jnp.max(rmax, axis=0, keepdims=True)
    rmin = jnp.min(rmin, axis=0, keepdims=True)
    ge0 = jnp.sum(ge0, axis=0, keepdims=True)
    gt0 = jnp.sum(gt0, axis=0, keepdims=True)
    minpos = jnp.min(minpos, axis=0, keepdims=True)

    navail = (t0 + 1 + lax.broadcasted_iota(jnp.int32, (1, SEL_TQ), 1)).astype(F32)

    def sweep(mid, snap):
        midb = jnp.broadcast_to(mid, (SEL_SUB, SEL_TQ))

        def body(c, carry):
            base = pl.multiple_of(c * SEL_SC, SEL_SC)
            cnt, up, dn = carry
            for u in range(SEL_SC // SEL_SUB):
                x = sc_ref[pl.ds(base + u * SEL_SUB, SEL_SUB), :]
                ge = x >= midb
                cnt = cnt + jnp.where(ge, 1.0, 0.0)
                if snap:
                    up = jnp.minimum(up, jnp.where(ge, x, jnp.inf))
                    dn = jnp.maximum(dn, jnp.where(ge, -jnp.inf, x))
            return cnt, up, dn

        init = (jnp.zeros((SEL_SUB, SEL_TQ), F32), jnp.full((SEL_SUB, SEL_TQ), jnp.inf, F32),
                jnp.full((SEL_SUB, SEL_TQ), -jnp.inf, F32))
        cnt, up, dn = lax.fori_loop(0, n_sc, body, init)
        return (jnp.sum(cnt, axis=0, keepdims=True), jnp.min(up, axis=0, keepdims=True),
                jnp.max(dn, axis=0, keepdims=True))

    def advance(st, mid, snap):
        it, lo, hi, hi_dn, c_lo, c_hi, done = st
        cnt, up, dn = sweep(mid, snap)
        live = jnp.where(done > 0.5, 0.0, jnp.where(mid <= lo, 0.0, jnp.where(mid >= hi, 0.0, 1.0)))
        ge = jnp.where(cnt >= kf, live, 0.0) > 0.5
        lt = jnp.where(cnt >= kf, 0.0, live) > 0.5
        lo = jnp.where(ge, up if snap else mid, lo)
        c_lo = jnp.where(ge, cnt, c_lo)
        hi = jnp.where(lt, mid, hi)
        hi_dn = jnp.where(lt, dn if snap else jnp.inf, hi_dn)
        c_hi = jnp.where(lt, cnt, c_hi)
        done = jnp.where(live < 0.5, 1.0, jnp.where(cnt == kf, 1.0, jnp.where(hi_dn <= lo, 1.0, 0.0)))
        return it + 1, lo, hi, hi_dn, c_lo, c_hi, done

    def make_step(snap):
        def step(st):
            return advance(st, 0.5 * st[1] + 0.5 * st[2], snap)
        return step

    def make_cond(limit):
        def cond(st):
            return jnp.logical_and(st[0] < limit, jnp.min(st[6]) < 0.5)
        return cond

    inf_row = jnp.full((1, SEL_TQ), jnp.inf, F32)
    all_in = navail <= kf
    above = jnp.logical_and(jnp.logical_not(all_in), gt0 >= kf)
    at_zero = jnp.logical_and(jnp.logical_not(all_in), jnp.logical_and(gt0 < kf, ge0 >= kf))
    below = jnp.logical_and(jnp.logical_not(all_in), ge0 < kf)
    lo = jnp.where(above, minpos, jnp.where(at_zero, 0.0, rmin))
    c_lo = jnp.where(above, gt0, jnp.where(at_zero, ge0, navail))
    hi = jnp.where(at_zero, minpos, jnp.where(below, 0.0, inf_row))
    c_hi = jnp.where(at_zero, gt0, jnp.where(below, ge0, 0.0))
    hi_dn = jnp.where(above, rmax, inf_row)
    done = jnp.where(all_in, 1.0, jnp.where(at_zero, 1.0, 0.0))
    st = (jnp.int32(0), lo, hi, hi_dn, c_lo, c_hi, done)
    st = advance(st, jnp.where(above, rmax, 0.5 * lo + 0.5 * hi), False)
    st = lax.while_loop(make_cond(SEL_FAST_ITERS), make_step(False), st)
    st = lax.while_loop(make_cond(SEL_MAX_ITERS), make_step(True), st)
    _, lo, hi, _, c_lo, c_hi, _ = st
    need = kf - c_hi
    ties = jnp.max(jnp.where(navail > kf, c_lo, kf)) > kf

    o_ref[...] = jnp.full(o_ref.shape, NEG, o_ref.dtype)

    @pl.when(jnp.logical_not(ties))
    def _():
        def mask_body(c, carry):
            base = pl.multiple_of(c * SEL_MC, SEL_MC)
            x = sc_ref[pl.ds(base, SEL_MC), :]
            o_ref[0, pl.ds(base, SEL_MC), :] = jnp.where(x >= lo, 0.0, NEG).astype(o_ref.dtype)
            return carry
        lax.fori_loop(0, n_mc, mask_body, 0)

    @pl.when(ties)
    def _():
        lmat = lmat_ref[...]

        def mask_body(c, run):
            base = pl.multiple_of(c * SEL_MC, SEL_MC)
            x = sc_ref[pl.ds(base, SEL_MC), :]
            bnd = jnp.where(x >= lo, jnp.where(x < hi, 1.0, 0.0), 0.0)
            incl = _dot(lmat, bnd.astype(BF16)) + run
            keep = jnp.where(x >= hi, 1.0, jnp.where(incl - bnd < need, bnd, 0.0))
            o_ref[0, pl.ds(base, SEL_MC), :] = jnp.where(keep > 0.5, 0.0, NEG).astype(o_ref.dtype)
            return incl[SEL_MC - 1:SEL_MC, :]
        lax.fori_loop(0, n_mc, mask_body, jnp.zeros((1, SEL_TQ), F32))


def _sel_b(z, st, lmat, batch, seq):
    nb = seq // SEL_TQ
    topk = min(TOPK_MAX, seq // 4)
    return pl.pallas_call(
        functools.partial(_sel_b_kernel, topk=topk),
        grid=(batch, nb),
        in_specs=[pl.BlockSpec((SEL_TQ, LANES), lambda b, i: (b * nb + i, OFF_BIQ // LANES)),
                  pl.BlockSpec((seq, LANES), lambda b, i: (b, OFF_BIK // LANES)),
                  pl.BlockSpec((8, SEL_TQ), lambda b, i: (0, b * nb + i)),
                  pl.BlockSpec((SEL_MC, SEL_MC), lambda b, i: (0, 0))],
        out_specs=pl.BlockSpec((1, seq, SEL_TQ), lambda b, i: (b, 0, i)),
        out_shape=jax.ShapeDtypeStruct((batch, seq, seq), BF16),
        scratch_shapes=[pltpu.VMEM((seq, SEL_TQ), F32)],
        compiler_params=_params(("arbitrary", "arbitrary"), 48),
        name="sel_b",
    )(z, z, st, lmat)


ATT_T = 256
B_NEAR = 3


def _mix_b_kernel(q_ref, k_ref, vt_ref, mask_ref, gate_ref, tbl_ref, o_ref, m_ref, l_ref, acc_ref):
    i = pl.program_id(1)
    _init_state(m_ref, l_ref, acc_ref)
    qs = [_head_q(q_ref, h) for h in range(N_HEADS)]

    def tile(j, near):
        base = pl.multiple_of(j * ATT_T, ATT_T)
        madd = mask_ref[0, pl.ds(base, ATT_T), :].astype(F32)

        def logits(h):
            kv = h // KV_GROUP
            s = _dot_nt(k_ref[pl.ds(base, ATT_T), kv * LANES:(kv + 1) * LANES], qs[h]) + madd
            return s if near is None else s + tbl_ref[h, near]

        def values(h):
            kv = h // KV_GROUP
            return vt_ref[kv * HEAD_DIM:(kv + 1) * HEAD_DIM, pl.ds(base, ATT_T)]

        _heads_pipelined(logits, values, m_ref, l_ref, acc_ref)

    def far_body(j, carry):
        tile(j, None)
        return carry

    lax.fori_loop(0, jnp.maximum(i - (B_NEAR - 1), 0), far_body, 0)
    for d in range(B_NEAR - 1, -1, -1):
        @pl.when(i >= d)
        def _(d=d):
            tile(i - d, d)
    for p in range(N_PAIRS):
        sl = slice(p * LANES, (p + 1) * LANES)
        o_ref[:, sl] = _gated_pair(p, gate_ref[:, sl], l_ref, acc_ref)


def _mix_b(qb, kb, z, vt, maskadd, tbl, batch, seq):
    n = qb.shape[0]
    t = ATT_T
    nq = seq // t
    return pl.pallas_call(
        _mix_b_kernel,
        grid=(batch, nq),
        in_specs=[pl.BlockSpec((t, 512), lambda b, i: (b * nq + i, 0)),
                  pl.BlockSpec((seq, 256), lambda b, i: (b, 0)),
                  pl.BlockSpec((LANES, seq), lambda b, i: (VT_B // LANES, b)),
                  pl.BlockSpec((1, seq, t), lambda b, i: (b, 0, i)),
                  pl.BlockSpec((t, 512), lambda b, i: (b * nq + i, OFF_BG // 512)),
                  pl.BlockSpec((N_HEADS, B_NEAR, t, t), lambda b, i: (0, 0, 0, 0))],
        out_specs=pl.BlockSpec((t, 512), lambda b, i: (b * nq + i, 0)),
        out_shape=jax.ShapeDtypeStruct((n, 512), BF16),
        scratch_shapes=[pltpu.VMEM((N_HEADS, 8, t), F32), pltpu.VMEM((N_HEADS, 8, t), F32),
                        pltpu.VMEM((N_HEADS, HEAD_DIM, t), F32)],
        compiler_params=_params(("arbitrary", "arbitrary"), 56),
        name="mix_b",
    )(qb, kb, vt, maskadd, z, tbl)


def _mix_c_kernel(q_ref, k_ref, vt_ref, ck_ref, cq_ref, gate_ref, o_ref, m_ref, l_ref, acc_ref):
    i = pl.program_id(1)
    _init_state(m_ref, l_ref, acc_ref)
    qs = [_head_q(q_ref, h) for h in range(N_HEADS)]
    q0 = pl.multiple_of(i * ATT_T, ATT_T)
    cq = cq_ref[0, :, pl.ds(q0, ATT_T)]
    krow = lax.broadcasted_iota(jnp.int32, (ATT_T, ATT_T), 0)
    qcol = lax.broadcasted_iota(jnp.int32, (ATT_T, ATT_T), 1)

    def tile(j, diag):
        base = pl.multiple_of(j * ATT_T, ATT_T)
        ck_all = ck_ref[pl.ds(base, ATT_T), :]

        def logits(h):
            p = h // 2
            ck = ck_all[:, MISC_CF + h:MISC_CF + h + 1]
            s = _dot_nt(k_ref[pl.ds(base, ATT_T), p * LANES:(p + 1) * LANES], qs[h]) + cq[h:h + 1, :] - ck
            return jnp.where(krow <= qcol, s, NEG) if diag else s

        def values(h):
            return vt_ref[h * HEAD_DIM:(h + 1) * HEAD_DIM, pl.ds(base, ATT_T)]

        _heads_pipelined(logits, values, m_ref, l_ref, acc_ref)

    def body(j, carry):
        tile(j, False)
        return carry

    lax.fori_loop(0, i, body, 0)
    tile(i, True)
    for p in range(N_PAIRS):
        sl = slice(p * LANES, (p + 1) * LANES)
        o_ref[:, sl] = _gated_pair(p, gate_ref[:, sl], l_ref, acc_ref)


def _mix_c(qc, kc, z, vt, cum, cum_t, batch, seq):
    n = qc.shape[0]
    t = ATT_T
    nq = seq // t
    return pl.pallas_call(
        _mix_c_kernel,
        grid=(batch, nq),
        in_specs=[pl.BlockSpec((t, 512), lambda b, i: (b * nq + i, 0)),
                  pl.BlockSpec((seq, 512), lambda b, i: (b, 0)),
                  pl.BlockSpec((512, seq), lambda b, i: (VT_C // 512, b)),
                  pl.BlockSpec((seq, LANES), lambda b, i: (b, 0)),
                  pl.BlockSpec((1, N_HEADS, seq), lambda b, i: (b, 0, 0)),
                  pl.BlockSpec((t, 512), lambda b, i: (b * nq + i, OFF_CG // 512))],
        out_specs=pl.BlockSpec((t, 512), lambda b, i: (b * nq + i, 0)),
        out_shape=jax.ShapeDtypeStruct((n, 512), BF16),
        scratch_shapes=[pltpu.VMEM((N_HEADS, 8, t), F32), pltpu.VMEM((N_HEADS, 8, t), F32),
                        pltpu.VMEM((N_HEADS, HEAD_DIM, t), F32)],
        compiler_params=_params(("arbitrary", "arbitrary"), 56),
        name="mix_c",
    )(qc, kc, vt, cum, cum_t, z)


def _merge_kernel(oa, ob, oc, ga, gb, gc, x_ref, wb_ref, wo_ref, o_ref):
    merged = None
    for br, (o, g) in enumerate(((oa, ga), (ob, gb), (oc, gc))):
        y = _dot(o[...], wb_ref[br])
        term = jax.nn.sigmoid(g[...].astype(F32)) * y
        merged = term if merged is None else merged + term
    o_ref[...] = x_ref[...] + _dot(merged.astype(BF16), wo_ref[...])


def _merge(oa, ob, oc, z, x2d, wb, wo):
    n, d = x2d.shape
    tm = 512

    def ospec():
        return pl.BlockSpec((tm, 512), lambda i: (i, 0))

    def gspec(k):
        return pl.BlockSpec((tm, d), lambda i, _k=k: (i, OFF_MERGE // d + _k))

    return pl.pallas_call(
        _merge_kernel,
        grid=(n // tm,),
        in_specs=[ospec(), ospec(), ospec(), gspec(0), gspec(1), gspec(2),
                  pl.BlockSpec((tm, d), lambda i: (i, 0)),
                  pl.BlockSpec((3, 512, d), lambda i: (0, 0, 0)),
                  pl.BlockSpec((d, d), lambda i: (0, 0))],
        out_specs=pl.BlockSpec((tm, d), lambda i: (i, 0)),
        out_shape=jax.ShapeDtypeStruct((n, d), F32),
        compiler_params=_params(("arbitrary",), 48),
        name="merge",
    )(oa, ob, oc, z, z, z, x2d, wb, wo)


def _bucket_lookup(table, bucket):
    out = jnp.zeros(bucket.shape + (table.shape[1],), F32)
    for b in range(N_BUCKETS):
        out = jnp.where((bucket == b)[..., None], table[b].astype(F32), out)
    return out


def _window_table(bias_a):
    ki = jnp.arange(2 * WINDOW)[:, None]
    qi = jnp.arange(WINDOW)[None, :]
    delta = qi + WINDOW - ki
    band = (delta >= 0) & (delta < WINDOW)
    bias = _bucket_lookup(bias_a, _t5_bucket(delta)).transpose(2, 0, 1) * LOG2E
    return jnp.where(band[None], bias, NEG)


def _near_table(bias_b):
    ki = jnp.arange(ATT_T)[:, None]
    qi = jnp.arange(ATT_T)[None, :]
    d = jnp.arange(B_NEAR)[:, None, None]
    delta = d * ATT_T + qi[None] - ki[None]
    bias = _bucket_lookup(bias_b, _t5_bucket(delta))
    far = bias_b[N_BUCKETS - 1].astype(F32)
    return ((bias - far) * LOG2E).transpose(3, 0, 1, 2)


def kernel(x, norm_gain, w_in, b_forget, qk_gain, sinks, w_branch, w_out, rel_bias):
    batch, seq, d = x.shape
    depth = norm_gain.shape[0]
    assert w_in.shape[-1] == IN_COLS and seq % 1024 == 0 and d == 1024
    assert (B_NEAR - 1) * ATT_T - (ATT_T - 1) < T5_FAR <= B_NEAR * ATT_T - (ATT_T - 1)

    w_perm = _gather_cols(w_in, _COLS).astype(BF16)
    w_vt = jnp.swapaxes(_gather_cols(w_in, _VT_COLS), 1, 2).astype(BF16)
    w_st = jnp.swapaxes(_gather_cols(w_in, _ST_COLS), 1, 2).astype(BF16)
    wb = w_branch.astype(BF16)
    wo = w_out.astype(BF16)

    gmat = jnp.asarray(np.kron(np.eye(N_HEADS), np.ones((HEAD_DIM, HEAD_DIM))), BF16)
    prep_t = 512
    lmat = jnp.asarray(np.tril(np.ones((prep_t, prep_t))), BF16)
    lmat_sel = lmat[:SEL_MC, :SEL_MC]
    tbl_a = _window_table(rel_bias[:, :N_HEADS])
    tbl_b = _near_table(rel_bias[:, N_HEADS:])

    x2d = x.reshape(batch * seq, d)
    for layer in range(depth):
        g = qk_gain[layer]
        qscale = ATTN_SCALE * LOG2E
        gains = jnp.zeros((8, 512), F32)
        gains = gains.at[0].set(jnp.tile(g[0, 0], N_HEADS) * qscale)
        gains = gains.at[1, :256].set(jnp.tile(g[0, 1], 4))
        gains = gains.at[2].set(jnp.tile(g[1, 0], N_HEADS) * qscale)
        gains = gains.at[3, :256].set(jnp.tile(g[1, 1], 4))
        gains = gains.at[4].set(jnp.tile(g[2, 0], N_HEADS) * qscale)
        gains = gains.at[5].set(jnp.tile(g[2, 1], N_HEADS))
        gains = gains.at[6, MISC_CF:MISC_CF + N_HEADS].set(b_forget[layer])

        z, misc, vt, st = _proj(x2d, norm_gain[layer][None, :], w_perm[layer], w_vt[layer], w_st[layer])
        qa, ka, qb, kb, qc, kc, cum = _prep(z, misc, gains, gmat, lmat, batch, seq)
        cum_t = cum.reshape(batch, seq, LANES)[:, :, MISC_CF:MISC_CF + N_HEADS].transpose(0, 2, 1)
        oa = _mix_a(qa, ka, z, vt, tbl_a, sinks[layer], batch, seq)
        maskadd = _sel_b(z, st, lmat_sel, batch, seq)
        ob = _mix_b(qb, kb, z, vt, maskadd, tbl_b, batch, seq)
        oc = _mix_c(qc, kc, z, vt, cum, cum_t, batch, seq)
        x2d = _merge(oa, ob, oc, z, x2d, wb[layer], wo[layer])
    return x2d.reshape(batch, seq, d)
```

```python
import functools
import math

import numpy as np
import jax
import jax.numpy as jnp
from jax import lax
from jax.experimental import pallas as pl
from jax.experimental.pallas import tpu as pltpu

F32 = jnp.float32
BF16 = jnp.bfloat16

HEAD_DIM = 64
LANES = 128
N_HEADS = 8
N_PAIRS = N_HEADS // 2
KV_GROUP = 4
WINDOW = 128
IDX_HEADS = 4
IDX_DIM = 32
TOPK_MAX = 256
N_BUCKETS = 32
MAX_DISTANCE = 512
EPS = 1e-6
ATTN_SCALE = HEAD_DIM ** -0.5
LOG2E = math.log2(math.e)
NEG = -1e30
T5_FAR = 413

_SIZES = dict(a_q=512, a_k=128, a_v=128, a_gate=512,
              b_q=512, b_k=128, b_v=128, b_iq=128, b_ik=32, b_iw=4, b_gate=512,
              c_q=512, c_k=512, c_v=512, c_f=8, c_gate=512, merge=3072)
_ORIG = {}
_o = 0
for _k, _v in _SIZES.items():
    _ORIG[_k] = _o
    _o += _v
IN_COLS = _o

OFF_MERGE = 0
OFF_AQ, OFF_AG, OFF_BQ, OFF_BG = 3072, 3584, 4096, 4608
OFF_CQ, OFF_CK, OFF_CG = 5120, 5632, 6144
OFF_AK, OFF_BK = 6656, 6912
OFF_BIQ, OFF_BIK, OFF_MISC = 7168, 7296, 7424
NP_COLS = 7680
PROJ_TN = 1536
MISC_CF = 8
VT_C, VT_A, VT_B, VT_ROWS = 0, 512, 640, 768


def _orig(name, idx=None):
    src = np.arange(_SIZES[name]) + _ORIG[name]
    return src if idx is None else src[idx]


def _column_map():
    cols = np.full((NP_COLS,), -1, np.int64)

    def put(off, src):
        cols[off:off + len(src)] = src

    for off, name in ((OFF_MERGE, "merge"), (OFF_AQ, "a_q"), (OFF_AG, "a_gate"), (OFF_BQ, "b_q"),
                      (OFF_BG, "b_gate"), (OFF_CQ, "c_q"), (OFF_CK, "c_k"), (OFF_CG, "c_gate"),
                      (OFF_BIQ, "b_iq")):
        put(off, _orig(name))
    dup = np.concatenate([np.arange(64), np.arange(64), np.arange(64, 128), np.arange(64, 128)])
    put(OFF_AK, _orig("a_k", dup))
    put(OFF_BK, _orig("b_k", dup))
    put(OFF_BIK, _orig("b_ik", np.tile(np.arange(IDX_DIM), IDX_HEADS)))
    put(OFF_MISC + MISC_CF, _orig("c_f"))
    return cols


def _gather_cols(w, cols):
    parts = []
    start = 0
    while start < len(cols):
        end = start + 1
        if cols[start] < 0:
            while end < len(cols) and cols[end] < 0:
                end += 1
            parts.append(jnp.zeros(w.shape[:-1] + (end - start,), w.dtype))
        else:
            while end < len(cols) and cols[end] == cols[end - 1] + 1:
                end += 1
            parts.append(w[..., int(cols[start]):int(cols[end - 1]) + 1])
        start = end
    return jnp.concatenate(parts, axis=-1)


_COLS = _column_map()
_VT_COLS = np.concatenate([_orig("c_v"), _orig("a_v"), _orig("b_v")])
_ST_COLS = np.concatenate([_orig("b_iw"), np.full((4,), -1, np.int64)])


def _params(sem, vmem_mb):
    return pltpu.CompilerParams(dimension_semantics=sem, vmem_limit_bytes=vmem_mb * 1024 * 1024)


def _dot(a, b):
    return jnp.dot(a, b, preferred_element_type=F32)


def _dot_nt(a, b):
    return lax.dot_general(a, b, (((1,), (1,)), ((), ())), preferred_element_type=F32)


def _t5_bucket(delta):
    n = jnp.maximum(delta, 0)
    max_exact = N_BUCKETS // 2
    nf = jnp.maximum(n, 1).astype(F32)
    large = max_exact + (jnp.log(nf / max_exact) / math.log(MAX_DISTANCE / max_exact)
                         * (N_BUCKETS - max_exact)).astype(jnp.int32)
    large = jnp.minimum(large, N_BUCKETS - 1)
    return jnp.where(n < max_exact, n, large)


def _proj_kernel(x_ref, g_ref, w_ref, wvt_ref, wst_ref, z_ref, misc_ref, vt_ref, st_ref, h_ref,
                 *, misc_tile, misc_local):
    j = pl.program_id(1)

    @pl.when(j == 0)
    def _():
        x = x_ref[...]
        ms = jnp.mean(x * x, axis=-1, keepdims=True)
        h = (x * lax.rsqrt(ms + EPS) * g_ref[...]).astype(BF16)
        h_ref[...] = h
        vt_ref[...] = _dot_nt(wvt_ref[...], h).astype(BF16)
        st_ref[...] = _dot_nt(wst_ref[...], h)

    acc = _dot(h_ref[...], w_ref[...])
    z_ref[...] = acc.astype(BF16)

    @pl.when(j == misc_tile)
    def _():
        misc_ref[...] = acc[:, misc_local:misc_local + LANES]


def _proj(x2d, gain, w, wvt, wst):
    n, d = x2d.shape
    tm = 1024
    return pl.pallas_call(
        functools.partial(_proj_kernel, misc_tile=OFF_MISC // PROJ_TN, misc_local=OFF_MISC % PROJ_TN),
        grid=(n // tm, NP_COLS // PROJ_TN),
        in_specs=[pl.BlockSpec((tm, d), lambda i, j: (i, 0)),
                  pl.BlockSpec((1, d), lambda i, j: (0, 0)),
                  pl.BlockSpec((d, PROJ_TN), lambda i, j: (0, j)),
                  pl.BlockSpec((VT_ROWS, d), lambda i, j: (0, 0)),
                  pl.BlockSpec((8, d), lambda i, j: (0, 0))],
        out_specs=[pl.BlockSpec((tm, PROJ_TN), lambda i, j: (i, j)),
                   pl.BlockSpec((tm, LANES), lambda i, j: (i, 0)),
                   pl.BlockSpec((VT_ROWS, tm), lambda i, j: (0, i)),
                   pl.BlockSpec((8, tm), lambda i, j: (0, i))],
        out_shape=[jax.ShapeDtypeStruct((n, NP_COLS), BF16),
                   jax.ShapeDtypeStruct((n, LANES), F32),
                   jax.ShapeDtypeStruct((VT_ROWS, n), BF16),
                   jax.ShapeDtypeStruct((8, n), F32)],
        scratch_shapes=[pltpu.VMEM((tm, d), BF16)],
        compiler_params=_params(("arbitrary", "arbitrary"), 56),
        name="proj",
    )(x2d, gain, w, wvt, wst)


def _group_rms(x_bf16, g_ref, gain_row):
    x = x_bf16.astype(F32)
    width = x.shape[-1]
    sq = x * x
    hi = sq.astype(BF16)
    lo = (sq - hi.astype(F32)).astype(BF16)
    g = g_ref[:width, :width]
    ss = _dot(hi, g) + _dot(lo, g)
    return (x * lax.rsqrt(ss * (1.0 / HEAD_DIM) + EPS) * gain_row).astype(BF16)


def _prep_kernel(aq, ak, bq, bk, cq, ck, misc, gains, gmat, lmat,
                 qa_o, ka_o, qb_o, kb_o, qc_o, kc_o, cum_o, carry):
    t = pl.program_id(1)
    qa_o[...] = _group_rms(aq[...], gmat, gains[0:1, :])
    ka_o[...] = _group_rms(ak[...], gmat, gains[1:2, :256])
    qb_o[...] = _group_rms(bq[...], gmat, gains[2:3, :])
    kb_o[...] = _group_rms(bk[...], gmat, gains[3:4, :256])
    qc_o[...] = _group_rms(cq[...], gmat, gains[4:5, :])
    kc_o[...] = _group_rms(ck[...], gmat, gains[5:6, :])

    @pl.when(t == 0)
    def _():
        carry[...] = jnp.zeros_like(carry)

    xm = misc[...] + gains[6:7, :LANES]
    logf = jnp.minimum(xm, 0.0) - jnp.log(1.0 + jnp.exp(-jnp.abs(xm)))
    hi = logf.astype(BF16)
    r1 = logf - hi.astype(F32)
    mid = r1.astype(BF16)
    lo = (r1 - mid.astype(F32)).astype(BF16)
    lm = lmat[...]
    c = _dot(lm, hi) + _dot(lm, mid) + _dot(lm, lo) + carry[...]
    cum_o[...] = c * LOG2E
    rows = c.shape[0]
    carry[...] = c[rows - 1:rows, :]


def _prep(z, misc, gains, gmat, lmat, batch, seq):
    n = z.shape[0]
    tp = lmat.shape[0]
    nt = seq // tp

    def zspec(width, off):
        return pl.BlockSpec((tp, width), lambda b, t, _c=off // width: (b * nt + t, _c))

    def ospec(width):
        return pl.BlockSpec((tp, width), lambda b, t: (b * nt + t, 0))

    return pl.pallas_call(
        _prep_kernel,
        grid=(batch, nt),
        in_specs=[zspec(512, OFF_AQ), zspec(256, OFF_AK), zspec(512, OFF_BQ), zspec(256, OFF_BK),
                  zspec(512, OFF_CQ), zspec(512, OFF_CK), ospec(LANES),
                  pl.BlockSpec((8, 512), lambda b, t: (0, 0)),
                  pl.BlockSpec((512, 512), lambda b, t: (0, 0)),
                  pl.BlockSpec((tp, tp), lambda b, t: (0, 0))],
        out_specs=[ospec(512), ospec(256), ospec(512), ospec(256), ospec(512), ospec(512), ospec(LANES)],
        out_shape=[jax.ShapeDtypeStruct((n, 512), BF16), jax.ShapeDtypeStruct((n, 256), BF16),
                   jax.ShapeDtypeStruct((n, 512), BF16), jax.ShapeDtypeStruct((n, 256), BF16),
                   jax.ShapeDtypeStruct((n, 512), BF16), jax.ShapeDtypeStruct((n, 512), BF16),
                   jax.ShapeDtypeStruct((n, LANES), F32)],
        scratch_shapes=[pltpu.VMEM((1, LANES), F32)],
        compiler_params=_params(("arbitrary", "arbitrary"), 48),
        name="prep",
    )(z, z, z, z, z, z, misc, gains, gmat, lmat)


def _half_mask(shape, half):
    lane = lax.broadcasted_iota(jnp.int32, shape, len(shape) - 1)
    return (lane < HEAD_DIM) if half == 0 else (lane >= HEAD_DIM)


def _head_q(q_ref, h):
    p = h // 2
    qp = q_ref[:, p * LANES:(p + 1) * LANES]
    return jnp.where(_half_mask(qp.shape, h % 2), qp, jnp.zeros_like(qp))


DEN_ROWS = 16
ACC_ROWS = HEAD_DIM + DEN_ROWS


def _with_ones(vt):
    return jnp.concatenate([vt, jnp.ones((DEN_ROWS, vt.shape[1]), vt.dtype)], axis=0)


def _softmax_step(h, s, vt, m_ref, acc_ref):
    m_prev = m_ref[h]
    m_new = jnp.maximum(m_prev, jnp.max(s, axis=0, keepdims=True))
    alpha = jnp.exp2(m_prev - m_new)
    p = jnp.exp2(s - m_new[0:1, :])
    acc_ref[h] = alpha[0:1, :] * acc_ref[h] + _dot(_with_ones(vt), p.astype(BF16))
    m_ref[h] = m_new


HEADS_AHEAD = 4


def _heads_pipelined(logits, values, m_ref, acc_ref):
    ahead = [logits(h) for h in range(HEADS_AHEAD)]
    for h in range(N_HEADS):
        if h + HEADS_AHEAD < N_HEADS:
            ahead.append(logits(h + HEADS_AHEAD))
        _softmax_step(h, ahead.pop(0), values(h), m_ref, acc_ref)


def _init_state(m_ref, acc_ref):
    m_ref[...] = jnp.full(m_ref.shape, NEG, F32)
    acc_ref[...] = jnp.zeros(acc_ref.shape, F32)


def _gated_pair(p, gate, acc_ref):
    halves = []
    for h in (2 * p, 2 * p + 1):
        a = acc_ref[h]
        halves.append(a[:HEAD_DIM] / a[HEAD_DIM:HEAD_DIM + 1])
    ot = jnp.concatenate(halves, axis=0)
    g = gate.astype(F32)
    return (ot.T * (g * jax.nn.sigmoid(g))).astype(BF16)


def _mix_a_kernel(sink_ref, q_ref, kp_ref, kc_ref, vp_ref, vc_ref, gate_ref, tbl_ref, o_ref, acc_ref):
    n = pl.program_id(1)
    pad_pen = jnp.where(n > 0, 0.0, NEG).astype(F32)
    den_rows = lax.broadcasted_iota(jnp.int32, (ACC_ROWS, WINDOW), 0) >= HEAD_DIM
    for h in range(N_HEADS):
        kv = h // KV_GROUP
        q = _head_q(q_ref, h)
        ksl = slice(kv * LANES, (kv + 1) * LANES)
        vsl = slice(kv * HEAD_DIM, (kv + 1) * HEAD_DIM)
        s_prev = _dot_nt(kp_ref[:, ksl], q) + tbl_ref[h, :WINDOW, :] + pad_pen
        s_cur = _dot_nt(kc_ref[:, ksl], q) + tbl_ref[h, WINDOW:, :]
        sink = sink_ref[h] * LOG2E
        m = jnp.maximum(jnp.maximum(jnp.max(s_prev, axis=0, keepdims=True),
                                    jnp.max(s_cur, axis=0, keepdims=True)), sink)
        p_prev = jnp.exp2(s_prev - m)
        p_cur = jnp.exp2(s_cur - m)
        acc = (_dot(_with_ones(vp_ref[vsl, :]), p_prev.astype(BF16))
               + _dot(_with_ones(vc_ref[vsl, :]), p_cur.astype(BF16)))
        acc_ref[h] = acc + jnp.where(den_rows, jnp.exp2(sink - m), 0.0)
    for p in range(N_PAIRS):
        sl = slice(p * LANES, (p + 1) * LANES)
        o_ref[:, sl] = _gated_pair(p, gate_ref[:, sl], acc_ref)


def _mix_a(qa, ka, z, vt, tbl, sinks, batch, seq):
    n = qa.shape[0]
    t = WINDOW
    nb = seq // t
    return pl.pallas_call(
        _mix_a_kernel,
        grid=(batch, nb),
        in_specs=[pl.BlockSpec(memory_space=pltpu.SMEM),
                  pl.BlockSpec((t, 512), lambda b, i: (b * nb + i, 0)),
                  pl.BlockSpec((t, 256), lambda b, i: (b * nb + jnp.maximum(i - 1, 0), 0)),
                  pl.BlockSpec((t, 256), lambda b, i: (b * nb + i, 0)),
                  pl.BlockSpec((LANES, t), lambda b, i: (VT_A // LANES, b * nb + jnp.maximum(i - 1, 0))),
                  pl.BlockSpec((LANES, t), lambda b, i: (VT_A // LANES, b * nb + i)),
                  pl.BlockSpec((t, 512), lambda b, i: (b * nb + i, OFF_AG // 512)),
                  pl.BlockSpec((N_HEADS, 2 * t, t), lambda b, i: (0, 0, 0))],
        out_specs=pl.BlockSpec((t, 512), lambda b, i: (b * nb + i, 0)),
        out_shape=jax.ShapeDtypeStruct((n, 512), BF16),
        scratch_shapes=[pltpu.VMEM((N_HEADS, ACC_ROWS, t), F32)],
        compiler_params=_params(("arbitrary", "arbitrary"), 32),
        name="mix_a",
    )(sinks, qa, ka, ka, vt, vt, z, tbl)


SEL_TQ = 256
SEL_SC = 512
SEL_SUB = 64
SEL_MC = 256
SEL_FAST_ITERS = 24
SEL_MAX_ITERS = 600


def _sel_b_kernel(iq_ref, ik_ref, wt_ref, lmat_ref, o_ref, sc_ref, *, topk):
    n = pl.program_id(1)
    t0 = n * SEL_TQ
    n_full = n // (SEL_SC // SEL_TQ)
    n_sc = n_full + 1
    n_mc = n + 1
    kf = float(topk)

    iq = iq_ref[...]
    lane = lax.broadcasted_iota(jnp.int32, iq.shape, 1)
    qs = jnp.concatenate(
        [jnp.where((lane >= h * IDX_DIM) & (lane < (h + 1) * IDX_DIM), iq, jnp.zeros_like(iq))
         for h in range(IDX_HEADS)], axis=0)
    wscale = (IDX_HEADS ** -0.5) * (IDX_DIM ** -0.5)
    w = [wt_ref[h:h + 1, :] * wscale for h in range(IDX_HEADS)]
    qpos = t0 + lax.broadcasted_iota(jnp.int32, (SEL_SUB, SEL_TQ), 1)
    kiota = lax.broadcasted_iota(jnp.int32, (SEL_SUB, SEL_TQ), 0)

    def score_chunk(c, carry, diagonal):
        rmax, rmin, ge0, gt0, minpos = carry
        base = pl.multiple_of(c * SEL_SC, SEL_SC)
        raw = _dot_nt(ik_ref[pl.ds(base, SEL_SC), :], qs)
        for u in range(SEL_SC // SEL_SUB):
            rows = slice(u * SEL_SUB, (u + 1) * SEL_SUB)
            score = w[0] * jnp.maximum(raw[rows, 0:SEL_TQ], 0.0)
            for h in range(1, IDX_HEADS):
                score = score + w[h] * jnp.maximum(raw[rows, h * SEL_TQ:(h + 1) * SEL_TQ], 0.0)
            rmin = jnp.minimum(rmin, score)
            if diagonal:
                score = jnp.where((kiota + (base + u * SEL_SUB)) <= qpos, score, -jnp.inf)
            sc_ref[pl.ds(base + u * SEL_SUB, SEL_SUB), :] = score
            rmax = jnp.maximum(rmax, score)
            pos = score > 0.0
            ge0 = ge0 + jnp.where(score >= 0.0, 1.0, 0.0)
            gt0 = gt0 + jnp.where(pos, 1.0, 0.0)
            minpos = jnp.minimum(minpos, jnp.where(pos, score, jnp.inf))
        return rmax, rmin, ge0, gt0, minpos

    def slab(v):
        return jnp.full((SEL_SUB, SEL_TQ), v, F32)

    carry = lax.fori_loop(0, n_full, lambda c, cr: score_chunk(c, cr, False),
                          (slab(-jnp.inf), slab(jnp.inf), slab(0.0), slab(0.0), slab(jnp.inf)))
    rmax, rmin, ge0, gt0, minpos = score_chunk(n_full, carry, True)
    rmax = jnp.max(rmax, axis=0, keepdims=True)
    rmin = jnp.min(rmin, axis=0, keepdims=True)
    ge0 = jnp.sum(ge0, axis=0, keepdims=True)
    gt0 = jnp.sum(gt0, axis=0, keepdims=True)
    minpos = jnp.min(minpos, axis=0, keepdims=True)

    navail = (t0 + 1 + lax.broadcasted_iota(jnp.int32, (1, SEL_TQ), 1)).astype(F32)

    def sweep(mid, snap):
        midb = jnp.broadcast_to(mid, (SEL_SUB, SEL_TQ))

        def body(c, carry):
            base = pl.multiple_of(c * SEL_SC, SEL_SC)
            cnt, up, dn = carry
            for u in range(SEL_SC // SEL_SUB):
                x = sc_ref[pl.ds(base + u * SEL_SUB, SEL_SUB), :]
                ge = x >= midb
                cnt = cnt + jnp.where(ge, 1.0, 0.0)
                if snap:
                    up = jnp.minimum(up, jnp.where(ge, x, jnp.inf))
                    dn = jnp.maximum(dn, jnp.where(ge, -jnp.inf, x))
            return cnt, up, dn

        init = (jnp.zeros((SEL_SUB, SEL_TQ), F32), jnp.full((SEL_SUB, SEL_TQ), jnp.inf, F32),
                jnp.full((SEL_SUB, SEL_TQ), -jnp.inf, F32))
        cnt, up, dn = lax.fori_loop(0, n_sc, body, init)
        return (jnp.sum(cnt, axis=0, keepdims=True), jnp.min(up, axis=0, keepdims=True),
                jnp.max(dn, axis=0, keepdims=True))

    def advance(st, mid, snap):
        it, lo, hi, hi_dn, c_lo, c_hi, done = st
        cnt, up, dn = sweep(mid, snap)
        live = jnp.where(done > 0.5, 0.0, jnp.where(mid <= lo, 0.0, jnp.where(mid >= hi, 0.0, 1.0)))
        ge = jnp.where(cnt >= kf, live, 0.0) > 0.5
        lt = jnp.where(cnt >= kf, 0.0, live) > 0.5
        lo = jnp.where(ge, up if snap else mid, lo)
        c_lo = jnp.where(ge, cnt, c_lo)
        hi = jnp.where(lt, mid, hi)
        hi_dn = jnp.where(lt, dn if snap else jnp.inf, hi_dn)
        c_hi = jnp.where(lt, cnt, c_hi)
        done = jnp.where(live < 0.5, 1.0, jnp.where(cnt == kf, 1.0, jnp.where(hi_dn <= lo, 1.0, 0.0)))
        return it + 1, lo, hi, hi_dn, c_lo, c_hi, done

    def make_step(snap):
        def step(st):
            return advance(st, 0.5 * st[1] + 0.5 * st[2], snap)
        return step

    def make_cond(limit):
        def cond(st):
            return jnp.logical_and(st[0] < limit, jnp.min(st[6]) < 0.5)
        return cond

    inf_row = jnp.full((1, SEL_TQ), jnp.inf, F32)
    all_in = navail <= kf
    above = jnp.logical_and(jnp.logical_not(all_in), gt0 >= kf)
    at_zero = jnp.logical_and(jnp.logical_not(all_in), jnp.logical_and(gt0 < kf, ge0 >= kf))
    below = jnp.logical_and(jnp.logical_not(all_in), ge0 < kf)
    lo = jnp.where(above, minpos, jnp.where(at_zero, 0.0, rmin))
    c_lo = jnp.where(above, gt0, jnp.where(at_zero, ge0, navail))
    hi = jnp.where(at_zero, minpos, jnp.where(below, 0.0, inf_row))
    c_hi = jnp.where(at_zero, gt0, jnp.where(below, ge0, 0.0))
    hi_dn = jnp.where(above, rmax, inf_row)
    done = jnp.where(all_in, 1.0, jnp.where(at_zero, 1.0, 0.0))
    st = (jnp.int32(0), lo, hi, hi_dn, c_lo, c_hi, done)
    st = advance(st, jnp.where(above, rmax, 0.5 * lo + 0.5 * hi), False)
    st = lax.while_loop(make_cond(SEL_FAST_ITERS), make_step(False), st)
    st = lax.while_loop(make_cond(SEL_MAX_ITERS), make_step(True), st)
    _, lo, hi, _, c_lo, c_hi, _ = st
    need = kf - c_hi
    ties = jnp.max(jnp.where(navail > kf, c_lo, kf)) > kf

    o_ref[...] = jnp.full(o_ref.shape, NEG, o_ref.dtype)

    @pl.when(jnp.logical_not(ties))
    def _():
        def mask_body(c, carry):
            base = pl.multiple_of(c * SEL_MC, SEL_MC)
            x = sc_ref[pl.ds(base, SEL_MC), :]
            o_ref[0, pl.ds(base, SEL_MC), :] = jnp.where(x >= lo, 0.0, NEG).astype(o_ref.dtype)
            return carry
        lax.fori_loop(0, n_mc, mask_body, 0)

    @pl.when(ties)
    def _():
        lmat = lmat_ref[...]

        def mask_body(c, run):
            base = pl.multiple_of(c * SEL_MC, SEL_MC)
            x = sc_ref[pl.ds(base, SEL_MC), :]
            bnd = jnp.where(x >= lo, jnp.where(x < hi, 1.0, 0.0), 0.0)
            incl = _dot(lmat, bnd.astype(BF16)) + run
            keep = jnp.where(x >= hi, 1.0, jnp.where(incl - bnd < need, bnd, 0.0))
            o_ref[0, pl.ds(base, SEL_MC), :] = jnp.where(keep > 0.5, 0.0, NEG).astype(o_ref.dtype)
            return incl[SEL_MC - 1:SEL_MC, :]
        lax.fori_loop(0, n_mc, mask_body, jnp.zeros((1, SEL_TQ), F32))


def _sel_b(z, st, lmat, batch, seq):
    nb = seq // SEL_TQ
    topk = min(TOPK_MAX, seq // 4)
    return pl.pallas_call(
        functools.partial(_sel_b_kernel, topk=topk),
        grid=(batch, nb),
        in_specs=[pl.BlockSpec((SEL_TQ, LANES), lambda b, i: (b * nb + i, OFF_BIQ // LANES)),
                  pl.BlockSpec((seq, LANES), lambda b, i: (b, OFF_BIK // LANES)),
                  pl.BlockSpec((8, SEL_TQ), lambda b, i: (0, b * nb + i)),
                  pl.BlockSpec((SEL_MC, SEL_MC), lambda b, i: (0, 0))],
        out_specs=pl.BlockSpec((1, seq, SEL_TQ), lambda b, i: (b, 0, i)),
        out_shape=jax.ShapeDtypeStruct((batch, seq, seq), BF16),
        scratch_shapes=[pltpu.VMEM((seq, SEL_TQ), F32)],
        compiler_params=_params(("arbitrary", "arbitrary"), 48),
        name="sel_b",
    )(z, z, st, lmat)


ATT_T = 256
B_NEAR = 3
FAR_TILES = 2


def _mix_b_kernel(q_ref, k_ref, vt_ref, mask_ref, gate_ref, tbl_ref, o_ref, m_ref, acc_ref):
    i = pl.program_id(1)
    _init_state(m_ref, acc_ref)
    qs = [_head_q(q_ref, h) for h in range(N_HEADS)]

    def tile(j, size, near):
        base = pl.multiple_of(j * ATT_T, ATT_T)
        madd = mask_ref[0, pl.ds(base, size), :].astype(F32)

        def logits(h):
            kv = h // KV_GROUP
            s = _dot_nt(k_ref[pl.ds(base, size), kv * LANES:(kv + 1) * LANES], qs[h]) + madd
            return s if near is None else s + tbl_ref[h, near]

        def values(h):
            kv = h // KV_GROUP
            return vt_ref[kv * HEAD_DIM:(kv + 1) * HEAD_DIM, pl.ds(base, size)]

        _heads_pipelined(logits, values, m_ref, acc_ref)

    def far_body(j, carry):
        tile(j * FAR_TILES, FAR_TILES * ATT_T, None)
        return carry

    n_far = jnp.maximum(i - (B_NEAR - 1), 0)
    lax.fori_loop(0, n_far // FAR_TILES, far_body, 0)

    @pl.when(n_far % FAR_TILES == 1)
    def _():
        tile(n_far - 1, ATT_T, None)

    for d in range(B_NEAR - 1, -1, -1):
        @pl.when(i >= d)
        def _(d=d):
            tile(i - d, ATT_T, d)
    for p in range(N_PAIRS):
        sl = slice(p * LANES, (p + 1) * LANES)
        o_ref[:, sl] = _gated_pair(p, gate_ref[:, sl], acc_ref)


def _mix_b(qb, kb, z, vt, maskadd, tbl, batch, seq):
    n = qb.shape[0]
    t = ATT_T
    nq = seq // t
    return pl.pallas_call(
        _mix_b_kernel,
        grid=(batch, nq),
        in_specs=[pl.BlockSpec((t, 512), lambda b, i: (b * nq + i, 0)),
                  pl.BlockSpec((seq, 256), lambda b, i: (b, 0)),
                  pl.BlockSpec((LANES, seq), lambda b, i: (VT_B // LANES, b)),
                  pl.BlockSpec((1, seq, t), lambda b, i: (b, 0, i)),
                  pl.BlockSpec((t, 512), lambda b, i: (b * nq + i, OFF_BG // 512)),
                  pl.BlockSpec((N_HEADS, B_NEAR, t, t), lambda b, i: (0, 0, 0, 0))],
        out_specs=pl.BlockSpec((t, 512), lambda b, i: (b * nq + i, 0)),
        out_shape=jax.ShapeDtypeStruct((n, 512), BF16),
        scratch_shapes=[pltpu.VMEM((N_HEADS, 8, t), F32), pltpu.VMEM((N_HEADS, ACC_ROWS, t), F32)],
        compiler_params=_params(("arbitrary", "arbitrary"), 56),
        name="mix_b",
    )(qb, kb, vt, maskadd, z, tbl)


def _mix_c_kernel(q_ref, k_ref, vt_ref, ck_ref, cq_ref, gate_ref, o_ref, m_ref, acc_ref):
    i = pl.program_id(1)
    _init_state(m_ref, acc_ref)
    qs = [_head_q(q_ref, h) for h in range(N_HEADS)]
    q0 = pl.multiple_of(i * ATT_T, ATT_T)
    cq = cq_ref[0, :, pl.ds(q0, ATT_T)]
    krow = lax.broadcasted_iota(jnp.int32, (ATT_T, ATT_T), 0)
    qcol = lax.broadcasted_iota(jnp.int32, (ATT_T, ATT_T), 1)

    def tile(j, size, diag):
        base = pl.multiple_of(j * ATT_T, ATT_T)
        ck_all = ck_ref[pl.ds(base, size), :]

        def logits(h):
            p = h // 2
            ck = ck_all[:, MISC_CF + h:MISC_CF + h + 1]
            s = _dot_nt(k_ref[pl.ds(base, size), p * LANES:(p + 1) * LANES], qs[h]) + cq[h:h + 1, :] - ck
            return jnp.where(krow <= qcol, s, NEG) if diag else s

        def values(h):
            return vt_ref[h * HEAD_DIM:(h + 1) * HEAD_DIM, pl.ds(base, size)]

        _heads_pipelined(logits, values, m_ref, acc_ref)

    def body(j, carry):
        tile(j * FAR_TILES, FAR_TILES * ATT_T, False)
        return carry

    lax.fori_loop(0, i // FAR_TILES, body, 0)

    @pl.when(i % FAR_TILES == 1)
    def _():
        tile(i - 1, ATT_T, False)

    tile(i, ATT_T, True)
    for p in range(N_PAIRS):
        sl = slice(p * LANES, (p + 1) * LANES)
        o_ref[:, sl] = _gated_pair(p, gate_ref[:, sl], acc_ref)


def _mix_c(qc, kc, z, vt, cum, cum_t, batch, seq):
    n = qc.shape[0]
    t = ATT_T
    nq = seq // t
    return pl.pallas_call(
        _mix_c_kernel,
        grid=(batch, nq),
        in_specs=[pl.BlockSpec((t, 512), lambda b, i: (b * nq + i, 0)),
                  pl.BlockSpec((seq, 512), lambda b, i: (b, 0)),
                  pl.BlockSpec((512, seq), lambda b, i: (VT_C // 512, b)),
                  pl.BlockSpec((seq, LANES), lambda b, i: (b, 0)),
                  pl.BlockSpec((1, N_HEADS, seq), lambda b, i: (b, 0, 0)),
                  pl.BlockSpec((t, 512), lambda b, i: (b * nq + i, OFF_CG // 512))],
        out_specs=pl.BlockSpec((t, 512), lambda b, i: (b * nq + i, 0)),
        out_shape=jax.ShapeDtypeStruct((n, 512), BF16),
        scratch_shapes=[pltpu.VMEM((N_HEADS, 8, t), F32), pltpu.VMEM((N_HEADS, ACC_ROWS, t), F32)],
        compiler_params=_params(("arbitrary", "arbitrary"), 56),
        name="mix_c",
    )(qc, kc, vt, cum, cum_t, z)


def _merge_kernel(oa, ob, oc, ga, gb, gc, x_ref, wb_ref, wo_ref, o_ref):
    merged = None
    for br, (o, g) in enumerate(((oa, ga), (ob, gb), (oc, gc))):
        y = _dot(o[...], wb_ref[br])
        term = jax.nn.sigmoid(g[...].astype(F32)) * y
        merged = term if merged is None else merged + term
    o_ref[...] = x_ref[...] + _dot(merged.astype(BF16), wo_ref[...])


def _merge(oa, ob, oc, z, x2d, wb, wo):
    n, d = x2d.shape
    tm = 512

    def ospec():
        return pl.BlockSpec((tm, 512), lambda i: (i, 0))

    def gspec(k):
        return pl.BlockSpec((tm, d), lambda i, _k=k: (i, OFF_MERGE // d + _k))

    return pl.pallas_call(
        _merge_kernel,
        grid=(n // tm,),
        in_specs=[ospec(), ospec(), ospec(), gspec(0), gspec(1), gspec(2),
                  pl.BlockSpec((tm, d), lambda i: (i, 0)),
                  pl.BlockSpec((3, 512, d), lambda i: (0, 0, 0)),
                  pl.BlockSpec((d, d), lambda i: (0, 0))],
        out_specs=pl.BlockSpec((tm, d), lambda i: (i, 0)),
        out_shape=jax.ShapeDtypeStruct((n, d), F32),
        compiler_params=_params(("arbitrary",), 48),
        name="merge",
    )(oa, ob, oc, z, z, z, x2d, wb, wo)


def _bucket_lookup(table, bucket):
    out = jnp.zeros(bucket.shape + (table.shape[1],), F32)
    for b in range(N_BUCKETS):
        out = jnp.where((bucket == b)[..., None], table[b].astype(F32), out)
    return out


def _window_table(bias_a):
    ki = jnp.arange(2 * WINDOW)[:, None]
    qi = jnp.arange(WINDOW)[None, :]
    delta = qi + WINDOW - ki
    band = (delta >= 0) & (delta < WINDOW)
    bias = _bucket_lookup(bias_a, _t5_bucket(delta)).transpose(2, 0, 1) * LOG2E
    return jnp.where(band[None], bias, NEG)


def _near_table(bias_b):
    ki = jnp.arange(ATT_T)[:, None]
    qi = jnp.arange(ATT_T)[None, :]
    d = jnp.arange(B_NEAR)[:, None, None]
    delta = d * ATT_T + qi[None] - ki[None]
    bias = _bucket_lookup(bias_b, _t5_bucket(delta))
    far = bias_b[N_BUCKETS - 1].astype(F32)
    return ((bias - far) * LOG2E).transpose(3, 0, 1, 2)


def kernel(x, norm_gain, w_in, b_forget, qk_gain, sinks, w_branch, w_out, rel_bias):
    batch, seq, d = x.shape
    depth = norm_gain.shape[0]
    assert w_in.shape[-1] == IN_COLS and seq % 1024 == 0 and d == 1024
    assert (B_NEAR - 1) * ATT_T - (ATT_T - 1) < T5_FAR <= B_NEAR * ATT_T - (ATT_T - 1)

    w_perm = _gather_cols(w_in, _COLS).astype(BF16)
    w_vt = jnp.swapaxes(_gather_cols(w_in, _VT_COLS), 1, 2).astype(BF16)
    w_st = jnp.swapaxes(_gather_cols(w_in, _ST_COLS), 1, 2).astype(BF16)
    wb = w_branch.astype(BF16)
    wo = w_out.astype(BF16)

    gmat = jnp.asarray(np.kron(np.eye(N_HEADS), np.ones((HEAD_DIM, HEAD_DIM))), BF16)
    prep_t = 512
    lmat = jnp.asarray(np.tril(np.ones((prep_t, prep_t))), BF16)
    lmat_sel = lmat[:SEL_MC, :SEL_MC]
    tbl_a = _window_table(rel_bias[:, :N_HEADS])
    tbl_b = _near_table(rel_bias[:, N_HEADS:])

    x2d = x.reshape(batch * seq, d)
    for layer in range(depth):
        g = qk_gain[layer]
        qscale = ATTN_SCALE * LOG2E
        gains = jnp.zeros((8, 512), F32)
        gains = gains.at[0].set(jnp.tile(g[0, 0], N_HEADS) * qscale)
        gains = gains.at[1, :256].set(jnp.tile(g[0, 1], 4))
        gains = gains.at[2].set(jnp.tile(g[1, 0], N_HEADS) * qscale)
        gains = gains.at[3, :256].set(jnp.tile(g[1, 1], 4))
        gains = gains.at[4].set(jnp.tile(g[2, 0], N_HEADS) * qscale)
        gains = gains.at[5].set(jnp.tile(g[2, 1], N_HEADS))
        gains = gains.at[6, MISC_CF:MISC_CF + N_HEADS].set(b_forget[layer])

        z, misc, vt, st = _proj(x2d, norm_gain[layer][None, :], w_perm[layer], w_vt[layer], w_st[layer])
        qa, ka, qb, kb, qc, kc, cum = _prep(z, misc, gains, gmat, lmat, batch, seq)
        cum_t = cum.reshape(batch, seq, LANES)[:, :, MISC_CF:MISC_CF + N_HEADS].transpose(0, 2, 1)
        oa = _mix_a(qa, ka, z, vt, tbl_a, sinks[layer], batch, seq)
        maskadd = _sel_b(z, st, lmat_sel, batch, seq)
        ob = _mix_b(qb, kb, z, vt, maskadd, tbl_b, batch, seq)
        oc = _mix_c(qc, kc, z, vt, cum, cum_t, batch, seq)
        x2d = _merge(oa, ob, oc, z, x2d, wb[layer], wo[layer])
    return x2d.reshape(batch, seq, d)
```

```python
import functools
import math

import numpy as np
import jax
import jax.numpy as jnp
from jax import lax
from jax.experimental import pallas as pl
from jax.experimental.pallas import tpu as pltpu

F32 = jnp.float32
BF16 = jnp.bfloat16

HEAD_DIM = 64
LANES = 128
N_HEADS = 8
N_PAIRS = N_HEADS // 2
KV_GROUP = 4
WINDOW = 128
IDX_HEADS = 4
IDX_DIM = 32
TOPK_MAX = 256
N_BUCKETS = 32
MAX_DISTANCE = 512
EPS = 1e-6
ATTN_SCALE = HEAD_DIM ** -0.5
LOG2E = math.log2(math.e)
NEG = -1e30
T5_FAR = 413

_SIZES = dict(a_q=512, a_k=128, a_v=128, a_gate=512,
              b_q=512, b_k=128, b_v=128, b_iq=128, b_ik=32, b_iw=4, b_gate=512,
              c_q=512, c_k=512, c_v=512, c_f=8, c_gate=512, merge=3072)
_ORIG = {}
_o = 0
for _k, _v in _SIZES.items():
    _ORIG[_k] = _o
    _o += _v
IN_COLS = _o

OFF_MERGE = 0
OFF_AQ, OFF_AG, OFF_BQ, OFF_BG = 3072, 3584, 4096, 4608
OFF_CQ, OFF_CK, OFF_CG = 5120, 5632, 6144
OFF_AK, OFF_BK = 6656, 6912
OFF_BIQ, OFF_BIK, OFF_MISC = 7168, 7296, 7424
NP_COLS = 7680
PROJ_TN = 1536
MISC_CF = 8
VT_C, VT_A, VT_B, VT_ROWS = 0, 512, 640, 768


def _orig(name, idx=None):
    src = np.arange(_SIZES[name]) + _ORIG[name]
    return src if idx is None else src[idx]


def _column_map():
    cols = np.full((NP_COLS,), -1, np.int64)

    def put(off, src):
        cols[off:off + len(src)] = src

    for off, name in ((OFF_MERGE, "merge"), (OFF_AQ, "a_q"), (OFF_AG, "a_gate"), (OFF_BQ, "b_q"),
                      (OFF_BG, "b_gate"), (OFF_CQ, "c_q"), (OFF_CK, "c_k"), (OFF_CG, "c_gate"),
                      (OFF_BIQ, "b_iq")):
        put(off, _orig(name))
    dup = np.concatenate([np.arange(64), np.arange(64), np.arange(64, 128), np.arange(64, 128)])
    put(OFF_AK, _orig("a_k", dup))
    put(OFF_BK, _orig("b_k", dup))
    put(OFF_BIK, _orig("b_ik", np.tile(np.arange(IDX_DIM), IDX_HEADS)))
    put(OFF_MISC + MISC_CF, _orig("c_f"))
    return cols


def _gather_cols(w, cols):
    parts = []
    start = 0
    while start < len(cols):
        end = start + 1
        if cols[start] < 0:
            while end < len(cols) and cols[end] < 0:
                end += 1
            parts.append(jnp.zeros(w.shape[:-1] + (end - start,), w.dtype))
        else:
            while end < len(cols) and cols[end] == cols[end - 1] + 1:
                end += 1
            parts.append(w[..., int(cols[start]):int(cols[end - 1]) + 1])
        start = end
    return jnp.concatenate(parts, axis=-1)


_COLS = _column_map()
_VT_COLS = np.concatenate([_orig("c_v"), _orig("a_v"), _orig("b_v")])
_ST_COLS = np.concatenate([_orig("b_iw"), np.full((4,), -1, np.int64)])


def _params(sem, vmem_mb):
    return pltpu.CompilerParams(dimension_semantics=sem, vmem_limit_bytes=vmem_mb * 1024 * 1024)


def _dot(a, b):
    return jnp.dot(a, b, preferred_element_type=F32)


def _dot_nt(a, b):
    return lax.dot_general(a, b, (((1,), (1,)), ((), ())), preferred_element_type=F32)


def _t5_bucket(delta):
    n = jnp.maximum(delta, 0)
    max_exact = N_BUCKETS // 2
    nf = jnp.maximum(n, 1).astype(F32)
    large = max_exact + (jnp.log(nf / max_exact) / math.log(MAX_DISTANCE / max_exact)
                         * (N_BUCKETS - max_exact)).astype(jnp.int32)
    large = jnp.minimum(large, N_BUCKETS - 1)
    return jnp.where(n < max_exact, n, large)


def _proj_kernel(x_ref, g_ref, w_ref, wvt_ref, wst_ref, z_ref, misc_ref, vt_ref, st_ref, h_ref,
                 *, misc_tile, misc_local):
    j = pl.program_id(1)

    @pl.when(j == 0)
    def _():
        x = x_ref[...]
        ms = jnp.mean(x * x, axis=-1, keepdims=True)
        h = (x * lax.rsqrt(ms + EPS) * g_ref[...]).astype(BF16)
        h_ref[...] = h
        vt_ref[...] = _dot_nt(wvt_ref[...], h).astype(BF16)
        st_ref[...] = _dot_nt(wst_ref[...], h)

    acc = _dot(h_ref[...], w_ref[...])
    z_ref[...] = acc.astype(BF16)

    @pl.when(j == misc_tile)
    def _():
        misc_ref[...] = acc[:, misc_local:misc_local + LANES]


def _proj(x2d, gain, w, wvt, wst):
    n, d = x2d.shape
    tm = 1024
    return pl.pallas_call(
        functools.partial(_proj_kernel, misc_tile=OFF_MISC // PROJ_TN, misc_local=OFF_MISC % PROJ_TN),
        grid=(n // tm, NP_COLS // PROJ_TN),
        in_specs=[pl.BlockSpec((tm, d), lambda i, j: (i, 0)),
                  pl.BlockSpec((1, d), lambda i, j: (0, 0)),
                  pl.BlockSpec((d, PROJ_TN), lambda i, j: (0, j)),
                  pl.BlockSpec((VT_ROWS, d), lambda i, j: (0, 0)),
                  pl.BlockSpec((8, d), lambda i, j: (0, 0))],
        out_specs=[pl.BlockSpec((tm, PROJ_TN), lambda i, j: (i, j)),
                   pl.BlockSpec((tm, LANES), lambda i, j: (i, 0)),
                   pl.BlockSpec((VT_ROWS, tm), lambda i, j: (0, i)),
                   pl.BlockSpec((8, tm), lambda i, j: (0, i))],
        out_shape=[jax.ShapeDtypeStruct((n, NP_COLS), BF16),
                   jax.ShapeDtypeStruct((n, LANES), F32),
                   jax.ShapeDtypeStruct((VT_ROWS, n), BF16),
                   jax.ShapeDtypeStruct((8, n), F32)],
        scratch_shapes=[pltpu.VMEM((tm, d), BF16)],
        compiler_params=_params(("arbitrary", "arbitrary"), 56),
        name="proj",
    )(x2d, gain, w, wvt, wst)


def _group_rms(x_bf16, g_ref, gain_row):
    x = x_bf16.astype(F32)
    width = x.shape[-1]
    sq = x * x
    hi = sq.astype(BF16)
    lo = (sq - hi.astype(F32)).astype(BF16)
    g = g_ref[:width, :width]
    ss = _dot(hi, g) + _dot(lo, g)
    return (x * lax.rsqrt(ss * (1.0 / HEAD_DIM) + EPS) * gain_row).astype(BF16)


def _prep_kernel(aq, ak, bq, bk, cq, ck, misc, gains, gmat, lmat,
                 qa_o, ka_o, qb_o, kb_o, qc_o, kc_o, cum_o, carry):
    t = pl.program_id(1)
    qa_o[...] = _group_rms(aq[...], gmat, gains[0:1, :])
    ka_o[...] = _group_rms(ak[...], gmat, gains[1:2, :256])
    qb_o[...] = _group_rms(bq[...], gmat, gains[2:3, :])
    kb_o[...] = _group_rms(bk[...], gmat, gains[3:4, :256])
    qc_o[...] = _group_rms(cq[...], gmat, gains[4:5, :])
    kc_o[...] = _group_rms(ck[...], gmat, gains[5:6, :])

    @pl.when(t == 0)
    def _():
        carry[...] = jnp.zeros_like(carry)

    xm = misc[...] + gains[6:7, :LANES]
    logf = jnp.minimum(xm, 0.0) - jnp.log(1.0 + jnp.exp(-jnp.abs(xm)))
    hi = logf.astype(BF16)
    r1 = logf - hi.astype(F32)
    mid = r1.astype(BF16)
    lo = (r1 - mid.astype(F32)).astype(BF16)
    lm = lmat[...]
    c = _dot(lm, hi) + _dot(lm, mid) + _dot(lm, lo) + carry[...]
    cum_o[...] = c * LOG2E
    rows = c.shape[0]
    carry[...] = c[rows - 1:rows, :]


def _prep(z, misc, gains, gmat, lmat, batch, seq):
    n = z.shape[0]
    tp = lmat.shape[0]
    nt = seq // tp

    def zspec(width, off):
        return pl.BlockSpec((tp, width), lambda b, t, _c=off // width: (b * nt + t, _c))

    def ospec(width):
        return pl.BlockSpec((tp, width), lambda b, t: (b * nt + t, 0))

    return pl.pallas_call(
        _prep_kernel,
        grid=(batch, nt),
        in_specs=[zspec(512, OFF_AQ), zspec(256, OFF_AK), zspec(512, OFF_BQ), zspec(256, OFF_BK),
                  zspec(512, OFF_CQ), zspec(512, OFF_CK), ospec(LANES),
                  pl.BlockSpec((8, 512), lambda b, t: (0, 0)),
                  pl.BlockSpec((512, 512), lambda b, t: (0, 0)),
                  pl.BlockSpec((tp, tp), lambda b, t: (0, 0))],
        out_specs=[ospec(512), ospec(256), ospec(512), ospec(256), ospec(512), ospec(512), ospec(LANES)],
        out_shape=[jax.ShapeDtypeStruct((n, 512), BF16), jax.ShapeDtypeStruct((n, 256), BF16),
                   jax.ShapeDtypeStruct((n, 512), BF16), jax.ShapeDtypeStruct((n, 256), BF16),
                   jax.ShapeDtypeStruct((n, 512), BF16), jax.ShapeDtypeStruct((n, 512), BF16),
                   jax.ShapeDtypeStruct((n, LANES), F32)],
        scratch_shapes=[pltpu.VMEM((1, LANES), F32)],
        compiler_params=_params(("arbitrary", "arbitrary"), 48),
        name="prep",
    )(z, z, z, z, z, z, misc, gains, gmat, lmat)


def _half_mask(shape, half):
    lane = lax.broadcasted_iota(jnp.int32, shape, len(shape) - 1)
    return (lane < HEAD_DIM) if half == 0 else (lane >= HEAD_DIM)


def _head_q(q_ref, h):
    p = h // 2
    qp = q_ref[:, p * LANES:(p + 1) * LANES]
    return jnp.where(_half_mask(qp.shape, h % 2), qp, jnp.zeros_like(qp))


DEN_ROWS = 16
ACC_ROWS = HEAD_DIM + DEN_ROWS


def _with_ones(vt):
    return jnp.concatenate([vt, jnp.ones((DEN_ROWS, vt.shape[1]), vt.dtype)], axis=0)


def _softmax_step(h, s, vt, m_ref, acc_ref, qbias=None):
    m_prev = m_ref[h]
    s_max = jnp.max(s, axis=0, keepdims=True)
    m_new = jnp.maximum(m_prev, s_max if qbias is None else s_max + qbias)
    alpha = jnp.exp2(m_prev - m_new)
    shift = m_new[0:1, :] if qbias is None else m_new[0:1, :] - qbias
    p = jnp.exp2(s - shift)
    acc_ref[h] = alpha[0:1, :] * acc_ref[h] + _dot(_with_ones(vt), p.astype(BF16))
    m_ref[h] = m_new


HEADS_AHEAD = 4


def _heads_pipelined(logits, values, m_ref, acc_ref, qbias=None):
    ahead = [logits(h) for h in range(HEADS_AHEAD)]
    for h in range(N_HEADS):
        if h + HEADS_AHEAD < N_HEADS:
            ahead.append(logits(h + HEADS_AHEAD))
        _softmax_step(h, ahead.pop(0), values(h), m_ref, acc_ref,
                      None if qbias is None else qbias(h))


def _init_state(m_ref, acc_ref):
    m_ref[...] = jnp.full(m_ref.shape, NEG, F32)
    acc_ref[...] = jnp.zeros(acc_ref.shape, F32)


def _gated_pair(p, gate, acc_ref):
    halves = []
    for h in (2 * p, 2 * p + 1):
        a = acc_ref[h]
        halves.append(a[:HEAD_DIM] / a[HEAD_DIM:HEAD_DIM + 1])
    ot = jnp.concatenate(halves, axis=0)
    g = gate.astype(F32)
    return (ot.T * (g * jax.nn.sigmoid(g))).astype(BF16)


def _mix_a_kernel(sink_ref, q_ref, kp_ref, kc_ref, vp_ref, vc_ref, gate_ref, tbl_ref, o_ref, acc_ref):
    n = pl.program_id(1)
    pad_pen = jnp.where(n > 0, 0.0, NEG).astype(F32)
    den_rows = lax.broadcasted_iota(jnp.int32, (ACC_ROWS, WINDOW), 0) >= HEAD_DIM

    def logits(h):
        q = _head_q(q_ref, h)
        ksl = slice((h // KV_GROUP) * LANES, (h // KV_GROUP + 1) * LANES)
        return (_dot_nt(kp_ref[:, ksl], q) + tbl_ref[h, :WINDOW, :] + pad_pen,
                _dot_nt(kc_ref[:, ksl], q) + tbl_ref[h, WINDOW:, :])

    all_logits = [logits(h) for h in range(N_HEADS)]
    for h in range(N_HEADS):
        kv = h // KV_GROUP
        vsl = slice(kv * HEAD_DIM, (kv + 1) * HEAD_DIM)
        s_prev, s_cur = all_logits[h]
        sink = sink_ref[h] * LOG2E
        m = jnp.maximum(jnp.maximum(jnp.max(s_prev, axis=0, keepdims=True),
                                    jnp.max(s_cur, axis=0, keepdims=True)), sink)
        p_prev = jnp.exp2(s_prev - m)
        p_cur = jnp.exp2(s_cur - m)
        acc = (_dot(_with_ones(vp_ref[vsl, :]), p_prev.astype(BF16))
               + _dot(_with_ones(vc_ref[vsl, :]), p_cur.astype(BF16)))
        acc_ref[h] = acc + jnp.where(den_rows, jnp.exp2(sink - m), 0.0)
    for p in range(N_PAIRS):
        sl = slice(p * LANES, (p + 1) * LANES)
        o_ref[:, sl] = _gated_pair(p, gate_ref[:, sl], acc_ref)


def _mix_a(qa, ka, z, vt, tbl, sinks, batch, seq):
    n = qa.shape[0]
    t = WINDOW
    nb = seq // t
    return pl.pallas_call(
        _mix_a_kernel,
        grid=(batch, nb),
        in_specs=[pl.BlockSpec(memory_space=pltpu.SMEM),
                  pl.BlockSpec((t, 512), lambda b, i: (b * nb + i, 0)),
                  pl.BlockSpec((t, 256), lambda b, i: (b * nb + jnp.maximum(i - 1, 0), 0)),
                  pl.BlockSpec((t, 256), lambda b, i: (b * nb + i, 0)),
                  pl.BlockSpec((LANES, t), lambda b, i: (VT_A // LANES, b * nb + jnp.maximum(i - 1, 0))),
                  pl.BlockSpec((LANES, t), lambda b, i: (VT_A // LANES, b * nb + i)),
                  pl.BlockSpec((t, 512), lambda b, i: (b * nb + i, OFF_AG // 512)),
                  pl.BlockSpec((N_HEADS, 2 * t, t), lambda b, i: (0, 0, 0))],
        out_specs=pl.BlockSpec((t, 512), lambda b, i: (b * nb + i, 0)),
        out_shape=jax.ShapeDtypeStruct((n, 512), BF16),
        scratch_shapes=[pltpu.VMEM((N_HEADS, ACC_ROWS, t), F32)],
        compiler_params=_params(("arbitrary", "arbitrary"), 32),
        name="mix_a",
    )(sinks, qa, ka, ka, vt, vt, z, tbl)


SEL_TQ = 256
SEL_SC = 512
SEL_SUB = 64
SEL_MC = 256
SEL_FAST_ITERS = 24
SEL_MAX_ITERS = 600


def _sel_b_kernel(iq_ref, ik_ref, wt_ref, lmat_ref, o_ref, sc_ref, *, topk):
    n = pl.program_id(1)
    t0 = n * SEL_TQ
    n_full = n // (SEL_SC // SEL_TQ)
    n_sc = n_full + 1
    n_mc = n + 1
    kf = float(topk)

    iq = iq_ref[...]
    lane = lax.broadcasted_iota(jnp.int32, iq.shape, 1)
    qs = jnp.concatenate(
        [jnp.where((lane >= h * IDX_DIM) & (lane < (h + 1) * IDX_DIM), iq, jnp.zeros_like(iq))
         for h in range(IDX_HEADS)], axis=0)
    wscale = (IDX_HEADS ** -0.5) * (IDX_DIM ** -0.5)
    w = [wt_ref[h:h + 1, :] * wscale for h in range(IDX_HEADS)]
    qpos = t0 + lax.broadcasted_iota(jnp.int32, (SEL_SUB, SEL_TQ), 1)
    kiota = lax.broadcasted_iota(jnp.int32, (SEL_SUB, SEL_TQ), 0)

    def score_chunk(c, carry, diagonal):
        rmax, rmin, ge0, gt0, minpos = carry
        base = pl.multiple_of(c * SEL_SC, SEL_SC)
        raw = _dot_nt(ik_ref[pl.ds(base, SEL_SC), :], qs)
        for u in range(SEL_SC // SEL_SUB):
            rows = slice(u * SEL_SUB, (u + 1) * SEL_SUB)
            score = w[0] * jnp.maximum(raw[rows, 0:SEL_TQ], 0.0)
            for h in range(1, IDX_HEADS):
                score = score + w[h] * jnp.maximum(raw[rows, h * SEL_TQ:(h + 1) * SEL_TQ], 0.0)
            rmin = jnp.minimum(rmin, score)
            if diagonal:
                score = jnp.where((kiota + (base + u * SEL_SUB)) <= qpos, score, -jnp.inf)
            sc_ref[pl.ds(base + u * SEL_SUB, SEL_SUB), :] = score
            rmax = jnp.maximum(rmax, score)
            pos = score > 0.0
            ge0 = ge0 + jnp.where(score >= 0.0, 1.0, 0.0)
            gt0 = gt0 + jnp.where(pos, 1.0, 0.0)
            minpos = jnp.minimum(minpos, jnp.where(pos, score, jnp.inf))
        return rmax, rmin, ge0, gt0, minpos

    def slab(v):
        return jnp.full((SEL_SUB, SEL_TQ), v, F32)

    carry = lax.fori_loop(0, n_full, lambda c, cr: score_chunk(c, cr, False),
                          (slab(-jnp.inf), slab(jnp.inf), slab(0.0), slab(0.0), slab(jnp.inf)))
    rmax, rmin, ge0, gt0, minpos = score_chunk(n_full, carry, True)
    rmax = jnp.max(rmax, axis=0, keepdims=True)
    rmin = jnp.min(rmin, axis=0, keepdims=True)
    ge0 = jnp.sum(ge0, axis=0, keepdims=True)
    gt0 = jnp.sum(gt0, axis=0, keepdims=True)
    minpos = jnp.min(minpos, axis=0, keepdims=True)

    navail = (t0 + 1 + lax.broadcasted_iota(jnp.int32, (1, SEL_TQ), 1)).astype(F32)

    def sweep(mid, snap):
        midb = jnp.broadcast_to(mid, (SEL_SUB, SEL_TQ))

        def body(c, carry):
            base = pl.multiple_of(c * SEL_SC, SEL_SC)
            cnt, up, dn = carry
            for u in range(SEL_SC // SEL_SUB):
                x = sc_ref[pl.ds(base + u * SEL_SUB, SEL_SUB), :]
                ge = x >= midb
                cnt = cnt + jnp.where(ge, 1.0, 0.0)
                if snap:
                    up = jnp.minimum(up, jnp.where(ge, x, jnp.inf))
                    dn = jnp.maximum(dn, jnp.where(ge, -jnp.inf, x))
            return cnt, up, dn

        init = (jnp.zeros((SEL_SUB, SEL_TQ), F32), jnp.full((SEL_SUB, SEL_TQ), jnp.inf, F32),
                jnp.full((SEL_SUB, SEL_TQ), -jnp.inf, F32))
        cnt, up, dn = lax.fori_loop(0, n_sc, body, init)
        return (jnp.sum(cnt, axis=0, keepdims=True), jnp.min(up, axis=0, keepdims=True),
                jnp.max(dn, axis=0, keepdims=True))

    def advance(st, mid, snap):
        it, lo, hi, hi_dn, c_lo, c_hi, done = st
        cnt, up, dn = sweep(mid, snap)
        live = jnp.where(done > 0.5, 0.0, jnp.where(mid <= lo, 0.0, jnp.where(mid >= hi, 0.0, 1.0)))
        ge = jnp.where(cnt >= kf, live, 0.0) > 0.5
        lt = jnp.where(cnt >= kf, 0.0, live) > 0.5
        lo = jnp.where(ge, up if snap else mid, lo)
        c_lo = jnp.where(ge, cnt, c_lo)
        hi = jnp.where(lt, mid, hi)
        hi_dn = jnp.where(lt, dn if snap else jnp.inf, hi_dn)
        c_hi = jnp.where(lt, cnt, c_hi)
        done = jnp.where(live < 0.5, 1.0, jnp.where(cnt == kf, 1.0, jnp.where(hi_dn <= lo, 1.0, 0.0)))
        return it + 1, lo, hi, hi_dn, c_lo, c_hi, done

    def make_step(snap, repeat=1):
        def step(st):
            for _ in range(repeat):
                st = advance(st, 0.5 * st[1] + 0.5 * st[2], snap)
            return st
        return step

    def make_cond(limit):
        def cond(st):
            return jnp.logical_and(st[0] < limit, jnp.min(st[6]) < 0.5)
        return cond

    inf_row = jnp.full((1, SEL_TQ), jnp.inf, F32)
    all_in = navail <= kf
    above = jnp.logical_and(jnp.logical_not(all_in), gt0 >= kf)
    at_zero = jnp.logical_and(jnp.logical_not(all_in), jnp.logical_and(gt0 < kf, ge0 >= kf))
    below = jnp.logical_and(jnp.logical_not(all_in), ge0 < kf)
    lo = jnp.where(above, minpos, jnp.where(at_zero, 0.0, rmin))
    c_lo = jnp.where(above, gt0, jnp.where(at_zero, ge0, navail))
    hi = jnp.where(at_zero, minpos, jnp.where(below, 0.0, inf_row))
    c_hi = jnp.where(at_zero, gt0, jnp.where(below, ge0, 0.0))
    hi_dn = jnp.where(above, rmax, inf_row)
    done = jnp.where(all_in, 1.0, jnp.where(at_zero, 1.0, 0.0))
    st = (jnp.int32(0), lo, hi, hi_dn, c_lo, c_hi, done)
    st = advance(st, jnp.where(above, rmax, 0.5 * lo + 0.5 * hi), False)
    st = lax.while_loop(make_cond(SEL_FAST_ITERS), make_step(False, repeat=2), st)
    st = lax.while_loop(make_cond(SEL_MAX_ITERS), make_step(True), st)
    _, lo, hi, _, c_lo, c_hi, _ = st
    need = kf - c_hi
    ties = jnp.max(jnp.where(navail > kf, c_lo, kf)) > kf

    o_ref[...] = jnp.full(o_ref.shape, NEG, o_ref.dtype)

    @pl.when(jnp.logical_not(ties))
    def _():
        def mask_body(c, carry):
            base = pl.multiple_of(c * SEL_MC, SEL_MC)
            x = sc_ref[pl.ds(base, SEL_MC), :]
            o_ref[0, pl.ds(base, SEL_MC), :] = jnp.where(x >= lo, 0.0, NEG).astype(o_ref.dtype)
            return carry
        lax.fori_loop(0, n_mc, mask_body, 0)

    @pl.when(ties)
    def _():
        lmat = lmat_ref[...]

        def mask_body(c, run):
            base = pl.multiple_of(c * SEL_MC, SEL_MC)
            x = sc_ref[pl.ds(base, SEL_MC), :]
            bnd = jnp.where(x >= lo, jnp.where(x < hi, 1.0, 0.0), 0.0)
            incl = _dot(lmat, bnd.astype(BF16)) + run
            keep = jnp.where(x >= hi, 1.0, jnp.where(incl - bnd < need, bnd, 0.0))
            o_ref[0, pl.ds(base, SEL_MC), :] = jnp.where(keep > 0.5, 0.0, NEG).astype(o_ref.dtype)
            return incl[SEL_MC - 1:SEL_MC, :]
        lax.fori_loop(0, n_mc, mask_body, jnp.zeros((1, SEL_TQ), F32))


def _sel_b(z, st, lmat, batch, seq):
    nb = seq // SEL_TQ
    topk = min(TOPK_MAX, seq // 4)
    return pl.pallas_call(
        functools.partial(_sel_b_kernel, topk=topk),
        grid=(batch, nb),
        in_specs=[pl.BlockSpec((SEL_TQ, LANES), lambda b, i: (b * nb + i, OFF_BIQ // LANES)),
                  pl.BlockSpec((seq, LANES), lambda b, i: (b, OFF_BIK // LANES)),
                  pl.BlockSpec((8, SEL_TQ), lambda b, i: (0, b * nb + i)),
                  pl.BlockSpec((SEL_MC, SEL_MC), lambda b, i: (0, 0))],
        out_specs=pl.BlockSpec((1, seq, SEL_TQ), lambda b, i: (b, 0, i)),
        out_shape=jax.ShapeDtypeStruct((batch, seq, seq), BF16),
        scratch_shapes=[pltpu.VMEM((seq, SEL_TQ), F32)],
        compiler_params=_params(("arbitrary", "arbitrary"), 48),
        name="sel_b",
    )(z, z, st, lmat)


ATT_T = 256
B_NEAR = 3
FAR_TILES = 2


def _mix_b_kernel(q_ref, k_ref, vt_ref, mask_ref, gate_ref, tbl_ref, o_ref, m_ref, acc_ref):
    i = pl.program_id(1)
    _init_state(m_ref, acc_ref)
    qs = [_head_q(q_ref, h) for h in range(N_HEADS)]

    def tile(j, size, near):
        base = pl.multiple_of(j * ATT_T, ATT_T)
        madd = mask_ref[0, pl.ds(base, size), :].astype(F32)

        def logits(h):
            kv = h // KV_GROUP
            s = _dot_nt(k_ref[pl.ds(base, size), kv * LANES:(kv + 1) * LANES], qs[h]) + madd
            return s if near is None else s + tbl_ref[h, near]

        def values(h):
            kv = h // KV_GROUP
            return vt_ref[kv * HEAD_DIM:(kv + 1) * HEAD_DIM, pl.ds(base, size)]

        _heads_pipelined(logits, values, m_ref, acc_ref)

    def far_body(j, carry):
        tile(j * FAR_TILES, FAR_TILES * ATT_T, None)
        return carry

    n_far = jnp.maximum(i - (B_NEAR - 1), 0)
    lax.fori_loop(0, n_far // FAR_TILES, far_body, 0)

    @pl.when(n_far % FAR_TILES == 1)
    def _():
        tile(n_far - 1, ATT_T, None)

    for d in range(B_NEAR - 1, -1, -1):
        @pl.when(i >= d)
        def _(d=d):
            tile(i - d, ATT_T, d)
    for p in range(N_PAIRS):
        sl = slice(p * LANES, (p + 1) * LANES)
        o_ref[:, sl] = _gated_pair(p, gate_ref[:, sl], acc_ref)


def _mix_b(qb, kb, z, vt, maskadd, tbl, batch, seq):
    n = qb.shape[0]
    t = ATT_T
    nq = seq // t
    return pl.pallas_call(
        _mix_b_kernel,
        grid=(batch, nq),
        in_specs=[pl.BlockSpec((t, 512), lambda b, i: (b * nq + i, 0)),
                  pl.BlockSpec((seq, 256), lambda b, i: (b, 0)),
                  pl.BlockSpec((LANES, seq), lambda b, i: (VT_B // LANES, b)),
                  pl.BlockSpec((1, seq, t), lambda b, i: (b, 0, i)),
                  pl.BlockSpec((t, 512), lambda b, i: (b * nq + i, OFF_BG // 512)),
                  pl.BlockSpec((N_HEADS, B_NEAR, t, t), lambda b, i: (0, 0, 0, 0))],
        out_specs=pl.BlockSpec((t, 512), lambda b, i: (b * nq + i, 0)),
        out_shape=jax.ShapeDtypeStruct((n, 512), BF16),
        scratch_shapes=[pltpu.VMEM((N_HEADS, 8, t), F32), pltpu.VMEM((N_HEADS, ACC_ROWS, t), F32)],
        compiler_params=_params(("arbitrary", "arbitrary"), 56),
        name="mix_b",
    )(qb, kb, vt, maskadd, z, tbl)


def _mix_c_kernel(q_ref, k_ref, vt_ref, ck_ref, cq_ref, gate_ref, o_ref, m_ref, acc_ref):
    i = pl.program_id(1)
    _init_state(m_ref, acc_ref)
    qs = [_head_q(q_ref, h) for h in range(N_HEADS)]
    q0 = pl.multiple_of(i * ATT_T, ATT_T)
    cq = cq_ref[0, :, pl.ds(q0, ATT_T)]
    krow = lax.broadcasted_iota(jnp.int32, (ATT_T, ATT_T), 0)
    qcol = lax.broadcasted_iota(jnp.int32, (ATT_T, ATT_T), 1)

    def tile(j, size, diag):
        base = pl.multiple_of(j * ATT_T, ATT_T)
        ck_all = ck_ref[pl.ds(base, size), :]

        def logits(h):
            p = h // 2
            ck = ck_all[:, MISC_CF + h:MISC_CF + h + 1]
            s = _dot_nt(k_ref[pl.ds(base, size), p * LANES:(p + 1) * LANES], qs[h]) - ck
            return jnp.where(krow <= qcol, s, NEG) if diag else s

        def values(h):
            return vt_ref[h * HEAD_DIM:(h + 1) * HEAD_DIM, pl.ds(base, size)]

        _heads_pipelined(logits, values, m_ref, acc_ref, qbias=lambda h: cq[h:h + 1, :])

    def body(j, carry):
        tile(j * FAR_TILES, FAR_TILES * ATT_T, False)
        return carry

    lax.fori_loop(0, i // FAR_TILES, body, 0)

    @pl.when(i % FAR_TILES == 1)
    def _():
        tile(i - 1, ATT_T, False)

    tile(i, ATT_T, True)
    for p in range(N_PAIRS):
        sl = slice(p * LANES, (p + 1) * LANES)
        o_ref[:, sl] = _gated_pair(p, gate_ref[:, sl], acc_ref)


def _mix_c(qc, kc, z, vt, cum, cum_t, batch, seq):
    n = qc.shape[0]
    t = ATT_T
    nq = seq // t
    return pl.pallas_call(
        _mix_c_kernel,
        grid=(batch, nq),
        in_specs=[pl.BlockSpec((t, 512), lambda b, i: (b * nq + i, 0)),
                  pl.BlockSpec((seq, 512), lambda b, i: (b, 0)),
                  pl.BlockSpec((512, seq), lambda b, i: (VT_C // 512, b)),
                  pl.BlockSpec((seq, LANES), lambda b, i: (b, 0)),
                  pl.BlockSpec((1, N_HEADS, seq), lambda b, i: (b, 0, 0)),
                  pl.BlockSpec((t, 512), lambda b, i: (b * nq + i, OFF_CG // 512))],
        out_specs=pl.BlockSpec((t, 512), lambda b, i: (b * nq + i, 0)),
        out_shape=jax.ShapeDtypeStruct((n, 512), BF16),
        scratch_shapes=[pltpu.VMEM((N_HEADS, 8, t), F32), pltpu.VMEM((N_HEADS, ACC_ROWS, t), F32)],
        compiler_params=_params(("arbitrary", "arbitrary"), 56),
        name="mix_c",
    )(qc, kc, vt, cum, cum_t, z)


def _merge_kernel(oa, ob, oc, ga, gb, gc, x_ref, wb_ref, wo_ref, o_ref):
    merged = None
    for br, (o, g) in enumerate(((oa, ga), (ob, gb), (oc, gc))):
        y = _dot(o[...], wb_ref[br])
        term = jax.nn.sigmoid(g[...].astype(F32)) * y
        merged = term if merged is None else merged + term
    o_ref[...] = x_ref[...] + _dot(merged.astype(BF16), wo_ref[...])


def _merge(oa, ob, oc, z, x2d, wb, wo):
    n, d = x2d.shape
    tm = 512

    def ospec():
        return pl.BlockSpec((tm, 512), lambda i: (i, 0))

    def gspec(k):
        return pl.BlockSpec((tm, d), lambda i, _k=k: (i, OFF_MERGE // d + _k))

    return pl.pallas_call(
        _merge_kernel,
        grid=(n // tm,),
        in_specs=[ospec(), ospec(), ospec(), gspec(0), gspec(1), gspec(2),
                  pl.BlockSpec((tm, d), lambda i: (i, 0)),
                  pl.BlockSpec((3, 512, d), lambda i: (0, 0, 0)),
                  pl.BlockSpec((d, d), lambda i: (0, 0))],
        out_specs=pl.BlockSpec((tm, d), lambda i: (i, 0)),
        out_shape=jax.ShapeDtypeStruct((n, d), F32),
        compiler_params=_params(("arbitrary",), 48),
        name="merge",
    )(oa, ob, oc, z, z, z, x2d, wb, wo)


def _bucket_lookup(table, bucket):
    out = jnp.zeros(bucket.shape + (table.shape[1],), F32)
    for b in range(N_BUCKETS):
        out = jnp.where((bucket == b)[..., None], table[b].astype(F32), out)
    return out


def _window_table(bias_a):
    ki = jnp.arange(2 * WINDOW)[:, None]
    qi = jnp.arange(WINDOW)[None, :]
    delta = qi + WINDOW - ki
    band = (delta >= 0) & (delta < WINDOW)
    bias = _bucket_lookup(bias_a, _t5_bucket(delta)).transpose(2, 0, 1) * LOG2E
    return jnp.where(band[None], bias, NEG)


def _near_table(bias_b):
    ki = jnp.arange(ATT_T)[:, None]
    qi = jnp.arange(ATT_T)[None, :]
    d = jnp.arange(B_NEAR)[:, None, None]
    delta = d * ATT_T + qi[None] - ki[None]
    bias = _bucket_lookup(bias_b, _t5_bucket(delta))
    far = bias_b[N_BUCKETS - 1].astype(F32)
    return ((bias - far) * LOG2E).transpose(3, 0, 1, 2)


def kernel(x, norm_gain, w_in, b_forget, qk_gain, sinks, w_branch, w_out, rel_bias):
    batch, seq, d = x.shape
    depth = norm_gain.shape[0]
    assert w_in.shape[-1] == IN_COLS and seq % 1024 == 0 and d == 1024
    assert (B_NEAR - 1) * ATT_T - (ATT_T - 1) < T5_FAR <= B_NEAR * ATT_T - (ATT_T - 1)

    w_perm = _gather_cols(w_in, _COLS).astype(BF16)
    w_vt = jnp.swapaxes(_gather_cols(w_in, _VT_COLS), 1, 2).astype(BF16)
    w_st = jnp.swapaxes(_gather_cols(w_in, _ST_COLS), 1, 2).astype(BF16)
    wb = w_branch.astype(BF16)
    wo = w_out.astype(BF16)

    gmat = jnp.asarray(np.kron(np.eye(N_HEADS), np.ones((HEAD_DIM, HEAD_DIM))), BF16)
    prep_t = 512
    lmat = jnp.asarray(np.tril(np.ones((prep_t, prep_t))), BF16)
    lmat_sel = lmat[:SEL_MC, :SEL_MC]
    tbl_a = _window_table(rel_bias[:, :N_HEADS])
    tbl_b = _near_table(rel_bias[:, N_HEADS:])

    x2d = x.reshape(batch * seq, d)
    for layer in range(depth):
        g = qk_gain[layer]
        qscale = ATTN_SCALE * LOG2E
        gains = jnp.zeros((8, 512), F32)
        gains = gains.at[0].set(jnp.tile(g[0, 0], N_HEADS) * qscale)
        gains = gains.at[1, :256].set(jnp.tile(g[0, 1], 4))
        gains = gains.at[2].set(jnp.tile(g[1, 0], N_HEADS) * qscale)
        gains = gains.at[3, :256].set(jnp.tile(g[1, 1], 4))
        gains = gains.at[4].set(jnp.tile(g[2, 0], N_HEADS) * qscale)
        gains = gains.at[5].set(jnp.tile(g[2, 1], N_HEADS))
        gains = gains.at[6, MISC_CF:MISC_CF + N_HEADS].set(b_forget[layer])

        z, misc, vt, st = _proj(x2d, norm_gain[layer][None, :], w_perm[layer], w_vt[layer], w_st[layer])
        qa, ka, qb, kb, qc, kc, cum = _prep(z, misc, gains, gmat, lmat, batch, seq)
        cum_t = cum.reshape(batch, seq, LANES)[:, :, MISC_CF:MISC_CF + N_HEADS].transpose(0, 2, 1)
        oa = _mix_a(qa, ka, z, vt, tbl_a, sinks[layer], batch, seq)
        maskadd = _sel_b(z, st, lmat_sel, batch, seq)
        ob = _mix_b(qb, kb, z, vt, maskadd, tbl_b, batch, seq)
        oc = _mix_c(qc, kc, z, vt, cum, cum_t, batch, seq)
        x2d = _merge(oa, ob, oc, z, x2d, wb[layer], wo[layer])
    return x2d.reshape(batch, seq, d)
```

```python
import functools
import math

import numpy as np
import jax
import jax.numpy as jnp
from jax import lax
from jax.experimental import pallas as pl
from jax.experimental.pallas import tpu as pltpu

F32 = jnp.float32
BF16 = jnp.bfloat16

HEAD_DIM = 64
LANES = 128
N_HEADS = 8
N_PAIRS = N_HEADS // 2
KV_GROUP = 4
WINDOW = 128
IDX_HEADS = 4
IDX_DIM = 32
TOPK_MAX = 256
N_BUCKETS = 32
MAX_DISTANCE = 512
EPS = 1e-6
ATTN_SCALE = HEAD_DIM ** -0.5
LOG2E = math.log2(math.e)
NEG = -1e30
T5_FAR = 413

_SIZES = dict(a_q=512, a_k=128, a_v=128, a_gate=512,
              b_q=512, b_k=128, b_v=128, b_iq=128, b_ik=32, b_iw=4, b_gate=512,
              c_q=512, c_k=512, c_v=512, c_f=8, c_gate=512, merge=3072)
_ORIG = {}
_o = 0
for _k, _v in _SIZES.items():
    _ORIG[_k] = _o
    _o += _v
IN_COLS = _o

OFF_MERGE = 0
OFF_AQ, OFF_AG, OFF_BQ, OFF_BG = 3072, 3584, 4096, 4608
OFF_CQ, OFF_CK, OFF_CG = 5120, 5632, 6144
OFF_AK, OFF_BK = 6656, 6912
OFF_BIQ, OFF_BIK, OFF_MISC = 7168, 7296, 7424
NP_COLS = 7680
PROJ_TN = 1536
MISC_CF = 8
VT_C, VT_A, VT_B, VT_ROWS = 0, 512, 640, 768


def _orig(name, idx=None):
    src = np.arange(_SIZES[name]) + _ORIG[name]
    return src if idx is None else src[idx]


def _column_map():
    cols = np.full((NP_COLS,), -1, np.int64)

    def put(off, src):
        cols[off:off + len(src)] = src

    for off, name in ((OFF_MERGE, "merge"), (OFF_AQ, "a_q"), (OFF_AG, "a_gate"), (OFF_BQ, "b_q"),
                      (OFF_BG, "b_gate"), (OFF_CQ, "c_q"), (OFF_CK, "c_k"), (OFF_CG, "c_gate"),
                      (OFF_BIQ, "b_iq")):
        put(off, _orig(name))
    dup = np.concatenate([np.arange(64), np.arange(64), np.arange(64, 128), np.arange(64, 128)])
    put(OFF_AK, _orig("a_k", dup))
    put(OFF_BK, _orig("b_k", dup))
    put(OFF_BIK, _orig("b_ik", np.tile(np.arange(IDX_DIM), IDX_HEADS)))
    put(OFF_MISC + MISC_CF, _orig("c_f"))
    return cols


def _gather_cols(w, cols):
    parts = []
    start = 0
    while start < len(cols):
        end = start + 1
        if cols[start] < 0:
            while end < len(cols) and cols[end] < 0:
                end += 1
            parts.append(jnp.zeros(w.shape[:-1] + (end - start,), w.dtype))
        else:
            while end < len(cols) and cols[end] == cols[end - 1] + 1:
                end += 1
            parts.append(w[..., int(cols[start]):int(cols[end - 1]) + 1])
        start = end
    return jnp.concatenate(parts, axis=-1)


_COLS = _column_map()
_VT_COLS = np.concatenate([_orig("c_v"), _orig("a_v"), _orig("b_v")])
_ST_COLS = np.concatenate([_orig("b_iw"), np.full((4,), -1, np.int64)])


def _params(sem, vmem_mb):
    return pltpu.CompilerParams(dimension_semantics=sem, vmem_limit_bytes=vmem_mb * 1024 * 1024)


def _dot(a, b):
    return jnp.dot(a, b, preferred_element_type=F32)


def _dot_nt(a, b):
    return lax.dot_general(a, b, (((1,), (1,)), ((), ())), preferred_element_type=F32)


def _t5_bucket(delta):
    n = jnp.maximum(delta, 0)
    max_exact = N_BUCKETS // 2
    nf = jnp.maximum(n, 1).astype(F32)
    large = max_exact + (jnp.log(nf / max_exact) / math.log(MAX_DISTANCE / max_exact)
                         * (N_BUCKETS - max_exact)).astype(jnp.int32)
    large = jnp.minimum(large, N_BUCKETS - 1)
    return jnp.where(n < max_exact, n, large)


def _proj_kernel(x_ref, g_ref, w_ref, wvt_ref, wst_ref, z_ref, misc_ref, vt_ref, st_ref, h_ref,
                 *, misc_tile, misc_local):
    j = pl.program_id(1)

    @pl.when(j == 0)
    def _():
        x = x_ref[...]
        ms = jnp.mean(x * x, axis=-1, keepdims=True)
        h = (x * lax.rsqrt(ms + EPS) * g_ref[...]).astype(BF16)
        h_ref[...] = h
        vt_ref[...] = _dot_nt(wvt_ref[...], h).astype(BF16)
        st_ref[...] = _dot_nt(wst_ref[...], h)

    acc = _dot(h_ref[...], w_ref[...])
    z_ref[...] = acc.astype(BF16)

    @pl.when(j == misc_tile)
    def _():
        misc_ref[...] = acc[:, misc_local:misc_local + LANES]


def _proj(x2d, gain, w, wvt, wst):
    n, d = x2d.shape
    tm = 1024
    return pl.pallas_call(
        functools.partial(_proj_kernel, misc_tile=OFF_MISC // PROJ_TN, misc_local=OFF_MISC % PROJ_TN),
        grid=(n // tm, NP_COLS // PROJ_TN),
        in_specs=[pl.BlockSpec((tm, d), lambda i, j: (i, 0)),
                  pl.BlockSpec((1, d), lambda i, j: (0, 0)),
                  pl.BlockSpec((d, PROJ_TN), lambda i, j: (0, j)),
                  pl.BlockSpec((VT_ROWS, d), lambda i, j: (0, 0)),
                  pl.BlockSpec((8, d), lambda i, j: (0, 0))],
        out_specs=[pl.BlockSpec((tm, PROJ_TN), lambda i, j: (i, j)),
                   pl.BlockSpec((tm, LANES), lambda i, j: (i, 0)),
                   pl.BlockSpec((VT_ROWS, tm), lambda i, j: (0, i)),
                   pl.BlockSpec((8, tm), lambda i, j: (0, i))],
        out_shape=[jax.ShapeDtypeStruct((n, NP_COLS), BF16),
                   jax.ShapeDtypeStruct((n, LANES), F32),
                   jax.ShapeDtypeStruct((VT_ROWS, n), BF16),
                   jax.ShapeDtypeStruct((8, n), F32)],
        scratch_shapes=[pltpu.VMEM((tm, d), BF16)],
        compiler_params=_params(("arbitrary", "arbitrary"), 56),
        name="proj",
    )(x2d, gain, w, wvt, wst)


def _group_rms(x_bf16, g_ref, gain_row):
    x = x_bf16.astype(F32)
    width = x.shape[-1]
    ss = _dot((x * x).astype(BF16), g_ref[:width, :width])
    return (x * lax.rsqrt(ss * (1.0 / HEAD_DIM) + EPS) * gain_row).astype(BF16)


def _prep_kernel(aq, ak, bq, bk, cq, ck, misc, gains, gmat, lmat,
                 qa_o, ka_o, qb_o, kb_o, qc_o, kc_o, cum_o, carry):
    t = pl.program_id(1)
    qa_o[...] = _group_rms(aq[...], gmat, gains[0:1, :])
    ka_o[...] = _group_rms(ak[...], gmat, gains[1:2, :256])
    qb_o[...] = _group_rms(bq[...], gmat, gains[2:3, :])
    kb_o[...] = _group_rms(bk[...], gmat, gains[3:4, :256])
    qc_o[...] = _group_rms(cq[...], gmat, gains[4:5, :])
    kc_o[...] = _group_rms(ck[...], gmat, gains[5:6, :])

    @pl.when(t == 0)
    def _():
        carry[...] = jnp.zeros_like(carry)

    xm = misc[...] + gains[6:7, :LANES]
    logf = jnp.minimum(xm, 0.0) - jnp.log(1.0 + jnp.exp(-jnp.abs(xm)))
    hi = logf.astype(BF16)
    r1 = logf - hi.astype(F32)
    mid = r1.astype(BF16)
    lo = (r1 - mid.astype(F32)).astype(BF16)
    lm = lmat[...]
    c = _dot(lm, hi) + _dot(lm, mid) + _dot(lm, lo) + carry[...]
    cum_o[...] = c * LOG2E
    rows = c.shape[0]
    carry[...] = c[rows - 1:rows, :]


def _prep(z, misc, gains, gmat, lmat, batch, seq):
    n = z.shape[0]
    tp = lmat.shape[0]
    nt = seq // tp

    def zspec(width, off):
        return pl.BlockSpec((tp, width), lambda b, t, _c=off // width: (b * nt + t, _c))

    def ospec(width):
        return pl.BlockSpec((tp, width), lambda b, t: (b * nt + t, 0))

    return pl.pallas_call(
        _prep_kernel,
        grid=(batch, nt),
        in_specs=[zspec(512, OFF_AQ), zspec(256, OFF_AK), zspec(512, OFF_BQ), zspec(256, OFF_BK),
                  zspec(512, OFF_CQ), zspec(512, OFF_CK), ospec(LANES),
                  pl.BlockSpec((8, 512), lambda b, t: (0, 0)),
                  pl.BlockSpec((512, 512), lambda b, t: (0, 0)),
                  pl.BlockSpec((tp, tp), lambda b, t: (0, 0))],
        out_specs=[ospec(512), ospec(256), ospec(512), ospec(256), ospec(512), ospec(512), ospec(LANES)],
        out_shape=[jax.ShapeDtypeStruct((n, 512), BF16), jax.ShapeDtypeStruct((n, 256), BF16),
                   jax.ShapeDtypeStruct((n, 512), BF16), jax.ShapeDtypeStruct((n, 256), BF16),
                   jax.ShapeDtypeStruct((n, 512), BF16), jax.ShapeDtypeStruct((n, 512), BF16),
                   jax.ShapeDtypeStruct((n, LANES), F32)],
        scratch_shapes=[pltpu.VMEM((1, LANES), F32)],
        compiler_params=_params(("arbitrary", "arbitrary"), 48),
        name="prep",
    )(z, z, z, z, z, z, misc, gains, gmat, lmat)


def _half_mask(shape, half):
    lane = lax.broadcasted_iota(jnp.int32, shape, len(shape) - 1)
    return (lane < HEAD_DIM) if half == 0 else (lane >= HEAD_DIM)


def _head_q(q_ref, h):
    p = h // 2
    qp = q_ref[:, p * LANES:(p + 1) * LANES]
    return jnp.where(_half_mask(qp.shape, h % 2), qp, jnp.zeros_like(qp))


DEN_ROWS = 16
ACC_ROWS = HEAD_DIM + DEN_ROWS


def _with_ones(vt):
    return jnp.concatenate([vt, jnp.ones((DEN_ROWS, vt.shape[1]), vt.dtype)], axis=0)


def _softmax_step(h, s, vt, m_ref, acc_ref, qbias=None):
    m_prev = m_ref[h]
    s_max = jnp.max(s, axis=0, keepdims=True)
    m_new = jnp.maximum(m_prev, s_max if qbias is None else s_max + qbias)
    alpha = jnp.exp2(m_prev - m_new)
    shift = m_new[0:1, :] if qbias is None else m_new[0:1, :] - qbias
    p = jnp.exp2(s - shift)
    acc_ref[h] = alpha[0:1, :] * acc_ref[h] + _dot(_with_ones(vt), p.astype(BF16))
    m_ref[h] = m_new


HEADS_AHEAD = 4


def _heads_pipelined(logits, values, m_ref, acc_ref, qbias=None):
    ahead = [logits(h) for h in range(HEADS_AHEAD)]
    for h in range(N_HEADS):
        if h + HEADS_AHEAD < N_HEADS:
            ahead.append(logits(h + HEADS_AHEAD))
        _softmax_step(h, ahead.pop(0), values(h), m_ref, acc_ref,
                      None if qbias is None else qbias(h))


def _init_state(m_ref, acc_ref):
    m_ref[...] = jnp.full(m_ref.shape, NEG, F32)
    acc_ref[...] = jnp.zeros(acc_ref.shape, F32)


def _gated_pair(p, gate, acc_ref):
    halves = []
    for h in (2 * p, 2 * p + 1):
        a = acc_ref[h]
        halves.append(a[:HEAD_DIM] / a[HEAD_DIM:HEAD_DIM + 1])
    ot = jnp.concatenate(halves, axis=0)
    g = gate.astype(F32)
    return (ot.T * (g * jax.nn.sigmoid(g))).astype(BF16)


def _mix_a_kernel(sink_ref, q_ref, kp_ref, kc_ref, vp_ref, vc_ref, gate_ref, tbl_ref, o_ref, acc_ref):
    n = pl.program_id(1)
    pad_pen = jnp.where(n > 0, 0.0, NEG).astype(F32)
    den_rows = lax.broadcasted_iota(jnp.int32, (ACC_ROWS, WINDOW), 0) >= HEAD_DIM

    def logits(h):
        q = _head_q(q_ref, h)
        ksl = slice((h // KV_GROUP) * LANES, (h // KV_GROUP + 1) * LANES)
        return (_dot_nt(kp_ref[:, ksl], q) + tbl_ref[h, :WINDOW, :] + pad_pen,
                _dot_nt(kc_ref[:, ksl], q) + tbl_ref[h, WINDOW:, :])

    all_logits = [logits(h) for h in range(N_HEADS)]
    for h in range(N_HEADS):
        kv = h // KV_GROUP
        vsl = slice(kv * HEAD_DIM, (kv + 1) * HEAD_DIM)
        s_prev, s_cur = all_logits[h]
        sink = sink_ref[h] * LOG2E
        m = jnp.maximum(jnp.maximum(jnp.max(s_prev, axis=0, keepdims=True),
                                    jnp.max(s_cur, axis=0, keepdims=True)), sink)
        p_prev = jnp.exp2(s_prev - m)
        p_cur = jnp.exp2(s_cur - m)
        acc = (_dot(_with_ones(vp_ref[vsl, :]), p_prev.astype(BF16))
               + _dot(_with_ones(vc_ref[vsl, :]), p_cur.astype(BF16)))
        acc_ref[h] = acc + jnp.where(den_rows, jnp.exp2(sink - m), 0.0)
    for p in range(N_PAIRS):
        sl = slice(p * LANES, (p + 1) * LANES)
        o_ref[:, sl] = _gated_pair(p, gate_ref[:, sl], acc_ref)


def _mix_a(qa, ka, z, vt, tbl, sinks, batch, seq):
    n = qa.shape[0]
    t = WINDOW
    nb = seq // t
    return pl.pallas_call(
        _mix_a_kernel,
        grid=(batch, nb),
        in_specs=[pl.BlockSpec(memory_space=pltpu.SMEM),
                  pl.BlockSpec((t, 512), lambda b, i: (b * nb + i, 0)),
                  pl.BlockSpec((t, 256), lambda b, i: (b * nb + jnp.maximum(i - 1, 0), 0)),
                  pl.BlockSpec((t, 256), lambda b, i: (b * nb + i, 0)),
                  pl.BlockSpec((LANES, t), lambda b, i: (VT_A // LANES, b * nb + jnp.maximum(i - 1, 0))),
                  pl.BlockSpec((LANES, t), lambda b, i: (VT_A // LANES, b * nb + i)),
                  pl.BlockSpec((t, 512), lambda b, i: (b * nb + i, OFF_AG // 512)),
                  pl.BlockSpec((N_HEADS, 2 * t, t), lambda b, i: (0, 0, 0))],
        out_specs=pl.BlockSpec((t, 512), lambda b, i: (b * nb + i, 0)),
        out_shape=jax.ShapeDtypeStruct((n, 512), BF16),
        scratch_shapes=[pltpu.VMEM((N_HEADS, ACC_ROWS, t), F32)],
        compiler_params=_params(("arbitrary", "arbitrary"), 32),
        name="mix_a",
    )(sinks, qa, ka, ka, vt, vt, z, tbl)


SEL_TQ = 256
SEL_SC = 512
SEL_SUB = 64
SEL_MC = 256
SEL_FAST_ITERS = 24
SEL_MAX_ITERS = 600


def _sel_b_kernel(iq_ref, ik_ref, wt_ref, lmat_ref, o_ref, sc_ref, *, topk):
    n = pl.program_id(1)
    t0 = n * SEL_TQ
    n_full = n // (SEL_SC // SEL_TQ)
    n_sc = n_full + 1
    n_mc = n + 1
    kf = float(topk)

    iq = iq_ref[...]
    lane = lax.broadcasted_iota(jnp.int32, iq.shape, 1)
    qs = jnp.concatenate(
        [jnp.where((lane >= h * IDX_DIM) & (lane < (h + 1) * IDX_DIM), iq, jnp.zeros_like(iq))
         for h in range(IDX_HEADS)], axis=0)
    wscale = (IDX_HEADS ** -0.5) * (IDX_DIM ** -0.5)
    w = [wt_ref[h:h + 1, :] * wscale for h in range(IDX_HEADS)]
    qpos = t0 + lax.broadcasted_iota(jnp.int32, (SEL_SUB, SEL_TQ), 1)
    kiota = lax.broadcasted_iota(jnp.int32, (SEL_SUB, SEL_TQ), 0)

    def score_chunk(c, carry, diagonal):
        rmax, rmin, ge0, gt0, minpos = carry
        base = pl.multiple_of(c * SEL_SC, SEL_SC)
        raw = _dot_nt(ik_ref[pl.ds(base, SEL_SC), :], qs)
        for u in range(SEL_SC // SEL_SUB):
            rows = slice(u * SEL_SUB, (u + 1) * SEL_SUB)
            score = w[0] * jnp.maximum(raw[rows, 0:SEL_TQ], 0.0)
            for h in range(1, IDX_HEADS):
                score = score + w[h] * jnp.maximum(raw[rows, h * SEL_TQ:(h + 1) * SEL_TQ], 0.0)
            rmin = jnp.minimum(rmin, score)
            if diagonal:
                score = jnp.where((kiota + (base + u * SEL_SUB)) <= qpos, score, -jnp.inf)
            sc_ref[pl.ds(base + u * SEL_SUB, SEL_SUB), :] = score
            rmax = jnp.maximum(rmax, score)
            pos = score > 0.0
            ge0 = ge0 + jnp.where(score >= 0.0, 1.0, 0.0)
            gt0 = gt0 + jnp.where(pos, 1.0, 0.0)
            minpos = jnp.minimum(minpos, jnp.where(pos, score, jnp.inf))
        return rmax, rmin, ge0, gt0, minpos

    def slab(v):
        return jnp.full((SEL_SUB, SEL_TQ), v, F32)

    carry = lax.fori_loop(0, n_full, lambda c, cr: score_chunk(c, cr, False),
                          (slab(-jnp.inf), slab(jnp.inf), slab(0.0), slab(0.0), slab(jnp.inf)))
    rmax, rmin, ge0, gt0, minpos = score_chunk(n_full, carry, True)
    rmax = jnp.max(rmax, axis=0, keepdims=True)
    rmin = jnp.min(rmin, axis=0, keepdims=True)
    ge0 = jnp.sum(ge0, axis=0, keepdims=True)
    gt0 = jnp.sum(gt0, axis=0, keepdims=True)
    minpos = jnp.min(minpos, axis=0, keepdims=True)

    navail = (t0 + 1 + lax.broadcasted_iota(jnp.int32, (1, SEL_TQ), 1)).astype(F32)

    def sweep(mid, snap):
        midb = jnp.broadcast_to(mid, (SEL_SUB, SEL_TQ))

        def body(c, carry):
            base = pl.multiple_of(c * SEL_SC, SEL_SC)
            cnt, up, dn = carry
            for u in range(SEL_SC // SEL_SUB):
                x = sc_ref[pl.ds(base + u * SEL_SUB, SEL_SUB), :]
                ge = x >= midb
                cnt = cnt + jnp.where(ge, 1.0, 0.0)
                if snap:
                    up = jnp.minimum(up, jnp.where(ge, x, jnp.inf))
                    dn = jnp.maximum(dn, jnp.where(ge, -jnp.inf, x))
            return cnt, up, dn

        init = (jnp.zeros((SEL_SUB, SEL_TQ), F32), jnp.full((SEL_SUB, SEL_TQ), jnp.inf, F32),
                jnp.full((SEL_SUB, SEL_TQ), -jnp.inf, F32))
        cnt, up, dn = lax.fori_loop(0, n_sc, body, init)
        return (jnp.sum(cnt, axis=0, keepdims=True), jnp.min(up, axis=0, keepdims=True),
                jnp.max(dn, axis=0, keepdims=True))

    def advance(st, mid, snap):
        it, lo, hi, hi_dn, c_lo, c_hi, done = st
        cnt, up, dn = sweep(mid, snap)
        live = jnp.where(done > 0.5, 0.0, jnp.where(mid <= lo, 0.0, jnp.where(mid >= hi, 0.0, 1.0)))
        ge = jnp.where(cnt >= kf, live, 0.0) > 0.5
        lt = jnp.where(cnt >= kf, 0.0, live) > 0.5
        lo = jnp.where(ge, up if snap else mid, lo)
        c_lo = jnp.where(ge, cnt, c_lo)
        hi = jnp.where(lt, mid, hi)
        hi_dn = jnp.where(lt, dn if snap else jnp.inf, hi_dn)
        c_hi = jnp.where(lt, cnt, c_hi)
        done = jnp.where(live < 0.5, 1.0, jnp.where(cnt == kf, 1.0, jnp.where(hi_dn <= lo, 1.0, 0.0)))
        return it + 1, lo, hi, hi_dn, c_lo, c_hi, done

    def make_step(snap, repeat=1):
        def step(st):
            for _ in range(repeat):
                st = advance(st, 0.5 * st[1] + 0.5 * st[2], snap)
            return st
        return step

    def make_cond(limit):
        def cond(st):
            return jnp.logical_and(st[0] < limit, jnp.min(st[6]) < 0.5)
        return cond

    inf_row = jnp.full((1, SEL_TQ), jnp.inf, F32)
    all_in = navail <= kf
    above = jnp.logical_and(jnp.logical_not(all_in), gt0 >= kf)
    at_zero = jnp.logical_and(jnp.logical_not(all_in), jnp.logical_and(gt0 < kf, ge0 >= kf))
    below = jnp.logical_and(jnp.logical_not(all_in), ge0 < kf)
    lo = jnp.where(above, minpos, jnp.where(at_zero, 0.0, rmin))
    c_lo = jnp.where(above, gt0, jnp.where(at_zero, ge0, navail))
    hi = jnp.where(at_zero, minpos, jnp.where(below, 0.0, inf_row))
    c_hi = jnp.where(at_zero, gt0, jnp.where(below, ge0, 0.0))
    hi_dn = jnp.where(above, rmax, inf_row)
    done = jnp.where(all_in, 1.0, jnp.where(at_zero, 1.0, 0.0))
    st = (jnp.int32(0), lo, hi, hi_dn, c_lo, c_hi, done)
    st = advance(st, jnp.where(above, rmax, 0.5 * lo + 0.5 * hi), False)
    st = lax.while_loop(make_cond(SEL_FAST_ITERS), make_step(False, repeat=2), st)
    st = lax.while_loop(make_cond(SEL_MAX_ITERS), make_step(True), st)
    _, lo, hi, _, c_lo, c_hi, _ = st
    need = kf - c_hi
    ties = jnp.max(jnp.where(navail > kf, c_lo, kf)) > kf

    def fill_body(c, carry):
        base = pl.multiple_of(c * SEL_MC, SEL_MC)
        o_ref[0, pl.ds(base, SEL_MC), :] = jnp.full((SEL_MC, SEL_TQ), NEG, o_ref.dtype)
        return carry
    lax.fori_loop(n_mc, o_ref.shape[1] // SEL_MC, fill_body, 0)

    @pl.when(jnp.logical_not(ties))
    def _():
        def mask_body(c, carry):
            base = pl.multiple_of(c * SEL_MC, SEL_MC)
            x = sc_ref[pl.ds(base, SEL_MC), :]
            o_ref[0, pl.ds(base, SEL_MC), :] = jnp.where(x >= lo, 0.0, NEG).astype(o_ref.dtype)
            return carry
        lax.fori_loop(0, n_mc, mask_body, 0)

    @pl.when(ties)
    def _():
        lmat = lmat_ref[...]

        def mask_body(c, run):
            base = pl.multiple_of(c * SEL_MC, SEL_MC)
            x = sc_ref[pl.ds(base, SEL_MC), :]
            bnd = jnp.where(x >= lo, jnp.where(x < hi, 1.0, 0.0), 0.0)
            incl = _dot(lmat, bnd.astype(BF16)) + run
            keep = jnp.where(x >= hi, 1.0, jnp.where(incl - bnd < need, bnd, 0.0))
            o_ref[0, pl.ds(base, SEL_MC), :] = jnp.where(keep > 0.5, 0.0, NEG).astype(o_ref.dtype)
            return incl[SEL_MC - 1:SEL_MC, :]
        lax.fori_loop(0, n_mc, mask_body, jnp.zeros((1, SEL_TQ), F32))


def _sel_b(z, st, lmat, batch, seq):
    nb = seq // SEL_TQ
    topk = min(TOPK_MAX, seq // 4)
    return pl.pallas_call(
        functools.partial(_sel_b_kernel, topk=topk),
        grid=(batch, nb),
        in_specs=[pl.BlockSpec((SEL_TQ, LANES), lambda b, i: (b * nb + i, OFF_BIQ // LANES)),
                  pl.BlockSpec((seq, LANES), lambda b, i: (b, OFF_BIK // LANES)),
                  pl.BlockSpec((8, SEL_TQ), lambda b, i: (0, b * nb + i)),
                  pl.BlockSpec((SEL_MC, SEL_MC), lambda b, i: (0, 0))],
        out_specs=pl.BlockSpec((1, seq, SEL_TQ), lambda b, i: (b, 0, i)),
        out_shape=jax.ShapeDtypeStruct((batch, seq, seq), BF16),
        scratch_shapes=[pltpu.VMEM((seq, SEL_TQ), F32)],
        compiler_params=_params(("arbitrary", "arbitrary"), 48),
        name="sel_b",
    )(z, z, st, lmat)


ATT_T = 256
B_NEAR = 3
FAR_TILES = 2


def _far_remainder(n_tiles, tile):
    done = (n_tiles // FAR_TILES) * FAR_TILES
    rem = n_tiles - done
    size = FAR_TILES // 2
    while size >= 1:
        @pl.when((rem // size) % 2 == 1)
        def _(size=size, start=done + (rem // (2 * size)) * (2 * size)):
            tile(start, size * ATT_T)
        size //= 2


def _mix_b_kernel(q_ref, k_ref, vt_ref, mask_ref, gate_ref, tbl_ref, o_ref, m_ref, acc_ref):
    i = pl.program_id(1)
    _init_state(m_ref, acc_ref)
    qs = [_head_q(q_ref, h) for h in range(N_HEADS)]

    def tile(j, size, near):
        base = pl.multiple_of(j * ATT_T, ATT_T)
        madd = mask_ref[0, pl.ds(base, size), :].astype(F32)

        def logits(h):
            kv = h // KV_GROUP
            s = _dot_nt(k_ref[pl.ds(base, size), kv * LANES:(kv + 1) * LANES], qs[h]) + madd
            return s if near is None else s + tbl_ref[h, near]

        def values(h):
            kv = h // KV_GROUP
            return vt_ref[kv * HEAD_DIM:(kv + 1) * HEAD_DIM, pl.ds(base, size)]

        _heads_pipelined(logits, values, m_ref, acc_ref)

    def far_body(j, carry):
        tile(j * FAR_TILES, FAR_TILES * ATT_T, None)
        return carry

    n_far = jnp.maximum(i - (B_NEAR - 1), 0)
    lax.fori_loop(0, n_far // FAR_TILES, far_body, 0)
    _far_remainder(n_far, lambda j, size: tile(j, size, None))

    for d in range(B_NEAR - 1, -1, -1):
        @pl.when(i >= d)
        def _(d=d):
            tile(i - d, ATT_T, d)
    for p in range(N_PAIRS):
        sl = slice(p * LANES, (p + 1) * LANES)
        o_ref[:, sl] = _gated_pair(p, gate_ref[:, sl], acc_ref)


def _mix_b(qb, kb, z, vt, maskadd, tbl, batch, seq):
    n = qb.shape[0]
    t = ATT_T
    nq = seq // t
    return pl.pallas_call(
        _mix_b_kernel,
        grid=(batch, nq),
        in_specs=[pl.BlockSpec((t, 512), lambda b, i: (b * nq + i, 0)),
                  pl.BlockSpec((seq, 256), lambda b, i: (b, 0)),
                  pl.BlockSpec((LANES, seq), lambda b, i: (VT_B // LANES, b)),
                  pl.BlockSpec((1, seq, t), lambda b, i: (b, 0, i)),
                  pl.BlockSpec((t, 512), lambda b, i: (b * nq + i, OFF_BG // 512)),
                  pl.BlockSpec((N_HEADS, B_NEAR, t, t), lambda b, i: (0, 0, 0, 0))],
        out_specs=pl.BlockSpec((t, 512), lambda b, i: (b * nq + i, 0)),
        out_shape=jax.ShapeDtypeStruct((n, 512), BF16),
        scratch_shapes=[pltpu.VMEM((N_HEADS, 8, t), F32), pltpu.VMEM((N_HEADS, ACC_ROWS, t), F32)],
        compiler_params=_params(("arbitrary", "arbitrary"), 56),
        name="mix_b",
    )(qb, kb, vt, maskadd, z, tbl)


def _mix_c_kernel(q_ref, k_ref, vt_ref, ck_ref, cq_ref, gate_ref, o_ref, m_ref, acc_ref):
    i = pl.program_id(1)
    _init_state(m_ref, acc_ref)
    qs = [_head_q(q_ref, h) for h in range(N_HEADS)]
    q0 = pl.multiple_of(i * ATT_T, ATT_T)
    cq = cq_ref[0, :, pl.ds(q0, ATT_T)]
    krow = lax.broadcasted_iota(jnp.int32, (ATT_T, ATT_T), 0)
    qcol = lax.broadcasted_iota(jnp.int32, (ATT_T, ATT_T), 1)

    def tile(j, size, diag):
        base = pl.multiple_of(j * ATT_T, ATT_T)
        ck_all = ck_ref[pl.ds(base, size), :]

        def logits(h):
            p = h // 2
            ck = ck_all[:, MISC_CF + h:MISC_CF + h + 1]
            s = _dot_nt(k_ref[pl.ds(base, size), p * LANES:(p + 1) * LANES], qs[h]) - ck
            return jnp.where(krow <= qcol, s, NEG) if diag else s

        def values(h):
            return vt_ref[h * HEAD_DIM:(h + 1) * HEAD_DIM, pl.ds(base, size)]

        _heads_pipelined(logits, values, m_ref, acc_ref, qbias=lambda h: cq[h:h + 1, :])

    def body(j, carry):
        tile(j * FAR_TILES, FAR_TILES * ATT_T, False)
        return carry

    lax.fori_loop(0, i // FAR_TILES, body, 0)
    _far_remainder(i, lambda j, size: tile(j, size, False))
    tile(i, ATT_T, True)
    for p in range(N_PAIRS):
        sl = slice(p * LANES, (p + 1) * LANES)
        o_ref[:, sl] = _gated_pair(p, gate_ref[:, sl], acc_ref)


def _mix_c(qc, kc, z, vt, cum, cum_t, batch, seq):
    n = qc.shape[0]
    t = ATT_T
    nq = seq // t
    return pl.pallas_call(
        _mix_c_kernel,
        grid=(batch, nq),
        in_specs=[pl.BlockSpec((t, 512), lambda b, i: (b * nq + i, 0)),
                  pl.BlockSpec((seq, 512), lambda b, i: (b, 0)),
                  pl.BlockSpec((512, seq), lambda b, i: (VT_C // 512, b)),
                  pl.BlockSpec((seq, LANES), lambda b, i: (b, 0)),
                  pl.BlockSpec((1, N_HEADS, seq), lambda b, i: (b, 0, 0)),
                  pl.BlockSpec((t, 512), lambda b, i: (b * nq + i, OFF_CG // 512))],
        out_specs=pl.BlockSpec((t, 512), lambda b, i: (b * nq + i, 0)),
        out_shape=jax.ShapeDtypeStruct((n, 512), BF16),
        scratch_shapes=[pltpu.VMEM((N_HEADS, 8, t), F32), pltpu.VMEM((N_HEADS, ACC_ROWS, t), F32)],
        compiler_params=_params(("arbitrary", "arbitrary"), 56),
        name="mix_c",
    )(qc, kc, vt, cum, cum_t, z)


def _merge_kernel(oa, ob, oc, ga, gb, gc, x_ref, wb_ref, wo_ref, o_ref):
    merged = None
    for br, (o, g) in enumerate(((oa, ga), (ob, gb), (oc, gc))):
        y = _dot(o[...], wb_ref[br])
        term = jax.nn.sigmoid(g[...].astype(F32)) * y
        merged = term if merged is None else merged + term
    o_ref[...] = x_ref[...] + _dot(merged.astype(BF16), wo_ref[...])


def _merge(oa, ob, oc, z, x2d, wb, wo):
    n, d = x2d.shape
    tm = 512

    def ospec():
        return pl.BlockSpec((tm, 512), lambda i: (i, 0))

    def gspec(k):
        return pl.BlockSpec((tm, d), lambda i, _k=k: (i, OFF_MERGE // d + _k))

    return pl.pallas_call(
        _merge_kernel,
        grid=(n // tm,),
        in_specs=[ospec(), ospec(), ospec(), gspec(0), gspec(1), gspec(2),
                  pl.BlockSpec((tm, d), lambda i: (i, 0)),
                  pl.BlockSpec((3, 512, d), lambda i: (0, 0, 0)),
                  pl.BlockSpec((d, d), lambda i: (0, 0))],
        out_specs=pl.BlockSpec((tm, d), lambda i: (i, 0)),
        out_shape=jax.ShapeDtypeStruct((n, d), F32),
        compiler_params=_params(("arbitrary",), 48),
        name="merge",
    )(oa, ob, oc, z, z, z, x2d, wb, wo)


def _bucket_lookup(table, bucket):
    out = jnp.zeros(bucket.shape + (table.shape[1],), F32)
    for b in range(N_BUCKETS):
        out = jnp.where((bucket == b)[..., None], table[b].astype(F32), out)
    return out


def _window_table(bias_a):
    ki = jnp.arange(2 * WINDOW)[:, None]
    qi = jnp.arange(WINDOW)[None, :]
    delta = qi + WINDOW - ki
    band = (delta >= 0) & (delta < WINDOW)
    bias = _bucket_lookup(bias_a, _t5_bucket(delta)).transpose(2, 0, 1) * LOG2E
    return jnp.where(band[None], bias, NEG)


def _near_table(bias_b):
    ki = jnp.arange(ATT_T)[:, None]
    qi = jnp.arange(ATT_T)[None, :]
    d = jnp.arange(B_NEAR)[:, None, None]
    delta = d * ATT_T + qi[None] - ki[None]
    bias = _bucket_lookup(bias_b, _t5_bucket(delta))
    far = bias_b[N_BUCKETS - 1].astype(F32)
    return ((bias - far) * LOG2E).transpose(3, 0, 1, 2)


def kernel(x, norm_gain, w_in, b_forget, qk_gain, sinks, w_branch, w_out, rel_bias):
    batch, seq, d = x.shape
    depth = norm_gain.shape[0]
    assert w_in.shape[-1] == IN_COLS and seq % 1024 == 0 and d == 1024
    assert (B_NEAR - 1) * ATT_T - (ATT_T - 1) < T5_FAR <= B_NEAR * ATT_T - (ATT_T - 1)

    w_perm = _gather_cols(w_in, _COLS).astype(BF16)
    w_vt = jnp.swapaxes(_gather_cols(w_in, _VT_COLS), 1, 2).astype(BF16)
    w_st = jnp.swapaxes(_gather_cols(w_in, _ST_COLS), 1, 2).astype(BF16)
    wb = w_branch.astype(BF16)
    wo = w_out.astype(BF16)

    gmat = jnp.asarray(np.kron(np.eye(N_HEADS), np.ones((HEAD_DIM, HEAD_DIM))), BF16)
    prep_t = 512
    lmat = jnp.asarray(np.tril(np.ones((prep_t, prep_t))), BF16)
    lmat_sel = lmat[:SEL_MC, :SEL_MC]
    tbl_a = _window_table(rel_bias[:, :N_HEADS])
    tbl_b = _near_table(rel_bias[:, N_HEADS:])

    x2d = x.reshape(batch * seq, d)
    for layer in range(depth):
        g = qk_gain[layer]
        qscale = ATTN_SCALE * LOG2E
        gains = jnp.zeros((8, 512), F32)
        gains = gains.at[0].set(jnp.tile(g[0, 0], N_HEADS) * qscale)
        gains = gains.at[1, :256].set(jnp.tile(g[0, 1], 4))
        gains = gains.at[2].set(jnp.tile(g[1, 0], N_HEADS) * qscale)
        gains = gains.at[3, :256].set(jnp.tile(g[1, 1], 4))
        gains = gains.at[4].set(jnp.tile(g[2, 0], N_HEADS) * qscale)
        gains = gains.at[5].set(jnp.tile(g[2, 1], N_HEADS))
        gains = gains.at[6, MISC_CF:MISC_CF + N_HEADS].set(b_forget[layer])

        z, misc, vt, st = _proj(x2d, norm_gain[layer][None, :], w_perm[layer], w_vt[layer], w_st[layer])
        qa, ka, qb, kb, qc, kc, cum = _prep(z, misc, gains, gmat, lmat, batch, seq)
        cum_t = cum.reshape(batch, seq, LANES)[:, :, MISC_CF:MISC_CF + N_HEADS].transpose(0, 2, 1)
        oa = _mix_a(qa, ka, z, vt, tbl_a, sinks[layer], batch, seq)
        maskadd = _sel_b(z, st, lmat_sel, batch, seq)
        ob = _mix_b(qb, kb, z, vt, maskadd, tbl_b, batch, seq)
        oc = _mix_c(qc, kc, z, vt, cum, cum_t, batch, seq)
        x2d = _merge(oa, ob, oc, z, x2d, wb[layer], wo[layer])
    return x2d.reshape(batch, seq, d)
```

```python
import functools
import math

import numpy as np
import jax
import jax.numpy as jnp
from jax import lax
from jax.experimental import pallas as pl
from jax.experimental.pallas import tpu as pltpu

F32 = jnp.float32
BF16 = jnp.bfloat16

HEAD_DIM = 64
LANES = 128
N_HEADS = 8
N_PAIRS = N_HEADS // 2
KV_GROUP = 4
WINDOW = 128
IDX_HEADS = 4
IDX_DIM = 32
TOPK_MAX = 256
N_BUCKETS = 32
MAX_DISTANCE = 512
EPS = 1e-6
ATTN_SCALE = HEAD_DIM ** -0.5
LOG2E = math.log2(math.e)
NEG = -1e30
T5_FAR = 413

_SIZES = dict(a_q=512, a_k=128, a_v=128, a_gate=512,
              b_q=512, b_k=128, b_v=128, b_iq=128, b_ik=32, b_iw=4, b_gate=512,
              c_q=512, c_k=512, c_v=512, c_f=8, c_gate=512, merge=3072)
_ORIG = {}
_o = 0
for _k, _v in _SIZES.items():
    _ORIG[_k] = _o
    _o += _v
IN_COLS = _o

OFF_MERGE = 0
OFF_AQ, OFF_AG, OFF_BQ, OFF_BG = 3072, 3584, 4096, 4608
OFF_CQ, OFF_CK, OFF_CG = 5120, 5632, 6144
OFF_AK, OFF_BK = 6656, 6912
OFF_BIQ, OFF_BIK, OFF_MISC = 7168, 7296, 7424
NP_COLS = 7680
PROJ_TN = 1536
MISC_CF = 8
VT_C, VT_A, VT_B, VT_ROWS = 0, 512, 640, 768


def _orig(name, idx=None):
    src = np.arange(_SIZES[name]) + _ORIG[name]
    return src if idx is None else src[idx]


def _column_map():
    cols = np.full((NP_COLS,), -1, np.int64)

    def put(off, src):
        cols[off:off + len(src)] = src

    for off, name in ((OFF_MERGE, "merge"), (OFF_AQ, "a_q"), (OFF_AG, "a_gate"), (OFF_BQ, "b_q"),
                      (OFF_BG, "b_gate"), (OFF_CQ, "c_q"), (OFF_CK, "c_k"), (OFF_CG, "c_gate"),
                      (OFF_BIQ, "b_iq")):
        put(off, _orig(name))
    dup = np.concatenate([np.arange(64), np.arange(64), np.arange(64, 128), np.arange(64, 128)])
    put(OFF_AK, _orig("a_k", dup))
    put(OFF_BK, _orig("b_k", dup))
    put(OFF_BIK, _orig("b_ik", np.tile(np.arange(IDX_DIM), IDX_HEADS)))
    put(OFF_MISC + MISC_CF, _orig("c_f"))
    return cols


def _gather_cols(w, cols):
    parts = []
    start = 0
    while start < len(cols):
        end = start + 1
        if cols[start] < 0:
            while end < len(cols) and cols[end] < 0:
                end += 1
            parts.append(jnp.zeros(w.shape[:-1] + (end - start,), w.dtype))
        else:
            while end < len(cols) and cols[end] == cols[end - 1] + 1:
                end += 1
            parts.append(w[..., int(cols[start]):int(cols[end - 1]) + 1])
        start = end
    return jnp.concatenate(parts, axis=-1)


_COLS = _column_map()
_VT_COLS = np.concatenate([_orig("c_v"), _orig("a_v"), _orig("b_v")])
_ST_COLS = np.concatenate([_orig("b_iw"), np.full((4,), -1, np.int64)])


def _params(sem, vmem_mb):
    return pltpu.CompilerParams(dimension_semantics=sem, vmem_limit_bytes=vmem_mb * 1024 * 1024)


def _dot(a, b):
    return jnp.dot(a, b, preferred_element_type=F32)


def _dot_nt(a, b):
    return lax.dot_general(a, b, (((1,), (1,)), ((), ())), preferred_element_type=F32)


def _t5_bucket(delta):
    n = jnp.maximum(delta, 0)
    max_exact = N_BUCKETS // 2
    nf = jnp.maximum(n, 1).astype(F32)
    large = max_exact + (jnp.log(nf / max_exact) / math.log(MAX_DISTANCE / max_exact)
                         * (N_BUCKETS - max_exact)).astype(jnp.int32)
    large = jnp.minimum(large, N_BUCKETS - 1)
    return jnp.where(n < max_exact, n, large)


def _proj_kernel(x_ref, g_ref, w_ref, wvt_ref, wst_ref, z_ref, misc_ref, vt_ref, st_ref, h_ref,
                 *, misc_tile, misc_local):
    j = pl.program_id(1)

    @pl.when(j == 0)
    def _():
        x = x_ref[...]
        ms = jnp.mean(x * x, axis=-1, keepdims=True)
        h = (x * lax.rsqrt(ms + EPS) * g_ref[...]).astype(BF16)
        h_ref[...] = h
        vt_ref[...] = _dot_nt(wvt_ref[...], h).astype(BF16)
        st_ref[...] = _dot_nt(wst_ref[...], h)

    acc = _dot(h_ref[...], w_ref[...])
    z_ref[...] = acc.astype(BF16)

    @pl.when(j == misc_tile)
    def _():
        misc_ref[...] = acc[:, misc_local:misc_local + LANES]


def _proj(x2d, gain, w, wvt, wst):
    n, d = x2d.shape
    tm = 1024
    return pl.pallas_call(
        functools.partial(_proj_kernel, misc_tile=OFF_MISC // PROJ_TN, misc_local=OFF_MISC % PROJ_TN),
        grid=(n // tm, NP_COLS // PROJ_TN),
        in_specs=[pl.BlockSpec((tm, d), lambda i, j: (i, 0)),
                  pl.BlockSpec((1, d), lambda i, j: (0, 0)),
                  pl.BlockSpec((d, PROJ_TN), lambda i, j: (0, j)),
                  pl.BlockSpec((VT_ROWS, d), lambda i, j: (0, 0)),
                  pl.BlockSpec((8, d), lambda i, j: (0, 0))],
        out_specs=[pl.BlockSpec((tm, PROJ_TN), lambda i, j: (i, j)),
                   pl.BlockSpec((tm, LANES), lambda i, j: (i, 0)),
                   pl.BlockSpec((VT_ROWS, tm), lambda i, j: (0, i)),
                   pl.BlockSpec((8, tm), lambda i, j: (0, i))],
        out_shape=[jax.ShapeDtypeStruct((n, NP_COLS), BF16),
                   jax.ShapeDtypeStruct((n, LANES), F32),
                   jax.ShapeDtypeStruct((VT_ROWS, n), BF16),
                   jax.ShapeDtypeStruct((8, n), F32)],
        scratch_shapes=[pltpu.VMEM((tm, d), BF16)],
        compiler_params=_params(("arbitrary", "arbitrary"), 56),
        name="proj",
    )(x2d, gain, w, wvt, wst)


def _group_rms(x_bf16, g_ref, gain_row):
    x = x_bf16.astype(F32)
    width = x.shape[-1]
    ss = _dot((x * x).astype(BF16), g_ref[:width, :width])
    return (x * lax.rsqrt(ss * (1.0 / HEAD_DIM) + EPS) * gain_row).astype(BF16)


def _prep_kernel(aq, ak, bq, bk, cq, ck, misc, gains, gmat, lmat,
                 qa_o, ka_o, qb_o, kb_o, qc_o, kc_o, cum_o, carry):
    t = pl.program_id(1)
    qa_o[...] = _group_rms(aq[...], gmat, gains[0:1, :])
    ka_o[...] = _group_rms(ak[...], gmat, gains[1:2, :256])
    qb_o[...] = _group_rms(bq[...], gmat, gains[2:3, :])
    kb_o[...] = _group_rms(bk[...], gmat, gains[3:4, :256])
    qc_o[...] = _group_rms(cq[...], gmat, gains[4:5, :])
    kc_o[...] = _group_rms(ck[...], gmat, gains[5:6, :])

    @pl.when(t == 0)
    def _():
        carry[...] = jnp.zeros_like(carry)

    xm = misc[...] + gains[6:7, :LANES]
    logf = jnp.minimum(xm, 0.0) - jnp.log(1.0 + jnp.exp(-jnp.abs(xm)))
    hi = logf.astype(BF16)
    r1 = logf - hi.astype(F32)
    mid = r1.astype(BF16)
    lo = (r1 - mid.astype(F32)).astype(BF16)
    lm = lmat[...]
    c = _dot(lm, hi) + _dot(lm, mid) + _dot(lm, lo) + carry[...]
    cum_o[...] = c * LOG2E
    rows = c.shape[0]
    carry[...] = c[rows - 1:rows, :]


def _prep(z, misc, gains, gmat, lmat, batch, seq):
    n = z.shape[0]
    tp = lmat.shape[0]
    nt = seq // tp

    def zspec(width, off):
        return pl.BlockSpec((tp, width), lambda b, t, _c=off // width: (b * nt + t, _c))

    def ospec(width):
        return pl.BlockSpec((tp, width), lambda b, t: (b * nt + t, 0))

    return pl.pallas_call(
        _prep_kernel,
        grid=(batch, nt),
        in_specs=[zspec(512, OFF_AQ), zspec(256, OFF_AK), zspec(512, OFF_BQ), zspec(256, OFF_BK),
                  zspec(512, OFF_CQ), zspec(512, OFF_CK), ospec(LANES),
                  pl.BlockSpec((8, 512), lambda b, t: (0, 0)),
                  pl.BlockSpec((512, 512), lambda b, t: (0, 0)),
                  pl.BlockSpec((tp, tp), lambda b, t: (0, 0))],
        out_specs=[ospec(512), ospec(256), ospec(512), ospec(256), ospec(512), ospec(512), ospec(LANES)],
        out_shape=[jax.ShapeDtypeStruct((n, 512), BF16), jax.ShapeDtypeStruct((n, 256), BF16),
                   jax.ShapeDtypeStruct((n, 512), BF16), jax.ShapeDtypeStruct((n, 256), BF16),
                   jax.ShapeDtypeStruct((n, 512), BF16), jax.ShapeDtypeStruct((n, 512), BF16),
                   jax.ShapeDtypeStruct((n, LANES), F32)],
        scratch_shapes=[pltpu.VMEM((1, LANES), F32)],
        compiler_params=_params(("arbitrary", "arbitrary"), 48),
        name="prep",
    )(z, z, z, z, z, z, misc, gains, gmat, lmat)


def _head_qt(q_ref, h):
    p = h // 2
    qt = q_ref[:, p * LANES:(p + 1) * LANES].astype(F32).T
    row = lax.broadcasted_iota(jnp.int32, qt.shape, 0)
    own = (row < HEAD_DIM) if h % 2 == 0 else (row >= HEAD_DIM)
    return jnp.where(own, qt, 0.0).astype(BF16)


DEN_ROWS = 16
ACC_ROWS = HEAD_DIM + DEN_ROWS


def _with_ones(vt):
    return jnp.concatenate([vt, jnp.ones((DEN_ROWS, vt.shape[1]), vt.dtype)], axis=0)


def _softmax_step(h, s, vt, m_ref, acc_ref, qbias=None):
    m_prev = m_ref[h]
    s_max = jnp.max(s, axis=0, keepdims=True)
    m_new = jnp.maximum(m_prev, s_max if qbias is None else s_max + qbias)
    alpha = jnp.exp2(m_prev - m_new)
    shift = m_new[0:1, :] if qbias is None else m_new[0:1, :] - qbias
    p = jnp.exp2(s - shift)
    acc_ref[h] = alpha[0:1, :] * acc_ref[h] + _dot(_with_ones(vt), p.astype(BF16))
    m_ref[h] = m_new


HEADS_AHEAD = 4


def _heads_pipelined(logits, values, m_ref, acc_ref, qbias=None):
    ahead = [logits(h) for h in range(HEADS_AHEAD)]
    for h in range(N_HEADS):
        if h + HEADS_AHEAD < N_HEADS:
            ahead.append(logits(h + HEADS_AHEAD))
        _softmax_step(h, ahead.pop(0), values(h), m_ref, acc_ref,
                      None if qbias is None else qbias(h))


def _init_state(m_ref, acc_ref):
    m_ref[...] = jnp.full(m_ref.shape, NEG, F32)
    acc_ref[...] = jnp.zeros(acc_ref.shape, F32)


def _gated_pair(p, gate, acc_ref):
    halves = []
    for h in (2 * p, 2 * p + 1):
        a = acc_ref[h]
        halves.append(a[:HEAD_DIM] / a[HEAD_DIM:HEAD_DIM + 1])
    ot = jnp.concatenate(halves, axis=0)
    g = gate.astype(F32)
    return (ot.T * (g * jax.nn.sigmoid(g))).astype(BF16)


def _mix_a_kernel(sink_ref, q_ref, kp_ref, kc_ref, vp_ref, vc_ref, gate_ref, tbl_ref, o_ref, acc_ref):
    n = pl.program_id(1)
    pad_pen = jnp.where(n > 0, 0.0, NEG).astype(F32)
    den_rows = lax.broadcasted_iota(jnp.int32, (ACC_ROWS, WINDOW), 0) >= HEAD_DIM

    def logits(h):
        qt = _head_qt(q_ref, h)
        ksl = slice((h // KV_GROUP) * LANES, (h // KV_GROUP + 1) * LANES)
        return (_dot(kp_ref[:, ksl], qt) + tbl_ref[h, :WINDOW, :] + pad_pen,
                _dot(kc_ref[:, ksl], qt) + tbl_ref[h, WINDOW:, :])

    all_logits = [logits(h) for h in range(N_HEADS)]
    for h in range(N_HEADS):
        kv = h // KV_GROUP
        vsl = slice(kv * HEAD_DIM, (kv + 1) * HEAD_DIM)
        s_prev, s_cur = all_logits[h]
        sink = sink_ref[h] * LOG2E
        m = jnp.maximum(jnp.maximum(jnp.max(s_prev, axis=0, keepdims=True),
                                    jnp.max(s_cur, axis=0, keepdims=True)), sink)
        p_prev = jnp.exp2(s_prev - m)
        p_cur = jnp.exp2(s_cur - m)
        acc = (_dot(_with_ones(vp_ref[vsl, :]), p_prev.astype(BF16))
               + _dot(_with_ones(vc_ref[vsl, :]), p_cur.astype(BF16)))
        acc_ref[h] = acc + jnp.where(den_rows, jnp.exp2(sink - m), 0.0)
    for p in range(N_PAIRS):
        sl = slice(p * LANES, (p + 1) * LANES)
        o_ref[:, sl] = _gated_pair(p, gate_ref[:, sl], acc_ref)


def _mix_a(qa, ka, z, vt, tbl, sinks, batch, seq):
    n = qa.shape[0]
    t = WINDOW
    nb = seq // t
    return pl.pallas_call(
        _mix_a_kernel,
        grid=(batch, nb),
        in_specs=[pl.BlockSpec(memory_space=pltpu.SMEM),
                  pl.BlockSpec((t, 512), lambda b, i: (b * nb + i, 0)),
                  pl.BlockSpec((t, 256), lambda b, i: (b * nb + jnp.maximum(i - 1, 0), 0)),
                  pl.BlockSpec((t, 256), lambda b, i: (b * nb + i, 0)),
                  pl.BlockSpec((LANES, t), lambda b, i: (VT_A // LANES, b * nb + jnp.maximum(i - 1, 0))),
                  pl.BlockSpec((LANES, t), lambda b, i: (VT_A // LANES, b * nb + i)),
                  pl.BlockSpec((t, 512), lambda b, i: (b * nb + i, OFF_AG // 512)),
                  pl.BlockSpec((N_HEADS, 2 * t, t), lambda b, i: (0, 0, 0))],
        out_specs=pl.BlockSpec((t, 512), lambda b, i: (b * nb + i, 0)),
        out_shape=jax.ShapeDtypeStruct((n, 512), BF16),
        scratch_shapes=[pltpu.VMEM((N_HEADS, ACC_ROWS, t), F32)],
        compiler_params=_params(("arbitrary", "arbitrary"), 32),
        name="mix_a",
    )(sinks, qa, ka, ka, vt, vt, z, tbl)


SEL_TQ = 256
SEL_SC = 512
SEL_SUB = 64
SEL_MC = 256
SEL_FAST_ITERS = 24
SEL_MAX_ITERS = 600


def _sel_b_kernel(iq_ref, ik_ref, wt_ref, lmat_ref, o_ref, sc_ref, *, topk):
    n = pl.program_id(1)
    t0 = n * SEL_TQ
    n_full = n // (SEL_SC // SEL_TQ)
    n_sc = n_full + 1
    n_mc = n + 1
    kf = float(topk)

    iqt = iq_ref[...].astype(F32).T
    row = lax.broadcasted_iota(jnp.int32, iqt.shape, 0)
    qst = jnp.concatenate(
        [jnp.where((row >= h * IDX_DIM) & (row < (h + 1) * IDX_DIM), iqt, 0.0)
         for h in range(IDX_HEADS)], axis=1).astype(BF16)
    wscale = (IDX_HEADS ** -0.5) * (IDX_DIM ** -0.5)
    w = [wt_ref[h:h + 1, :] * wscale for h in range(IDX_HEADS)]
    qpos = t0 + lax.broadcasted_iota(jnp.int32, (SEL_SUB, SEL_TQ), 1)
    kiota = lax.broadcasted_iota(jnp.int32, (SEL_SUB, SEL_TQ), 0)

    def score_chunk(c, carry, diagonal):
        rmax, rmin, ge0, gt0, minpos = carry
        base = pl.multiple_of(c * SEL_SC, SEL_SC)
        raw = _dot(ik_ref[pl.ds(base, SEL_SC), :], qst)
        for u in range(SEL_SC // SEL_SUB):
            rows = slice(u * SEL_SUB, (u + 1) * SEL_SUB)
            score = w[0] * jnp.maximum(raw[rows, 0:SEL_TQ], 0.0)
            for h in range(1, IDX_HEADS):
                score = score + w[h] * jnp.maximum(raw[rows, h * SEL_TQ:(h + 1) * SEL_TQ], 0.0)
            rmin = jnp.minimum(rmin, score)
            if diagonal:
                score = jnp.where((kiota + (base + u * SEL_SUB)) <= qpos, score, -jnp.inf)
            sc_ref[pl.ds(base + u * SEL_SUB, SEL_SUB), :] = score
            rmax = jnp.maximum(rmax, score)
            pos = score > 0.0
            ge0 = ge0 + jnp.where(score >= 0.0, 1.0, 0.0)
            gt0 = gt0 + jnp.where(pos, 1.0, 0.0)
            minpos = jnp.minimum(minpos, jnp.where(pos, score, jnp.inf))
        return rmax, rmin, ge0, gt0, minpos

    def slab(v):
        return jnp.full((SEL_SUB, SEL_TQ), v, F32)

    carry = lax.fori_loop(0, n_full, lambda c, cr: score_chunk(c, cr, False),
                          (slab(-jnp.inf), slab(jnp.inf), slab(0.0), slab(0.0), slab(jnp.inf)))
    rmax, rmin, ge0, gt0, minpos = score_chunk(n_full, carry, True)
    rmax = jnp.max(rmax, axis=0, keepdims=True)
    rmin = jnp.min(rmin, axis=0, keepdims=True)
    ge0 = jnp.sum(ge0, axis=0, keepdims=True)
    gt0 = jnp.sum(gt0, axis=0, keepdims=True)
    minpos = jnp.min(minpos, axis=0, keepdims=True)

    navail = (t0 + 1 + lax.broadcasted_iota(jnp.int32, (1, SEL_TQ), 1)).astype(F32)

    def sweep(mid, snap):
        midb = jnp.broadcast_to(mid, (SEL_SUB, SEL_TQ))

        def body(c, carry):
            base = pl.multiple_of(c * SEL_SC, SEL_SC)
            cnt, up, dn = carry
            for u in range(SEL_SC // SEL_SUB):
                x = sc_ref[pl.ds(base + u * SEL_SUB, SEL_SUB), :]
                ge = x >= midb
                cnt = cnt + jnp.where(ge, 1.0, 0.0)
                if snap:
                    up = jnp.minimum(up, jnp.where(ge, x, jnp.inf))
                    dn = jnp.maximum(dn, jnp.where(ge, -jnp.inf, x))
            return cnt, up, dn

        init = (jnp.zeros((SEL_SUB, SEL_TQ), F32), jnp.full((SEL_SUB, SEL_TQ), jnp.inf, F32),
                jnp.full((SEL_SUB, SEL_TQ), -jnp.inf, F32))
        cnt, up, dn = lax.fori_loop(0, n_sc, body, init)
        return (jnp.sum(cnt, axis=0, keepdims=True), jnp.min(up, axis=0, keepdims=True),
                jnp.max(dn, axis=0, keepdims=True))

    def advance(st, mid, snap):
        it, lo, hi, hi_dn, c_lo, c_hi, done = st
        cnt, up, dn = sweep(mid, snap)
        live = jnp.where(done > 0.5, 0.0, jnp.where(mid <= lo, 0.0, jnp.where(mid >= hi, 0.0, 1.0)))
        ge = jnp.where(cnt >= kf, live, 0.0) > 0.5
        lt = jnp.where(cnt >= kf, 0.0, live) > 0.5
        lo = jnp.where(ge, up if snap else mid, lo)
        c_lo = jnp.where(ge, cnt, c_lo)
        hi = jnp.where(lt, mid, hi)
        hi_dn = jnp.where(lt, dn if snap else jnp.inf, hi_dn)
        c_hi = jnp.where(lt, cnt, c_hi)
        done = jnp.where(live < 0.5, 1.0, jnp.where(cnt == kf, 1.0, jnp.where(hi_dn <= lo, 1.0, 0.0)))
        return it + 1, lo, hi, hi_dn, c_lo, c_hi, done

    def make_step(snap, repeat=1):
        def step(st):
            for _ in range(repeat):
                st = advance(st, 0.5 * st[1] + 0.5 * st[2], snap)
            return st
        return step

    def make_cond(limit):
        def cond(st):
            return jnp.logical_and(st[0] < limit, jnp.min(st[6]) < 0.5)
        return cond

    inf_row = jnp.full((1, SEL_TQ), jnp.inf, F32)
    all_in = navail <= kf
    above = jnp.logical_and(jnp.logical_not(all_in), gt0 >= kf)
    at_zero = jnp.logical_and(jnp.logical_not(all_in), jnp.logical_and(gt0 < kf, ge0 >= kf))
    below = jnp.logical_and(jnp.logical_not(all_in), ge0 < kf)
    lo = jnp.where(above, minpos, jnp.where(at_zero, 0.0, rmin))
    c_lo = jnp.where(above, gt0, jnp.where(at_zero, ge0, navail))
    hi = jnp.where(at_zero, minpos, jnp.where(below, 0.0, inf_row))
    c_hi = jnp.where(at_zero, gt0, jnp.where(below, ge0, 0.0))
    hi_dn = jnp.where(above, rmax, inf_row)
    done = jnp.where(all_in, 1.0, jnp.where(at_zero, 1.0, 0.0))
    st = (jnp.int32(0), lo, hi, hi_dn, c_lo, c_hi, done)
    st = advance(st, jnp.where(above, rmax, 0.5 * lo + 0.5 * hi), False)
    st = lax.while_loop(make_cond(SEL_FAST_ITERS), make_step(False, repeat=2), st)
    st = lax.while_loop(make_cond(SEL_MAX_ITERS), make_step(True), st)
    _, lo, hi, _, c_lo, c_hi, _ = st
    need = kf - c_hi
    ties = jnp.max(jnp.where(navail > kf, c_lo, kf)) > kf

    def fill_body(c, carry):
        base = pl.multiple_of(c * SEL_MC, SEL_MC)
        o_ref[0, pl.ds(base, SEL_MC), :] = jnp.full((SEL_MC, SEL_TQ), NEG, o_ref.dtype)
        return carry
    lax.fori_loop(n_mc, o_ref.shape[1] // SEL_MC, fill_body, 0)

    @pl.when(jnp.logical_not(ties))
    def _():
        def mask_body(c, carry):
            base = pl.multiple_of(c * SEL_MC, SEL_MC)
            x = sc_ref[pl.ds(base, SEL_MC), :]
            o_ref[0, pl.ds(base, SEL_MC), :] = jnp.where(x >= lo, 0.0, NEG).astype(o_ref.dtype)
            return carry
        lax.fori_loop(0, n_mc, mask_body, 0)

    @pl.when(ties)
    def _():
        lmat = lmat_ref[...]

        def mask_body(c, run):
            base = pl.multiple_of(c * SEL_MC, SEL_MC)
            x = sc_ref[pl.ds(base, SEL_MC), :]
            bnd = jnp.where(x >= lo, jnp.where(x < hi, 1.0, 0.0), 0.0)
            incl = _dot(lmat, bnd.astype(BF16)) + run
            keep = jnp.where(x >= hi, 1.0, jnp.where(incl - bnd < need, bnd, 0.0))
            o_ref[0, pl.ds(base, SEL_MC), :] = jnp.where(keep > 0.5, 0.0, NEG).astype(o_ref.dtype)
            return incl[SEL_MC - 1:SEL_MC, :]
        lax.fori_loop(0, n_mc, mask_body, jnp.zeros((1, SEL_TQ), F32))


def _sel_b(z, st, lmat, batch, seq):
    nb = seq // SEL_TQ
    topk = min(TOPK_MAX, seq // 4)
    return pl.pallas_call(
        functools.partial(_sel_b_kernel, topk=topk),
        grid=(batch, nb),
        in_specs=[pl.BlockSpec((SEL_TQ, LANES), lambda b, i: (b * nb + i, OFF_BIQ // LANES)),
                  pl.BlockSpec((seq, LANES), lambda b, i: (b, OFF_BIK // LANES)),
                  pl.BlockSpec((8, SEL_TQ), lambda b, i: (0, b * nb + i)),
                  pl.BlockSpec((SEL_MC, SEL_MC), lambda b, i: (0, 0))],
        out_specs=pl.BlockSpec((1, seq, SEL_TQ), lambda b, i: (b, 0, i)),
        out_shape=jax.ShapeDtypeStruct((batch, seq, seq), BF16),
        scratch_shapes=[pltpu.VMEM((seq, SEL_TQ), F32)],
        compiler_params=_params(("arbitrary", "arbitrary"), 48),
        name="sel_b",
    )(z, z, st, lmat)


ATT_T = 256
B_NEAR = 3
FAR_TILES = 2


def _far_remainder(n_tiles, tile):
    done = (n_tiles // FAR_TILES) * FAR_TILES
    rem = n_tiles - done
    size = FAR_TILES // 2
    while size >= 1:
        @pl.when((rem // size) % 2 == 1)
        def _(size=size, start=done + (rem // (2 * size)) * (2 * size)):
            tile(start, size * ATT_T)
        size //= 2


def _mix_b_kernel(q_ref, k_ref, vt_ref, mask_ref, gate_ref, tbl_ref, o_ref, m_ref, acc_ref):
    i = pl.program_id(1)
    _init_state(m_ref, acc_ref)
    qs = [_head_qt(q_ref, h) for h in range(N_HEADS)]

    def tile(j, size, near):
        base = pl.multiple_of(j * ATT_T, ATT_T)
        madd = mask_ref[0, pl.ds(base, size), :].astype(F32)

        def logits(h):
            kv = h // KV_GROUP
            s = _dot(k_ref[pl.ds(base, size), kv * LANES:(kv + 1) * LANES], qs[h]) + madd
            return s if near is None else s + tbl_ref[h, near]

        def values(h):
            kv = h // KV_GROUP
            return vt_ref[kv * HEAD_DIM:(kv + 1) * HEAD_DIM, pl.ds(base, size)]

        _heads_pipelined(logits, values, m_ref, acc_ref)

    def far_body(j, carry):
        tile(j * FAR_TILES, FAR_TILES * ATT_T, None)
        return carry

    n_far = jnp.maximum(i - (B_NEAR - 1), 0)
    lax.fori_loop(0, n_far // FAR_TILES, far_body, 0)
    _far_remainder(n_far, lambda j, size: tile(j, size, None))

    for d in range(B_NEAR - 1, -1, -1):
        @pl.when(i >= d)
        def _(d=d):
            tile(i - d, ATT_T, d)
    for p in range(N_PAIRS):
        sl = slice(p * LANES, (p + 1) * LANES)
        o_ref[:, sl] = _gated_pair(p, gate_ref[:, sl], acc_ref)


def _mix_b(qb, kb, z, vt, maskadd, tbl, batch, seq):
    n = qb.shape[0]
    t = ATT_T
    nq = seq // t
    return pl.pallas_call(
        _mix_b_kernel,
        grid=(batch, nq),
        in_specs=[pl.BlockSpec((t, 512), lambda b, i: (b * nq + i, 0)),
                  pl.BlockSpec((seq, 256), lambda b, i: (b, 0)),
                  pl.BlockSpec((LANES, seq), lambda b, i: (VT_B // LANES, b)),
                  pl.BlockSpec((1, seq, t), lambda b, i: (b, 0, i)),
                  pl.BlockSpec((t, 512), lambda b, i: (b * nq + i, OFF_BG // 512)),
                  pl.BlockSpec((N_HEADS, B_NEAR, t, t), lambda b, i: (0, 0, 0, 0))],
        out_specs=pl.BlockSpec((t, 512), lambda b, i: (b * nq + i, 0)),
        out_shape=jax.ShapeDtypeStruct((n, 512), BF16),
        scratch_shapes=[pltpu.VMEM((N_HEADS, 8, t), F32), pltpu.VMEM((N_HEADS, ACC_ROWS, t), F32)],
        compiler_params=_params(("arbitrary", "arbitrary"), 56),
        name="mix_b",
    )(qb, kb, vt, maskadd, z, tbl)


def _mix_c_kernel(q_ref, k_ref, vt_ref, ck_ref, cq_ref, gate_ref, o_ref, m_ref, acc_ref):
    i = pl.program_id(1)
    _init_state(m_ref, acc_ref)
    qs = [_head_qt(q_ref, h) for h in range(N_HEADS)]
    q0 = pl.multiple_of(i * ATT_T, ATT_T)
    cq = cq_ref[0, :, pl.ds(q0, ATT_T)]
    krow = lax.broadcasted_iota(jnp.int32, (ATT_T, ATT_T), 0)
    qcol = lax.broadcasted_iota(jnp.int32, (ATT_T, ATT_T), 1)

    def tile(j, size, diag):
        base = pl.multiple_of(j * ATT_T, ATT_T)
        ck_all = ck_ref[pl.ds(base, size), :]

        def logits(h):
            p = h // 2
            ck = ck_all[:, MISC_CF + h:MISC_CF + h + 1]
            s = _dot(k_ref[pl.ds(base, size), p * LANES:(p + 1) * LANES], qs[h]) - ck
            return jnp.where(krow <= qcol, s, NEG) if diag else s

        def values(h):
            return vt_ref[h * HEAD_DIM:(h + 1) * HEAD_DIM, pl.ds(base, size)]

        _heads_pipelined(logits, values, m_ref, acc_ref, qbias=lambda h: cq[h:h + 1, :])

    def body(j, carry):
        tile(j * FAR_TILES, FAR_TILES * ATT_T, False)
        return carry

    lax.fori_loop(0, i // FAR_TILES, body, 0)
    _far_remainder(i, lambda j, size: tile(j, size, False))
    tile(i, ATT_T, True)
    for p in range(N_PAIRS):
        sl = slice(p * LANES, (p + 1) * LANES)
        o_ref[:, sl] = _gated_pair(p, gate_ref[:, sl], acc_ref)


def _mix_c(qc, kc, z, vt, cum, cum_t, batch, seq):
    n = qc.shape[0]
    t = ATT_T
    nq = seq // t
    return pl.pallas_call(
        _mix_c_kernel,
        grid=(batch, nq),
        in_specs=[pl.BlockSpec((t, 512), lambda b, i: (b * nq + i, 0)),
                  pl.BlockSpec((seq, 512), lambda b, i: (b, 0)),
                  pl.BlockSpec((512, seq), lambda b, i: (VT_C // 512, b)),
                  pl.BlockSpec((seq, LANES), lambda b, i: (b, 0)),
                  pl.BlockSpec((1, N_HEADS, seq), lambda b, i: (b, 0, 0)),
                  pl.BlockSpec((t, 512), lambda b, i: (b * nq + i, OFF_CG // 512))],
        out_specs=pl.BlockSpec((t, 512), lambda b, i: (b * nq + i, 0)),
        out_shape=jax.ShapeDtypeStruct((n, 512), BF16),
        scratch_shapes=[pltpu.VMEM((N_HEADS, 8, t), F32), pltpu.VMEM((N_HEADS, ACC_ROWS, t), F32)],
        compiler_params=_params(("arbitrary", "arbitrary"), 56),
        name="mix_c",
    )(qc, kc, vt, cum, cum_t, z)


def _merge_kernel(oa, ob, oc, ga, gb, gc, x_ref, wb_ref, wo_ref, o_ref):
    merged = None
    for br, (o, g) in enumerate(((oa, ga), (ob, gb), (oc, gc))):
        y = _dot(o[...], wb_ref[br])
        term = jax.nn.sigmoid(g[...].astype(F32)) * y
        merged = term if merged is None else merged + term
    o_ref[...] = x_ref[...] + _dot(merged.astype(BF16), wo_ref[...])


def _merge(oa, ob, oc, z, x2d, wb, wo):
    n, d = x2d.shape
    tm = 512

    def ospec():
        return pl.BlockSpec((tm, 512), lambda i: (i, 0))

    def gspec(k):
        return pl.BlockSpec((tm, d), lambda i, _k=k: (i, OFF_MERGE // d + _k))

    return pl.pallas_call(
        _merge_kernel,
        grid=(n // tm,),
        in_specs=[ospec(), ospec(), ospec(), gspec(0), gspec(1), gspec(2),
                  pl.BlockSpec((tm, d), lambda i: (i, 0)),
                  pl.BlockSpec((3, 512, d), lambda i: (0, 0, 0)),
                  pl.BlockSpec((d, d), lambda i: (0, 0))],
        out_specs=pl.BlockSpec((tm, d), lambda i: (i, 0)),
        out_shape=jax.ShapeDtypeStruct((n, d), F32),
        compiler_params=_params(("arbitrary",), 48),
        name="merge",
    )(oa, ob, oc, z, z, z, x2d, wb, wo)


def _bucket_lookup(table, bucket):
    out = jnp.zeros(bucket.shape + (table.shape[1],), F32)
    for b in range(N_BUCKETS):
        out = jnp.where((bucket == b)[..., None], table[b].astype(F32), out)
    return out


def _window_table(bias_a):
    ki = jnp.arange(2 * WINDOW)[:, None]
    qi = jnp.arange(WINDOW)[None, :]
    delta = qi + WINDOW - ki
    band = (delta >= 0) & (delta < WINDOW)
    bias = _bucket_lookup(bias_a, _t5_bucket(delta)).transpose(2, 0, 1) * LOG2E
    return jnp.where(band[None], bias, NEG)


def _near_table(bias_b):
    ki = jnp.arange(ATT_T)[:, None]
    qi = jnp.arange(ATT_T)[None, :]
    d = jnp.arange(B_NEAR)[:, None, None]
    delta = d * ATT_T + qi[None] - ki[None]
    bias = _bucket_lookup(bias_b, _t5_bucket(delta))
    far = bias_b[N_BUCKETS - 1].astype(F32)
    return ((bias - far) * LOG2E).transpose(3, 0, 1, 2)


def kernel(x, norm_gain, w_in, b_forget, qk_gain, sinks, w_branch, w_out, rel_bias):
    batch, seq, d = x.shape
    depth = norm_gain.shape[0]
    assert w_in.shape[-1] == IN_COLS and seq % 1024 == 0 and d == 1024
    assert (B_NEAR - 1) * ATT_T - (ATT_T - 1) < T5_FAR <= B_NEAR * ATT_T - (ATT_T - 1)

    w_perm = _gather_cols(w_in, _COLS).astype(BF16)
    w_vt = jnp.swapaxes(_gather_cols(w_in, _VT_COLS), 1, 2).astype(BF16)
    w_st = jnp.swapaxes(_gather_cols(w_in, _ST_COLS), 1, 2).astype(BF16)
    wb = w_branch.astype(BF16)
    wo = w_out.astype(BF16)

    gmat = jnp.asarray(np.kron(np.eye(N_HEADS), np.ones((HEAD_DIM, HEAD_DIM))), BF16)
    prep_t = 512
    lmat = jnp.asarray(np.tril(np.ones((prep_t, prep_t))), BF16)
    lmat_sel = lmat[:SEL_MC, :SEL_MC]
    tbl_a = _window_table(rel_bias[:, :N_HEADS])
    tbl_b = _near_table(rel_bias[:, N_HEADS:])

    x2d = x.reshape(batch * seq, d)
    for layer in range(depth):
        g = qk_gain[layer]
        qscale = ATTN_SCALE * LOG2E
        gains = jnp.zeros((8, 512), F32)
        gains = gains.at[0].set(jnp.tile(g[0, 0], N_HEADS) * qscale)
        gains = gains.at[1, :256].set(jnp.tile(g[0, 1], 4))
        gains = gains.at[2].set(jnp.tile(g[1, 0], N_HEADS) * qscale)
        gains = gains.at[3, :256].set(jnp.tile(g[1, 1], 4))
        gains = gains.at[4].set(jnp.tile(g[2, 0], N_HEADS) * qscale)
        gains = gains.at[5].set(jnp.tile(g[2, 1], N_HEADS))
        gains = gains.at[6, MISC_CF:MISC_CF + N_HEADS].set(b_forget[layer])

        z, misc, vt, st = _proj(x2d, norm_gain[layer][None, :], w_perm[layer], w_vt[layer], w_st[layer])
        qa, ka, qb, kb, qc, kc, cum = _prep(z, misc, gains, gmat, lmat, batch, seq)
        cum_t = cum.reshape(batch, seq, LANES)[:, :, MISC_CF:MISC_CF + N_HEADS].transpose(0, 2, 1)
        oa = _mix_a(qa, ka, z, vt, tbl_a, sinks[layer], batch, seq)
        maskadd = _sel_b(z, st, lmat_sel, batch, seq)
        ob = _mix_b(qb, kb, z, vt, maskadd, tbl_b, batch, seq)
        oc = _mix_c(qc, kc, z, vt, cum, cum_t, batch, seq)
        x2d = _merge(oa, ob, oc, z, x2d, wb[layer], wo[layer])
    return x2d.reshape(batch, seq, d)
```

```python
import functools
import math

import numpy as np
import jax
import jax.numpy as jnp
from jax import lax
from jax.experimental import pallas as pl
from jax.experimental.pallas import tpu as pltpu

F32 = jnp.float32
BF16 = jnp.bfloat16

HEAD_DIM = 64
LANES = 128
N_HEADS = 8
N_PAIRS = N_HEADS // 2
KV_GROUP = 4
WINDOW = 128
IDX_HEADS = 4
IDX_DIM = 32
TOPK_MAX = 256
N_BUCKETS = 32
MAX_DISTANCE = 512
EPS = 1e-6
ATTN_SCALE = HEAD_DIM ** -0.5
LOG2E = math.log2(math.e)
NEG = -1e30
T5_FAR = 413

_SIZES = dict(a_q=512, a_k=128, a_v=128, a_gate=512,
              b_q=512, b_k=128, b_v=128, b_iq=128, b_ik=32, b_iw=4, b_gate=512,
              c_q=512, c_k=512, c_v=512, c_f=8, c_gate=512, merge=3072)
_ORIG = {}
_o = 0
for _k, _v in _SIZES.items():
    _ORIG[_k] = _o
    _o += _v
IN_COLS = _o

OFF_MERGE = 0
OFF_AQ, OFF_AG, OFF_BQ, OFF_BG = 3072, 3584, 4096, 4608
OFF_CQ, OFF_CK, OFF_CG = 5120, 5632, 6144
OFF_AK, OFF_BK = 6656, 6912
OFF_BIQ, OFF_BIK, OFF_MISC = 7168, 7296, 7424
NP_COLS = 7680
PROJ_TN = 1536
MISC_CF = 8
VT_C, VT_A, VT_B, VT_ROWS = 0, 512, 640, 768


def _orig(name, idx=None):
    src = np.arange(_SIZES[name]) + _ORIG[name]
    return src if idx is None else src[idx]


def _column_map():
    cols = np.full((NP_COLS,), -1, np.int64)

    def put(off, src):
        cols[off:off + len(src)] = src

    for off, name in ((OFF_MERGE, "merge"), (OFF_AQ, "a_q"), (OFF_AG, "a_gate"), (OFF_BQ, "b_q"),
                      (OFF_BG, "b_gate"), (OFF_CQ, "c_q"), (OFF_CK, "c_k"), (OFF_CG, "c_gate"),
                      (OFF_BIQ, "b_iq")):
        put(off, _orig(name))
    dup = np.concatenate([np.arange(64), np.arange(64), np.arange(64, 128), np.arange(64, 128)])
    put(OFF_AK, _orig("a_k", dup))
    put(OFF_BK, _orig("b_k", dup))
    put(OFF_BIK, _orig("b_ik", np.tile(np.arange(IDX_DIM), IDX_HEADS)))
    put(OFF_MISC + MISC_CF, _orig("c_f"))
    return cols


def _gather_cols(w, cols):
    parts = []
    start = 0
    while start < len(cols):
        end = start + 1
        if cols[start] < 0:
            while end < len(cols) and cols[end] < 0:
                end += 1
            parts.append(jnp.zeros(w.shape[:-1] + (end - start,), w.dtype))
        else:
            while end < len(cols) and cols[end] == cols[end - 1] + 1:
                end += 1
            parts.append(w[..., int(cols[start]):int(cols[end - 1]) + 1])
        start = end
    return jnp.concatenate(parts, axis=-1)


_COLS = _column_map()
_VT_COLS = np.concatenate([_orig("c_v"), _orig("a_v"), _orig("b_v")])
_ST_COLS = np.concatenate([_orig("b_iw"), np.full((4,), -1, np.int64)])


def _params(sem, vmem_mb):
    return pltpu.CompilerParams(dimension_semantics=sem, vmem_limit_bytes=vmem_mb * 1024 * 1024)


def _dot(a, b):
    return jnp.dot(a, b, preferred_element_type=F32)


def _dot_nt(a, b):
    return lax.dot_general(a, b, (((1,), (1,)), ((), ())), preferred_element_type=F32)


def _t5_bucket(delta):
    n = jnp.maximum(delta, 0)
    max_exact = N_BUCKETS // 2
    nf = jnp.maximum(n, 1).astype(F32)
    large = max_exact + (jnp.log(nf / max_exact) / math.log(MAX_DISTANCE / max_exact)
                         * (N_BUCKETS - max_exact)).astype(jnp.int32)
    large = jnp.minimum(large, N_BUCKETS - 1)
    return jnp.where(n < max_exact, n, large)


def _proj_kernel(x_ref, g_ref, w_ref, wvt_ref, wst_ref, z_ref, misc_ref, vt_ref, st_ref, h_ref,
                 *, misc_tile, misc_local):
    j = pl.program_id(1)

    @pl.when(j == 0)
    def _():
        x = x_ref[...]
        ms = jnp.mean(x * x, axis=-1, keepdims=True)
        h = (x * lax.rsqrt(ms + EPS) * g_ref[...]).astype(BF16)
        h_ref[...] = h
        vt_ref[...] = _dot_nt(wvt_ref[...], h).astype(BF16)
        st_ref[...] = _dot_nt(wst_ref[...], h)

    acc = _dot(h_ref[...], w_ref[...])
    z_ref[...] = acc.astype(BF16)

    @pl.when(j == misc_tile)
    def _():
        misc_ref[...] = acc[:, misc_local:misc_local + LANES]


def _proj(x2d, gain, w, wvt, wst):
    n, d = x2d.shape
    tm = 1024
    return pl.pallas_call(
        functools.partial(_proj_kernel, misc_tile=OFF_MISC // PROJ_TN, misc_local=OFF_MISC % PROJ_TN),
        grid=(n // tm, NP_COLS // PROJ_TN),
        in_specs=[pl.BlockSpec((tm, d), lambda i, j: (i, 0)),
                  pl.BlockSpec((1, d), lambda i, j: (0, 0)),
                  pl.BlockSpec((d, PROJ_TN), lambda i, j: (0, j)),
                  pl.BlockSpec((VT_ROWS, d), lambda i, j: (0, 0)),
                  pl.BlockSpec((8, d), lambda i, j: (0, 0))],
        out_specs=[pl.BlockSpec((tm, PROJ_TN), lambda i, j: (i, j)),
                   pl.BlockSpec((tm, LANES), lambda i, j: (i, 0)),
                   pl.BlockSpec((VT_ROWS, tm), lambda i, j: (0, i)),
                   pl.BlockSpec((8, tm), lambda i, j: (0, i))],
        out_shape=[jax.ShapeDtypeStruct((n, NP_COLS), BF16),
                   jax.ShapeDtypeStruct((n, LANES), F32),
                   jax.ShapeDtypeStruct((VT_ROWS, n), BF16),
                   jax.ShapeDtypeStruct((8, n), F32)],
        scratch_shapes=[pltpu.VMEM((tm, d), BF16)],
        compiler_params=_params(("arbitrary", "arbitrary"), 56),
        name="proj",
    )(x2d, gain, w, wvt, wst)


def _group_rms(x_bf16, g_ref, gain_row):
    x = x_bf16.astype(F32)
    width = x.shape[-1]
    ss = _dot((x * x).astype(BF16), g_ref[:width, :width])
    return (x * lax.rsqrt(ss * (1.0 / HEAD_DIM) + EPS) * gain_row).astype(BF16)


def _prep_kernel(aq, ak, bq, bk, cq, ck, misc, gains, gmat, lmat,
                 qa_o, ka_o, qb_o, kb_o, qc_o, kc_o, cum_o, carry):
    t = pl.program_id(1)
    qa_o[...] = _group_rms(aq[...], gmat, gains[0:1, :])
    ka_o[...] = _group_rms(ak[...], gmat, gains[1:2, :256])
    qb_o[...] = _group_rms(bq[...], gmat, gains[2:3, :])
    kb_o[...] = _group_rms(bk[...], gmat, gains[3:4, :256])
    qc_o[...] = _group_rms(cq[...], gmat, gains[4:5, :])
    kc_o[...] = _group_rms(ck[...], gmat, gains[5:6, :])

    @pl.when(t == 0)
    def _():
        carry[...] = jnp.zeros_like(carry)

    xm = misc[...] + gains[6:7, :LANES]
    logf = jnp.minimum(xm, 0.0) - jnp.log(1.0 + jnp.exp(-jnp.abs(xm)))
    hi = logf.astype(BF16)
    r1 = logf - hi.astype(F32)
    mid = r1.astype(BF16)
    lo = (r1 - mid.astype(F32)).astype(BF16)
    lm = lmat[...]
    c = _dot(lm, hi) + _dot(lm, mid) + _dot(lm, lo) + carry[...]
    cum_o[...] = c * LOG2E
    rows = c.shape[0]
    carry[...] = c[rows - 1:rows, :]


def _prep(z, misc, gains, gmat, lmat, batch, seq):
    n = z.shape[0]
    tp = lmat.shape[0]
    nt = seq // tp

    def zspec(width, off):
        return pl.BlockSpec((tp, width), lambda b, t, _c=off // width: (b * nt + t, _c))

    def ospec(width):
        return pl.BlockSpec((tp, width), lambda b, t: (b * nt + t, 0))

    return pl.pallas_call(
        _prep_kernel,
        grid=(batch, nt),
        in_specs=[zspec(512, OFF_AQ), zspec(256, OFF_AK), zspec(512, OFF_BQ), zspec(256, OFF_BK),
                  zspec(512, OFF_CQ), zspec(512, OFF_CK), ospec(LANES),
                  pl.BlockSpec((8, 512), lambda b, t: (0, 0)),
                  pl.BlockSpec((512, 512), lambda b, t: (0, 0)),
                  pl.BlockSpec((tp, tp), lambda b, t: (0, 0))],
        out_specs=[ospec(512), ospec(256), ospec(512), ospec(256), ospec(512), ospec(512), ospec(LANES)],
        out_shape=[jax.ShapeDtypeStruct((n, 512), BF16), jax.ShapeDtypeStruct((n, 256), BF16),
                   jax.ShapeDtypeStruct((n, 512), BF16), jax.ShapeDtypeStruct((n, 256), BF16),
                   jax.ShapeDtypeStruct((n, 512), BF16), jax.ShapeDtypeStruct((n, 512), BF16),
                   jax.ShapeDtypeStruct((n, LANES), F32)],
        scratch_shapes=[pltpu.VMEM((1, LANES), F32)],
        compiler_params=_params(("arbitrary", "arbitrary"), 48),
        name="prep",
    )(z, z, z, z, z, z, misc, gains, gmat, lmat)


def _head_qt(q_ref, h):
    p = h // 2
    qt = q_ref[:, p * LANES:(p + 1) * LANES].astype(F32).T
    row = lax.broadcasted_iota(jnp.int32, qt.shape, 0)
    own = (row < HEAD_DIM) if h % 2 == 0 else (row >= HEAD_DIM)
    return jnp.where(own, qt, 0.0).astype(BF16)


DEN_ROWS = 16
ACC_ROWS = HEAD_DIM + DEN_ROWS


def _with_ones(vt):
    return jnp.concatenate([vt, jnp.ones((DEN_ROWS, vt.shape[1]), vt.dtype)], axis=0)


def _softmax_step(h, s, vt, m_ref, acc_ref, qbias=None):
    m_prev = m_ref[h]
    s_max = jnp.max(s, axis=0, keepdims=True)
    m_new = jnp.maximum(m_prev, s_max if qbias is None else s_max + qbias)
    alpha = jnp.exp2(m_prev - m_new)
    shift = m_new[0:1, :] if qbias is None else m_new[0:1, :] - qbias
    p = jnp.exp2((s - shift).astype(BF16))
    acc_ref[h] = alpha[0:1, :] * acc_ref[h] + _dot(_with_ones(vt), p)
    m_ref[h] = m_new


HEADS_AHEAD = 4


def _heads_pipelined(logits, values, m_ref, acc_ref, qbias=None):
    ahead = [logits(h) for h in range(HEADS_AHEAD)]
    for h in range(N_HEADS):
        if h + HEADS_AHEAD < N_HEADS:
            ahead.append(logits(h + HEADS_AHEAD))
        _softmax_step(h, ahead.pop(0), values(h), m_ref, acc_ref,
                      None if qbias is None else qbias(h))


def _init_state(m_ref, acc_ref):
    m_ref[...] = jnp.full(m_ref.shape, NEG, F32)
    acc_ref[...] = jnp.zeros(acc_ref.shape, F32)


def _gated_pair(p, gate, acc_ref):
    halves = []
    for h in (2 * p, 2 * p + 1):
        a = acc_ref[h]
        halves.append(a[:HEAD_DIM] / a[HEAD_DIM:HEAD_DIM + 1])
    ot = jnp.concatenate(halves, axis=0)
    g = gate.astype(F32)
    return (ot.T * (g * jax.nn.sigmoid(g))).astype(BF16)


def _mix_a_kernel(sink_ref, q_ref, kp_ref, kc_ref, vp_ref, vc_ref, gate_ref, tbl_ref, o_ref, acc_ref):
    n = pl.program_id(1)
    pad_pen = jnp.where(n > 0, 0.0, NEG).astype(F32)
    den_rows = lax.broadcasted_iota(jnp.int32, (ACC_ROWS, WINDOW), 0) >= HEAD_DIM

    def logits(h):
        qt = _head_qt(q_ref, h)
        ksl = slice((h // KV_GROUP) * LANES, (h // KV_GROUP + 1) * LANES)
        return (_dot(kp_ref[:, ksl], qt) + tbl_ref[h, :WINDOW, :] + pad_pen,
                _dot(kc_ref[:, ksl], qt) + tbl_ref[h, WINDOW:, :])

    all_logits = [logits(h) for h in range(N_HEADS)]
    for h in range(N_HEADS):
        kv = h // KV_GROUP
        vsl = slice(kv * HEAD_DIM, (kv + 1) * HEAD_DIM)
        s_prev, s_cur = all_logits[h]
        sink = sink_ref[h] * LOG2E
        m = jnp.maximum(jnp.maximum(jnp.max(s_prev, axis=0, keepdims=True),
                                    jnp.max(s_cur, axis=0, keepdims=True)), sink)
        p_prev = jnp.exp2((s_prev - m).astype(BF16))
        p_cur = jnp.exp2((s_cur - m).astype(BF16))
        acc = _dot(_with_ones(vp_ref[vsl, :]), p_prev) + _dot(_with_ones(vc_ref[vsl, :]), p_cur)
        acc_ref[h] = acc + jnp.where(den_rows, jnp.exp2(sink - m), 0.0)
    for p in range(N_PAIRS):
        sl = slice(p * LANES, (p + 1) * LANES)
        o_ref[:, sl] = _gated_pair(p, gate_ref[:, sl], acc_ref)


def _mix_a(qa, ka, z, vt, tbl, sinks, batch, seq):
    n = qa.shape[0]
    t = WINDOW
    nb = seq // t
    return pl.pallas_call(
        _mix_a_kernel,
        grid=(batch, nb),
        in_specs=[pl.BlockSpec(memory_space=pltpu.SMEM),
                  pl.BlockSpec((t, 512), lambda b, i: (b * nb + i, 0)),
                  pl.BlockSpec((t, 256), lambda b, i: (b * nb + jnp.maximum(i - 1, 0), 0)),
                  pl.BlockSpec((t, 256), lambda b, i: (b * nb + i, 0)),
                  pl.BlockSpec((LANES, t), lambda b, i: (VT_A // LANES, b * nb + jnp.maximum(i - 1, 0))),
                  pl.BlockSpec((LANES, t), lambda b, i: (VT_A // LANES, b * nb + i)),
                  pl.BlockSpec((t, 512), lambda b, i: (b * nb + i, OFF_AG // 512)),
                  pl.BlockSpec((N_HEADS, 2 * t, t), lambda b, i: (0, 0, 0))],
        out_specs=pl.BlockSpec((t, 512), lambda b, i: (b * nb + i, 0)),
        out_shape=jax.ShapeDtypeStruct((n, 512), BF16),
        scratch_shapes=[pltpu.VMEM((N_HEADS, ACC_ROWS, t), F32)],
        compiler_params=_params(("arbitrary", "arbitrary"), 32),
        name="mix_a",
    )(sinks, qa, ka, ka, vt, vt, z, tbl)


SEL_TQ = 256
SEL_SC = 512
SEL_SUB = 64
SEL_MC = 256
SEL_FAST_ITERS = 24
SEL_MAX_ITERS = 600


def _sel_b_kernel(iq_ref, ik_ref, wt_ref, lmat_ref, o_ref, sc_ref, *, topk):
    n = pl.program_id(1)
    t0 = n * SEL_TQ
    n_full = n // (SEL_SC // SEL_TQ)
    n_sc = n_full + 1
    n_mc = n + 1
    kf = float(topk)

    iqt = iq_ref[...].astype(F32).T
    row = lax.broadcasted_iota(jnp.int32, iqt.shape, 0)
    qst = jnp.concatenate(
        [jnp.where((row >= h * IDX_DIM) & (row < (h + 1) * IDX_DIM), iqt, 0.0)
         for h in range(IDX_HEADS)], axis=1).astype(BF16)
    wscale = (IDX_HEADS ** -0.5) * (IDX_DIM ** -0.5)
    w = [wt_ref[h:h + 1, :] * wscale for h in range(IDX_HEADS)]
    qpos = t0 + lax.broadcasted_iota(jnp.int32, (SEL_SUB, SEL_TQ), 1)
    kiota = lax.broadcasted_iota(jnp.int32, (SEL_SUB, SEL_TQ), 0)

    def score_chunk(c, carry, diagonal):
        rmax, rmin, ge0, gt0, minpos = carry
        base = pl.multiple_of(c * SEL_SC, SEL_SC)
        raw = _dot(ik_ref[pl.ds(base, SEL_SC), :], qst)
        for u in range(SEL_SC // SEL_SUB):
            rows = slice(u * SEL_SUB, (u + 1) * SEL_SUB)
            score = w[0] * jnp.maximum(raw[rows, 0:SEL_TQ], 0.0)
            for h in range(1, IDX_HEADS):
                score = score + w[h] * jnp.maximum(raw[rows, h * SEL_TQ:(h + 1) * SEL_TQ], 0.0)
            rmin = jnp.minimum(rmin, score)
            if diagonal:
                score = jnp.where((kiota + (base + u * SEL_SUB)) <= qpos, score, -jnp.inf)
            sc_ref[pl.ds(base + u * SEL_SUB, SEL_SUB), :] = score
            rmax = jnp.maximum(rmax, score)
            pos = score > 0.0
            ge0 = ge0 + jnp.where(score >= 0.0, 1.0, 0.0)
            gt0 = gt0 + jnp.where(pos, 1.0, 0.0)
            minpos = jnp.minimum(minpos, jnp.where(pos, score, jnp.inf))
        return rmax, rmin, ge0, gt0, minpos

    def slab(v):
        return jnp.full((SEL_SUB, SEL_TQ), v, F32)

    carry = lax.fori_loop(0, n_full, lambda c, cr: score_chunk(c, cr, False),
                          (slab(-jnp.inf), slab(jnp.inf), slab(0.0), slab(0.0), slab(jnp.inf)))
    rmax, rmin, ge0, gt0, minpos = score_chunk(n_full, carry, True)
    rmax = jnp.max(rmax, axis=0, keepdims=True)
    rmin = jnp.min(rmin, axis=0, keepdims=True)
    ge0 = jnp.sum(ge0, axis=0, keepdims=True)
    gt0 = jnp.sum(gt0, axis=0, keepdims=True)
    minpos = jnp.min(minpos, axis=0, keepdims=True)

    navail = (t0 + 1 + lax.broadcasted_iota(jnp.int32, (1, SEL_TQ), 1)).astype(F32)

    def sweep(mid, snap):
        midb = jnp.broadcast_to(mid, (SEL_SUB, SEL_TQ))

        def body(c, carry):
            base = pl.multiple_of(c * SEL_SC, SEL_SC)
            cnt, up, dn = carry
            for u in range(SEL_SC // SEL_SUB):
                x = sc_ref[pl.ds(base + u * SEL_SUB, SEL_SUB), :]
                ge = x >= midb
                cnt = cnt + jnp.where(ge, 1.0, 0.0)
                if snap:
                    up = jnp.minimum(up, jnp.where(ge, x, jnp.inf))
                    dn = jnp.maximum(dn, jnp.where(ge, -jnp.inf, x))
            return cnt, up, dn

        init = (jnp.zeros((SEL_SUB, SEL_TQ), F32), jnp.full((SEL_SUB, SEL_TQ), jnp.inf, F32),
                jnp.full((SEL_SUB, SEL_TQ), -jnp.inf, F32))
        cnt, up, dn = lax.fori_loop(0, n_sc, body, init)
        return (jnp.sum(cnt, axis=0, keepdims=True), jnp.min(up, axis=0, keepdims=True),
                jnp.max(dn, axis=0, keepdims=True))

    def advance(st, mid, snap):
        it, lo, hi, hi_dn, c_lo, c_hi, done = st
        cnt, up, dn = sweep(mid, snap)
        live = jnp.where(done > 0.5, 0.0, jnp.where(mid <= lo, 0.0, jnp.where(mid >= hi, 0.0, 1.0)))
        ge = jnp.where(cnt >= kf, live, 0.0) > 0.5
        lt = jnp.where(cnt >= kf, 0.0, live) > 0.5
        lo = jnp.where(ge, up if snap else mid, lo)
        c_lo = jnp.where(ge, cnt, c_lo)
        hi = jnp.where(lt, mid, hi)
        hi_dn = jnp.where(lt, dn if snap else jnp.inf, hi_dn)
        c_hi = jnp.where(lt, cnt, c_hi)
        done = jnp.where(live < 0.5, 1.0, jnp.where(cnt == kf, 1.0, jnp.where(hi_dn <= lo, 1.0, 0.0)))
        return it + 1, lo, hi, hi_dn, c_lo, c_hi, done

    def make_step(snap, repeat=1):
        def step(st):
            for _ in range(repeat):
                st = advance(st, 0.5 * st[1] + 0.5 * st[2], snap)
            return st
        return step

    def make_cond(limit):
        def cond(st):
            return jnp.logical_and(st[0] < limit, jnp.min(st[6]) < 0.5)
        return cond

    inf_row = jnp.full((1, SEL_TQ), jnp.inf, F32)
    all_in = navail <= kf
    above = jnp.logical_and(jnp.logical_not(all_in), gt0 >= kf)
    at_zero = jnp.logical_and(jnp.logical_not(all_in), jnp.logical_and(gt0 < kf, ge0 >= kf))
    below = jnp.logical_and(jnp.logical_not(all_in), ge0 < kf)
    lo = jnp.where(above, minpos, jnp.where(at_zero, 0.0, rmin))
    c_lo = jnp.where(above, gt0, jnp.where(at_zero, ge0, navail))
    hi = jnp.where(at_zero, minpos, jnp.where(below, 0.0, inf_row))
    c_hi = jnp.where(at_zero, gt0, jnp.where(below, ge0, 0.0))
    hi_dn = jnp.where(above, rmax, inf_row)
    done = jnp.where(all_in, 1.0, jnp.where(at_zero, 1.0, 0.0))
    st = (jnp.int32(0), lo, hi, hi_dn, c_lo, c_hi, done)
    st = advance(st, jnp.where(above, rmax, 0.5 * lo + 0.5 * hi), False)
    st = lax.while_loop(make_cond(SEL_FAST_ITERS), make_step(False, repeat=2), st)
    st = lax.while_loop(make_cond(SEL_MAX_ITERS), make_step(True), st)
    _, lo, hi, _, c_lo, c_hi, _ = st
    need = kf - c_hi
    ties = jnp.max(jnp.where(navail > kf, c_lo, kf)) > kf

    def fill_body(c, carry):
        base = pl.multiple_of(c * SEL_MC, SEL_MC)
        o_ref[0, pl.ds(base, SEL_MC), :] = jnp.full((SEL_MC, SEL_TQ), NEG, o_ref.dtype)
        return carry
    lax.fori_loop(n_mc, o_ref.shape[1] // SEL_MC, fill_body, 0)

    @pl.when(jnp.logical_not(ties))
    def _():
        def mask_body(c, carry):
            base = pl.multiple_of(c * SEL_MC, SEL_MC)
            x = sc_ref[pl.ds(base, SEL_MC), :]
            o_ref[0, pl.ds(base, SEL_MC), :] = jnp.where(x >= lo, 0.0, NEG).astype(o_ref.dtype)
            return carry
        lax.fori_loop(0, n_mc, mask_body, 0)

    @pl.when(ties)
    def _():
        lmat = lmat_ref[...]

        def mask_body(c, run):
            base = pl.multiple_of(c * SEL_MC, SEL_MC)
            x = sc_ref[pl.ds(base, SEL_MC), :]
            ge_lo = x >= lo
            lt_hi = x < hi
            bnd = jnp.where(ge_lo, jnp.where(lt_hi, 1.0, 0.0), 0.0)
            incl = _dot(lmat, bnd.astype(BF16)) + run
            inside = jnp.where(lt_hi, jnp.where(incl <= need, 0.0, NEG), 0.0)
            o_ref[0, pl.ds(base, SEL_MC), :] = jnp.where(ge_lo, inside, NEG).astype(o_ref.dtype)
            return incl[SEL_MC - 1:SEL_MC, :]
        lax.fori_loop(0, n_mc, mask_body, jnp.zeros((1, SEL_TQ), F32))


def _sel_b(z, st, lmat, batch, seq):
    nb = seq // SEL_TQ
    topk = min(TOPK_MAX, seq // 4)
    return pl.pallas_call(
        functools.partial(_sel_b_kernel, topk=topk),
        grid=(batch, nb),
        in_specs=[pl.BlockSpec((SEL_TQ, LANES), lambda b, i: (b * nb + i, OFF_BIQ // LANES)),
                  pl.BlockSpec((seq, LANES), lambda b, i: (b, OFF_BIK // LANES)),
                  pl.BlockSpec((8, SEL_TQ), lambda b, i: (0, b * nb + i)),
                  pl.BlockSpec((SEL_MC, SEL_MC), lambda b, i: (0, 0))],
        out_specs=pl.BlockSpec((1, seq, SEL_TQ), lambda b, i: (b, 0, i)),
        out_shape=jax.ShapeDtypeStruct((batch, seq, seq), BF16),
        scratch_shapes=[pltpu.VMEM((seq, SEL_TQ), F32)],
        compiler_params=_params(("arbitrary", "arbitrary"), 48),
        name="sel_b",
    )(z, z, st, lmat)


ATT_T = 256
B_NEAR = 3
FAR_TILES = 2


def _far_remainder(n_tiles, tile):
    done = (n_tiles // FAR_TILES) * FAR_TILES
    rem = n_tiles - done
    size = FAR_TILES // 2
    while size >= 1:
        @pl.when((rem // size) % 2 == 1)
        def _(size=size, start=done + (rem // (2 * size)) * (2 * size)):
            tile(start, size * ATT_T)
        size //= 2


def _mix_b_kernel(q_ref, k_ref, vt_ref, mask_ref, gate_ref, tbl_ref, o_ref, m_ref, acc_ref):
    i = pl.program_id(1)
    _init_state(m_ref, acc_ref)
    qs = [_head_qt(q_ref, h) for h in range(N_HEADS)]

    def tile(j, size, near):
        base = pl.multiple_of(j * ATT_T, ATT_T)
        madd = mask_ref[0, pl.ds(base, size), :].astype(F32)

        def logits(h):
            kv = h // KV_GROUP
            s = _dot(k_ref[pl.ds(base, size), kv * LANES:(kv + 1) * LANES], qs[h]) + madd
            return s if near is None else s + tbl_ref[h, near]

        def values(h):
            kv = h // KV_GROUP
            return vt_ref[kv * HEAD_DIM:(kv + 1) * HEAD_DIM, pl.ds(base, size)]

        _heads_pipelined(logits, values, m_ref, acc_ref)

    def far_body(j, carry):
        tile(j * FAR_TILES, FAR_TILES * ATT_T, None)
        return carry

    n_far = jnp.maximum(i - (B_NEAR - 1), 0)
    lax.fori_loop(0, n_far // FAR_TILES, far_body, 0)
    _far_remainder(n_far, lambda j, size: tile(j, size, None))

    for d in range(B_NEAR - 1, -1, -1):
        @pl.when(i >= d)
        def _(d=d):
            tile(i - d, ATT_T, d)
    for p in range(N_PAIRS):
        sl = slice(p * LANES, (p + 1) * LANES)
        o_ref[:, sl] = _gated_pair(p, gate_ref[:, sl], acc_ref)


def _mix_b(qb, kb, z, vt, maskadd, tbl, batch, seq):
    n = qb.shape[0]
    t = ATT_T
    nq = seq // t
    return pl.pallas_call(
        _mix_b_kernel,
        grid=(batch, nq),
        in_specs=[pl.BlockSpec((t, 512), lambda b, i: (b * nq + i, 0)),
                  pl.BlockSpec((seq, 256), lambda b, i: (b, 0)),
                  pl.BlockSpec((LANES, seq), lambda b, i: (VT_B // LANES, b)),
                  pl.BlockSpec((1, seq, t), lambda b, i: (b, 0, i)),
                  pl.BlockSpec((t, 512), lambda b, i: (b * nq + i, OFF_BG // 512)),
                  pl.BlockSpec((N_HEADS, B_NEAR, t, t), lambda b, i: (0, 0, 0, 0))],
        out_specs=pl.BlockSpec((t, 512), lambda b, i: (b * nq + i, 0)),
        out_shape=jax.ShapeDtypeStruct((n, 512), BF16),
        scratch_shapes=[pltpu.VMEM((N_HEADS, 8, t), F32), pltpu.VMEM((N_HEADS, ACC_ROWS, t), F32)],
        compiler_params=_params(("arbitrary", "arbitrary"), 56),
        name="mix_b",
    )(qb, kb, vt, maskadd, z, tbl)


def _mix_c_kernel(q_ref, k_ref, vt_ref, ck_ref, cq_ref, gate_ref, o_ref, m_ref, acc_ref):
    i = pl.program_id(1)
    _init_state(m_ref, acc_ref)
    qs = [_head_qt(q_ref, h) for h in range(N_HEADS)]
    q0 = pl.multiple_of(i * ATT_T, ATT_T)
    cq = cq_ref[0, :, pl.ds(q0, ATT_T)]
    krow = lax.broadcasted_iota(jnp.int32, (ATT_T, ATT_T), 0)
    qcol = lax.broadcasted_iota(jnp.int32, (ATT_T, ATT_T), 1)

    def tile(j, size, diag):
        base = pl.multiple_of(j * ATT_T, ATT_T)
        ck_all = ck_ref[pl.ds(base, size), :]

        def logits(h):
            p = h // 2
            ck = ck_all[:, MISC_CF + h:MISC_CF + h + 1]
            s = _dot(k_ref[pl.ds(base, size), p * LANES:(p + 1) * LANES], qs[h]) - ck
            return jnp.where(krow <= qcol, s, NEG) if diag else s

        def values(h):
            return vt_ref[h * HEAD_DIM:(h + 1) * HEAD_DIM, pl.ds(base, size)]

        _heads_pipelined(logits, values, m_ref, acc_ref, qbias=lambda h: cq[h:h + 1, :])

    def body(j, carry):
        tile(j * FAR_TILES, FAR_TILES * ATT_T, False)
        return carry

    lax.fori_loop(0, i // FAR_TILES, body, 0)
    _far_remainder(i, lambda j, size: tile(j, size, False))
    tile(i, ATT_T, True)
    for p in range(N_PAIRS):
        sl = slice(p * LANES, (p + 1) * LANES)
        o_ref[:, sl] = _gated_pair(p, gate_ref[:, sl], acc_ref)


def _mix_c(qc, kc, z, vt, cum, cum_t, batch, seq):
    n = qc.shape[0]
    t = ATT_T
    nq = seq // t
    return pl.pallas_call(
        _mix_c_kernel,
        grid=(batch, nq),
        in_specs=[pl.BlockSpec((t, 512), lambda b, i: (b * nq + i, 0)),
                  pl.BlockSpec((seq, 512), lambda b, i: (b, 0)),
                  pl.BlockSpec((512, seq), lambda b, i: (VT_C // 512, b)),
                  pl.BlockSpec((seq, LANES), lambda b, i: (b, 0)),
                  pl.BlockSpec((1, N_HEADS, seq), lambda b, i: (b, 0, 0)),
                  pl.BlockSpec((t, 512), lambda b, i: (b * nq + i, OFF_CG // 512))],
        out_specs=pl.BlockSpec((t, 512), lambda b, i: (b * nq + i, 0)),
        out_shape=jax.ShapeDtypeStruct((n, 512), BF16),
        scratch_shapes=[pltpu.VMEM((N_HEADS, 8, t), F32), pltpu.VMEM((N_HEADS, ACC_ROWS, t), F32)],
        compiler_params=_params(("arbitrary", "arbitrary"), 56),
        name="mix_c",
    )(qc, kc, vt, cum, cum_t, z)


def _merge_kernel(oa, ob, oc, ga, gb, gc, x_ref, wb_ref, wo_ref, o_ref):
    merged = None
    for br, (o, g) in enumerate(((oa, ga), (ob, gb), (oc, gc))):
        y = _dot(o[...], wb_ref[br])
        term = jax.nn.sigmoid(g[...].astype(F32)) * y
        merged = term if merged is None else merged + term
    o_ref[...] = x_ref[...] + _dot(merged.astype(BF16), wo_ref[...])


def _merge(oa, ob, oc, z, x2d, wb, wo):
    n, d = x2d.shape
    tm = 512

    def ospec():
        return pl.BlockSpec((tm, 512), lambda i: (i, 0))

    def gspec(k):
        return pl.BlockSpec((tm, d), lambda i, _k=k: (i, OFF_MERGE // d + _k))

    return pl.pallas_call(
        _merge_kernel,
        grid=(n // tm,),
        in_specs=[ospec(), ospec(), ospec(), gspec(0), gspec(1), gspec(2),
                  pl.BlockSpec((tm, d), lambda i: (i, 0)),
                  pl.BlockSpec((3, 512, d), lambda i: (0, 0, 0)),
                  pl.BlockSpec((d, d), lambda i: (0, 0))],
        out_specs=pl.BlockSpec((tm, d), lambda i: (i, 0)),
        out_shape=jax.ShapeDtypeStruct((n, d), F32),
        compiler_params=_params(("arbitrary",), 48),
        name="merge",
    )(oa, ob, oc, z, z, z, x2d, wb, wo)


def _bucket_lookup(table, bucket):
    out = jnp.zeros(bucket.shape + (table.shape[1],), F32)
    for b in range(N_BUCKETS):
        out = jnp.where((bucket == b)[..., None], table[b].astype(F32), out)
    return out


def _window_table(bias_a):
    ki = jnp.arange(2 * WINDOW)[:, None]
    qi = jnp.arange(WINDOW)[None, :]
    delta = qi + WINDOW - ki
    band = (delta >= 0) & (delta < WINDOW)
    bias = _bucket_lookup(bias_a, _t5_bucket(delta)).transpose(2, 0, 1) * LOG2E
    return jnp.where(band[None], bias, NEG)


def _near_table(bias_b):
    ki = jnp.arange(ATT_T)[:, None]
    qi = jnp.arange(ATT_T)[None, :]
    d = jnp.arange(B_NEAR)[:, None, None]
    delta = d * ATT_T + qi[None] - ki[None]
    bias = _bucket_lookup(bias_b, _t5_bucket(delta))
    far = bias_b[N_BUCKETS - 1].astype(F32)
    return ((bias - far) * LOG2E).transpose(3, 0, 1, 2)


def kernel(x, norm_gain, w_in, b_forget, qk_gain, sinks, w_branch, w_out, rel_bias):
    batch, seq, d = x.shape
    depth = norm_gain.shape[0]
    assert w_in.shape[-1] == IN_COLS and seq % 1024 == 0 and d == 1024
    assert (B_NEAR - 1) * ATT_T - (ATT_T - 1) < T5_FAR <= B_NEAR * ATT_T - (ATT_T - 1)

    w_perm = _gather_cols(w_in, _COLS).astype(BF16)
    w_vt = jnp.swapaxes(_gather_cols(w_in, _VT_COLS), 1, 2).astype(BF16)
    w_st = jnp.swapaxes(_gather_cols(w_in, _ST_COLS), 1, 2).astype(BF16)
    wb = w_branch.astype(BF16)
    wo = w_out.astype(BF16)

    gmat = jnp.asarray(np.kron(np.eye(N_HEADS), np.ones((HEAD_DIM, HEAD_DIM))), BF16)
    prep_t = 512
    lmat = jnp.asarray(np.tril(np.ones((prep_t, prep_t))), BF16)
    lmat_sel = lmat[:SEL_MC, :SEL_MC]
    tbl_a = _window_table(rel_bias[:, :N_HEADS])
    tbl_b = _near_table(rel_bias[:, N_HEADS:])

    x2d = x.reshape(batch * seq, d)
    for layer in range(depth):
        g = qk_gain[layer]
        qscale = ATTN_SCALE * LOG2E
        gains = jnp.zeros((8, 512), F32)
        gains = gains.at[0].set(jnp.tile(g[0, 0], N_HEADS) * qscale)
        gains = gains.at[1, :256].set(jnp.tile(g[0, 1], 4))
        gains = gains.at[2].set(jnp.tile(g[1, 0], N_HEADS) * qscale)
        gains = gains.at[3, :256].set(jnp.tile(g[1, 1], 4))
        gains = gains.at[4].set(jnp.tile(g[2, 0], N_HEADS) * qscale)
        gains = gains.at[5].set(jnp.tile(g[2, 1], N_HEADS))
        gains = gains.at[6, MISC_CF:MISC_CF + N_HEADS].set(b_forget[layer])

        z, misc, vt, st = _proj(x2d, norm_gain[layer][None, :], w_perm[layer], w_vt[layer], w_st[layer])
        qa, ka, qb, kb, qc, kc, cum = _prep(z, misc, gains, gmat, lmat, batch, seq)
        cum_t = cum.reshape(batch, seq, LANES)[:, :, MISC_CF:MISC_CF + N_HEADS].transpose(0, 2, 1)
        oa = _mix_a(qa, ka, z, vt, tbl_a, sinks[layer], batch, seq)
        maskadd = _sel_b(z, st, lmat_sel, batch, seq)
        ob = _mix_b(qb, kb, z, vt, maskadd, tbl_b, batch, seq)
        oc = _mix_c(qc, kc, z, vt, cum, cum_t, batch, seq)
        x2d = _merge(oa, ob, oc, z, x2d, wb[layer], wo[layer])
    return x2d.reshape(batch, seq, d)
```

```python
import functools
import math

import numpy as np
import jax
import jax.numpy as jnp
from jax import lax
from jax.experimental import pallas as pl
from jax.experimental.pallas import tpu as pltpu

F32 = jnp.float32
BF16 = jnp.bfloat16

HEAD_DIM = 64
LANES = 128
N_HEADS = 8
N_PAIRS = N_HEADS // 2
KV_GROUP = 4
WINDOW = 128
IDX_HEADS = 4
IDX_DIM = 32
TOPK_MAX = 256
N_BUCKETS = 32
MAX_DISTANCE = 512
EPS = 1e-6
ATTN_SCALE = HEAD_DIM ** -0.5
LOG2E = math.log2(math.e)
NEG = -1e30
T5_FAR = 413

_SIZES = dict(a_q=512, a_k=128, a_v=128, a_gate=512,
              b_q=512, b_k=128, b_v=128, b_iq=128, b_ik=32, b_iw=4, b_gate=512,
              c_q=512, c_k=512, c_v=512, c_f=8, c_gate=512, merge=3072)
_ORIG = {}
_o = 0
for _k, _v in _SIZES.items():
    _ORIG[_k] = _o
    _o += _v
IN_COLS = _o

OFF_MERGE = 0
OFF_AQ, OFF_AG, OFF_BQ, OFF_BG = 3072, 3584, 4096, 4608
OFF_CQ, OFF_CK, OFF_CG = 5120, 5632, 6144
OFF_AK, OFF_BK = 6656, 6912
OFF_BIQ, OFF_BIK, OFF_MISC = 7168, 7296, 7424
NP_COLS = 7680
PROJ_TN = 1536
MISC_CF = 8
VT_C, VT_A, VT_B, VT_ROWS = 0, 512, 640, 768


def _orig(name, idx=None):
    src = np.arange(_SIZES[name]) + _ORIG[name]
    return src if idx is None else src[idx]


def _column_map():
    cols = np.full((NP_COLS,), -1, np.int64)

    def put(off, src):
        cols[off:off + len(src)] = src

    for off, name in ((OFF_MERGE, "merge"), (OFF_AQ, "a_q"), (OFF_AG, "a_gate"), (OFF_BQ, "b_q"),
                      (OFF_BG, "b_gate"), (OFF_CQ, "c_q"), (OFF_CK, "c_k"), (OFF_CG, "c_gate"),
                      (OFF_BIQ, "b_iq")):
        put(off, _orig(name))
    dup = np.concatenate([np.arange(64), np.arange(64), np.arange(64, 128), np.arange(64, 128)])
    put(OFF_AK, _orig("a_k", dup))
    put(OFF_BK, _orig("b_k", dup))
    put(OFF_BIK, _orig("b_ik", np.tile(np.arange(IDX_DIM), IDX_HEADS)))
    put(OFF_MISC + MISC_CF, _orig("c_f"))
    return cols


def _gather_cols(w, cols):
    parts = []
    start = 0
    while start < len(cols):
        end = start + 1
        if cols[start] < 0:
            while end < len(cols) and cols[end] < 0:
                end += 1
            parts.append(jnp.zeros(w.shape[:-1] + (end - start,), w.dtype))
        else:
            while end < len(cols) and cols[end] == cols[end - 1] + 1:
                end += 1
            parts.append(w[..., int(cols[start]):int(cols[end - 1]) + 1])
        start = end
    return jnp.concatenate(parts, axis=-1)


_COLS = _column_map()
_VT_COLS = np.concatenate([_orig("c_v"), _orig("a_v"), _orig("b_v")])
_ST_COLS = np.concatenate([_orig("b_iw"), np.full((4,), -1, np.int64)])


def _params(sem, vmem_mb):
    return pltpu.CompilerParams(dimension_semantics=sem, vmem_limit_bytes=vmem_mb * 1024 * 1024)


def _dot(a, b):
    return jnp.dot(a, b, preferred_element_type=F32)


def _dot_nt(a, b):
    return lax.dot_general(a, b, (((1,), (1,)), ((), ())), preferred_element_type=F32)


def _t5_bucket(delta):
    n = jnp.maximum(delta, 0)
    max_exact = N_BUCKETS // 2
    nf = jnp.maximum(n, 1).astype(F32)
    large = max_exact + (jnp.log(nf / max_exact) / math.log(MAX_DISTANCE / max_exact)
                         * (N_BUCKETS - max_exact)).astype(jnp.int32)
    large = jnp.minimum(large, N_BUCKETS - 1)
    return jnp.where(n < max_exact, n, large)


def _proj_kernel(x_ref, g_ref, w_ref, wvt_ref, wst_ref, z_ref, misc_ref, vt_ref, st_ref, h_ref,
                 *, misc_tile, misc_local):
    j = pl.program_id(1)

    @pl.when(j == 0)
    def _():
        x = x_ref[...]
        ms = jnp.mean(x * x, axis=-1, keepdims=True)
        h = (x * lax.rsqrt(ms + EPS) * g_ref[...]).astype(BF16)
        h_ref[...] = h
        vt_ref[...] = _dot_nt(wvt_ref[...], h).astype(BF16)
        st_ref[...] = _dot_nt(wst_ref[...], h)

    acc = _dot(h_ref[...], w_ref[...])
    z_ref[...] = acc.astype(BF16)

    @pl.when(j == misc_tile)
    def _():
        misc_ref[...] = acc[:, misc_local:misc_local + LANES]


def _proj(x2d, gain, w, wvt, wst):
    n, d = x2d.shape
    tm = 1024
    return pl.pallas_call(
        functools.partial(_proj_kernel, misc_tile=OFF_MISC // PROJ_TN, misc_local=OFF_MISC % PROJ_TN),
        grid=(n // tm, NP_COLS // PROJ_TN),
        in_specs=[pl.BlockSpec((tm, d), lambda i, j: (i, 0)),
                  pl.BlockSpec((1, d), lambda i, j: (0, 0)),
                  pl.BlockSpec((d, PROJ_TN), lambda i, j: (0, j)),
                  pl.BlockSpec((VT_ROWS, d), lambda i, j: (0, 0)),
                  pl.BlockSpec((8, d), lambda i, j: (0, 0))],
        out_specs=[pl.BlockSpec((tm, PROJ_TN), lambda i, j: (i, j)),
                   pl.BlockSpec((tm, LANES), lambda i, j: (i, 0)),
                   pl.BlockSpec((VT_ROWS, tm), lambda i, j: (0, i)),
                   pl.BlockSpec((8, tm), lambda i, j: (0, i))],
        out_shape=[jax.ShapeDtypeStruct((n, NP_COLS), BF16),
                   jax.ShapeDtypeStruct((n, LANES), F32),
                   jax.ShapeDtypeStruct((VT_ROWS, n), BF16),
                   jax.ShapeDtypeStruct((8, n), F32)],
        scratch_shapes=[pltpu.VMEM((tm, d), BF16)],
        compiler_params=_params(("arbitrary", "arbitrary"), 56),
        name="proj",
    )(x2d, gain, w, wvt, wst)


def _group_rms(x_bf16, g_ref, gain_row):
    x = x_bf16.astype(F32)
    width = x.shape[-1]
    ss = _dot((x * x).astype(BF16), g_ref[:width, :width])
    return (x * lax.rsqrt(ss * (1.0 / HEAD_DIM) + EPS) * gain_row).astype(BF16)


def _prep_kernel(aq, ak, bq, bk, cq, ck, misc, gains, gmat, lmat,
                 qa_o, ka_o, qb_o, kb_o, qc_o, kc_o, cum_o, carry):
    t = pl.program_id(1)
    qa_o[...] = _group_rms(aq[...], gmat, gains[0:1, :])
    ka_o[...] = _group_rms(ak[...], gmat, gains[1:2, :256])
    qb_o[...] = _group_rms(bq[...], gmat, gains[2:3, :])
    kb_o[...] = _group_rms(bk[...], gmat, gains[3:4, :256])
    qc_o[...] = _group_rms(cq[...], gmat, gains[4:5, :])
    kc_o[...] = _group_rms(ck[...], gmat, gains[5:6, :])

    @pl.when(t == 0)
    def _():
        carry[...] = jnp.zeros_like(carry)

    xm = misc[...] + gains[6:7, :LANES]
    logf = jnp.minimum(xm, 0.0) - jnp.log(1.0 + jnp.exp(-jnp.abs(xm)))
    hi = logf.astype(BF16)
    r1 = logf - hi.astype(F32)
    mid = r1.astype(BF16)
    lo = (r1 - mid.astype(F32)).astype(BF16)
    lm = lmat[...]
    c = _dot(lm, hi) + _dot(lm, mid) + _dot(lm, lo) + carry[...]
    cum_o[...] = c * LOG2E
    rows = c.shape[0]
    carry[...] = c[rows - 1:rows, :]


def _prep(z, misc, gains, gmat, lmat, batch, seq):
    n = z.shape[0]
    tp = lmat.shape[0]
    nt = seq // tp

    def zspec(width, off):
        return pl.BlockSpec((tp, width), lambda b, t, _c=off // width: (b * nt + t, _c))

    def ospec(width):
        return pl.BlockSpec((tp, width), lambda b, t: (b * nt + t, 0))

    return pl.pallas_call(
        _prep_kernel,
        grid=(batch, nt),
        in_specs=[zspec(512, OFF_AQ), zspec(256, OFF_AK), zspec(512, OFF_BQ), zspec(256, OFF_BK),
                  zspec(512, OFF_CQ), zspec(512, OFF_CK), ospec(LANES),
                  pl.BlockSpec((8, 512), lambda b, t: (0, 0)),
                  pl.BlockSpec((512, 512), lambda b, t: (0, 0)),
                  pl.BlockSpec((tp, tp), lambda b, t: (0, 0))],
        out_specs=[ospec(512), ospec(256), ospec(512), ospec(256), ospec(512), ospec(512), ospec(LANES)],
        out_shape=[jax.ShapeDtypeStruct((n, 512), BF16), jax.ShapeDtypeStruct((n, 256), BF16),
                   jax.ShapeDtypeStruct((n, 512), BF16), jax.ShapeDtypeStruct((n, 256), BF16),
                   jax.ShapeDtypeStruct((n, 512), BF16), jax.ShapeDtypeStruct((n, 512), BF16),
                   jax.ShapeDtypeStruct((n, LANES), F32)],
        scratch_shapes=[pltpu.VMEM((1, LANES), F32)],
        compiler_params=_params(("arbitrary", "arbitrary"), 48),
        name="prep",
    )(z, z, z, z, z, z, misc, gains, gmat, lmat)


def _head_qt(q_ref, h):
    p = h // 2
    qt = q_ref[:, p * LANES:(p + 1) * LANES].astype(F32).T
    row = lax.broadcasted_iota(jnp.int32, qt.shape, 0)
    own = (row < HEAD_DIM) if h % 2 == 0 else (row >= HEAD_DIM)
    return jnp.where(own, qt, 0.0).astype(BF16)


DEN_ROWS = 16
ACC_ROWS = HEAD_DIM + DEN_ROWS


def _with_ones(vt):
    return jnp.concatenate([vt, jnp.ones((DEN_ROWS, vt.shape[1]), vt.dtype)], axis=0)


def _softmax_step(h, s, vt, m_ref, acc_ref, qbias=None):
    m_prev = m_ref[h]
    s_of = s if callable(s) else (lambda: s)
    s_max = jnp.max(s_of(), axis=0, keepdims=True)
    m_new = jnp.maximum(m_prev, s_max if qbias is None else s_max + qbias)
    alpha = jnp.exp2(m_prev - m_new)
    shift = m_new[0:1, :] if qbias is None else m_new[0:1, :] - qbias
    p = jnp.exp2((s_of() - shift).astype(BF16))
    acc_ref[h] = alpha[0:1, :] * acc_ref[h] + _dot(_with_ones(vt), p)
    m_ref[h] = m_new


HEADS_AHEAD = 4


def _heads_pipelined(logits, values, m_ref, acc_ref, qbias=None):
    ahead = [logits(h) for h in range(HEADS_AHEAD)]
    for h in range(N_HEADS):
        if h + HEADS_AHEAD < N_HEADS:
            ahead.append(logits(h + HEADS_AHEAD))
        _softmax_step(h, ahead.pop(0), values(h), m_ref, acc_ref,
                      None if qbias is None else qbias(h))


def _tiles_skewed(n_tiles, tile_logits, tile_values, m_ref, acc_ref, s_ref, qbias=None):
    bias = (lambda h: None) if qbias is None else qbias

    def produce(t, buf):
        logits = tile_logits(t)
        for h in range(N_HEADS):
            s_ref[buf, h] = logits(h)

    def consume(t, buf):
        values = tile_values(t)
        for h in range(N_HEADS):
            _softmax_step(h, lambda h=h: s_ref[buf, h], values(h), m_ref, acc_ref, bias(h))

    def both(t, buf):
        logits, values = tile_logits(t + 1), tile_values(t)
        s_ref[1 - buf, 0] = logits(0)
        for h in range(N_HEADS):
            if h + 1 < N_HEADS:
                s_ref[1 - buf, h + 1] = logits(h + 1)
            _softmax_step(h, lambda h=h: s_ref[buf, h], values(h), m_ref, acc_ref, bias(h))

    pairs = jnp.maximum(n_tiles - 1, 0) // 2
    rem = n_tiles - 2 * pairs

    @pl.when(n_tiles > 0)
    def _():
        produce(0, 0)

    def trip(p, carry):
        both(2 * p, 0)
        both(2 * p + 1, 1)
        return carry

    lax.fori_loop(0, pairs, trip, 0)

    @pl.when(rem == 2)
    def _():
        both(2 * pairs, 0)
        consume(2 * pairs + 1, 1)

    @pl.when(rem == 1)
    def _():
        consume(2 * pairs, 0)


def _init_state(m_ref, acc_ref):
    m_ref[...] = jnp.full(m_ref.shape, NEG, F32)
    acc_ref[...] = jnp.zeros(acc_ref.shape, F32)


def _gated_pair(p, gate, acc_ref):
    halves = []
    for h in (2 * p, 2 * p + 1):
        a = acc_ref[h]
        halves.append(a[:HEAD_DIM] / a[HEAD_DIM:HEAD_DIM + 1])
    ot = jnp.concatenate(halves, axis=0)
    g = gate.astype(F32)
    return (ot.T * (g * jax.nn.sigmoid(g))).astype(BF16)


def _mix_a_kernel(sink_ref, q_ref, kp_ref, kc_ref, vp_ref, vc_ref, gate_ref, tbl_ref, o_ref, acc_ref):
    n = pl.program_id(1)
    pad_pen = jnp.where(n > 0, 0.0, NEG).astype(F32)
    den_rows = lax.broadcasted_iota(jnp.int32, (ACC_ROWS, WINDOW), 0) >= HEAD_DIM

    def logits(h):
        qt = _head_qt(q_ref, h)
        ksl = slice((h // KV_GROUP) * LANES, (h // KV_GROUP + 1) * LANES)
        return (_dot(kp_ref[:, ksl], qt) + tbl_ref[h, :WINDOW, :] + pad_pen,
                _dot(kc_ref[:, ksl], qt) + tbl_ref[h, WINDOW:, :])

    all_logits = [logits(h) for h in range(N_HEADS)]
    for h in range(N_HEADS):
        kv = h // KV_GROUP
        vsl = slice(kv * HEAD_DIM, (kv + 1) * HEAD_DIM)
        s_prev, s_cur = all_logits[h]
        sink = sink_ref[h] * LOG2E
        m = jnp.maximum(jnp.maximum(jnp.max(s_prev, axis=0, keepdims=True),
                                    jnp.max(s_cur, axis=0, keepdims=True)), sink)
        p_prev = jnp.exp2((s_prev - m).astype(BF16))
        p_cur = jnp.exp2((s_cur - m).astype(BF16))
        acc = _dot(_with_ones(vp_ref[vsl, :]), p_prev) + _dot(_with_ones(vc_ref[vsl, :]), p_cur)
        acc_ref[h] = acc + jnp.where(den_rows, jnp.exp2(sink - m), 0.0)
    for p in range(N_PAIRS):
        sl = slice(p * LANES, (p + 1) * LANES)
        o_ref[:, sl] = _gated_pair(p, gate_ref[:, sl], acc_ref)


def _mix_a(qa, ka, z, vt, tbl, sinks, batch, seq):
    n = qa.shape[0]
    t = WINDOW
    nb = seq // t
    return pl.pallas_call(
        _mix_a_kernel,
        grid=(batch, nb),
        in_specs=[pl.BlockSpec(memory_space=pltpu.SMEM),
                  pl.BlockSpec((t, 512), lambda b, i: (b * nb + i, 0)),
                  pl.BlockSpec((t, 256), lambda b, i: (b * nb + jnp.maximum(i - 1, 0), 0)),
                  pl.BlockSpec((t, 256), lambda b, i: (b * nb + i, 0)),
                  pl.BlockSpec((LANES, t), lambda b, i: (VT_A // LANES, b * nb + jnp.maximum(i - 1, 0))),
                  pl.BlockSpec((LANES, t), lambda b, i: (VT_A // LANES, b * nb + i)),
                  pl.BlockSpec((t, 512), lambda b, i: (b * nb + i, OFF_AG // 512)),
                  pl.BlockSpec((N_HEADS, 2 * t, t), lambda b, i: (0, 0, 0))],
        out_specs=pl.BlockSpec((t, 512), lambda b, i: (b * nb + i, 0)),
        out_shape=jax.ShapeDtypeStruct((n, 512), BF16),
        scratch_shapes=[pltpu.VMEM((N_HEADS, ACC_ROWS, t), F32)],
        compiler_params=_params(("arbitrary", "arbitrary"), 32),
        name="mix_a",
    )(sinks, qa, ka, ka, vt, vt, z, tbl)


SEL_TQ = 256
SEL_SC = 512
SEL_SUB = 64
SEL_MC = 256
SEL_FAST_ITERS = 24
SEL_MAX_ITERS = 600


def _sel_b_kernel(iq_ref, ik_ref, wt_ref, lmat_ref, o_ref, sc_ref, *, topk):
    n = pl.program_id(1)
    t0 = n * SEL_TQ
    n_full = n // (SEL_SC // SEL_TQ)
    n_sc = n_full + 1
    n_mc = n + 1
    kf = float(topk)

    iqt = iq_ref[...].astype(F32).T
    row = lax.broadcasted_iota(jnp.int32, iqt.shape, 0)
    qst = jnp.concatenate(
        [jnp.where((row >= h * IDX_DIM) & (row < (h + 1) * IDX_DIM), iqt, 0.0)
         for h in range(IDX_HEADS)], axis=1).astype(BF16)
    wscale = (IDX_HEADS ** -0.5) * (IDX_DIM ** -0.5)
    w = [wt_ref[h:h + 1, :] * wscale for h in range(IDX_HEADS)]
    qpos = t0 + lax.broadcasted_iota(jnp.int32, (SEL_SUB, SEL_TQ), 1)
    kiota = lax.broadcasted_iota(jnp.int32, (SEL_SUB, SEL_TQ), 0)

    def score_chunk(c, carry, diagonal):
        rmax, rmin, ge0, gt0, minpos = carry
        base = pl.multiple_of(c * SEL_SC, SEL_SC)
        raw = _dot(ik_ref[pl.ds(base, SEL_SC), :], qst)
        for u in range(SEL_SC // SEL_SUB):
            rows = slice(u * SEL_SUB, (u + 1) * SEL_SUB)
            score = w[0] * jnp.maximum(raw[rows, 0:SEL_TQ], 0.0)
            for h in range(1, IDX_HEADS):
                score = score + w[h] * jnp.maximum(raw[rows, h * SEL_TQ:(h + 1) * SEL_TQ], 0.0)
            rmin = jnp.minimum(rmin, score)
            if diagonal:
                score = jnp.where((kiota + (base + u * SEL_SUB)) <= qpos, score, -jnp.inf)
            sc_ref[pl.ds(base + u * SEL_SUB, SEL_SUB), :] = score
            rmax = jnp.maximum(rmax, score)
            pos = score > 0.0
            ge0 = ge0 + jnp.where(score >= 0.0, 1.0, 0.0)
            gt0 = gt0 + jnp.where(pos, 1.0, 0.0)
            minpos = jnp.minimum(minpos, jnp.where(pos, score, jnp.inf))
        return rmax, rmin, ge0, gt0, minpos

    def slab(v):
        return jnp.full((SEL_SUB, SEL_TQ), v, F32)

    carry = lax.fori_loop(0, n_full, lambda c, cr: score_chunk(c, cr, False),
                          (slab(-jnp.inf), slab(jnp.inf), slab(0.0), slab(0.0), slab(jnp.inf)))
    rmax, rmin, ge0, gt0, minpos = score_chunk(n_full, carry, True)
    rmax = jnp.max(rmax, axis=0, keepdims=True)
    rmin = jnp.min(rmin, axis=0, keepdims=True)
    ge0 = jnp.sum(ge0, axis=0, keepdims=True)
    gt0 = jnp.sum(gt0, axis=0, keepdims=True)
    minpos = jnp.min(minpos, axis=0, keepdims=True)

    navail = (t0 + 1 + lax.broadcasted_iota(jnp.int32, (1, SEL_TQ), 1)).astype(F32)

    def sweep(mid, snap):
        midb = jnp.broadcast_to(mid, (SEL_SUB, SEL_TQ))

        def body(c, carry):
            base = pl.multiple_of(c * SEL_SC, SEL_SC)
            cnt, up, dn = carry
            for u in range(SEL_SC // SEL_SUB):
                x = sc_ref[pl.ds(base + u * SEL_SUB, SEL_SUB), :]
                ge = x >= midb
                cnt = cnt + jnp.where(ge, 1.0, 0.0)
                if snap:
                    up = jnp.minimum(up, jnp.where(ge, x, jnp.inf))
                    dn = jnp.maximum(dn, jnp.where(ge, -jnp.inf, x))
            return cnt, up, dn

        init = (jnp.zeros((SEL_SUB, SEL_TQ), F32), jnp.full((SEL_SUB, SEL_TQ), jnp.inf, F32),
                jnp.full((SEL_SUB, SEL_TQ), -jnp.inf, F32))
        cnt, up, dn = lax.fori_loop(0, n_sc, body, init)
        return (jnp.sum(cnt, axis=0, keepdims=True), jnp.min(up, axis=0, keepdims=True),
                jnp.max(dn, axis=0, keepdims=True))

    def advance(st, mid, snap):
        it, lo, hi, hi_dn, c_lo, c_hi, done = st
        cnt, up, dn = sweep(mid, snap)
        live = jnp.where(done > 0.5, 0.0, jnp.where(mid <= lo, 0.0, jnp.where(mid >= hi, 0.0, 1.0)))
        ge = jnp.where(cnt >= kf, live, 0.0) > 0.5
        lt = jnp.where(cnt >= kf, 0.0, live) > 0.5
        lo = jnp.where(ge, up if snap else mid, lo)
        c_lo = jnp.where(ge, cnt, c_lo)
        hi = jnp.where(lt, mid, hi)
        hi_dn = jnp.where(lt, dn if snap else jnp.inf, hi_dn)
        c_hi = jnp.where(lt, cnt, c_hi)
        done = jnp.where(live < 0.5, 1.0, jnp.where(cnt == kf, 1.0, jnp.where(hi_dn <= lo, 1.0, 0.0)))
        return it + 1, lo, hi, hi_dn, c_lo, c_hi, done

    def make_step(snap, repeat=1):
        def step(st):
            for _ in range(repeat):
                st = advance(st, 0.5 * st[1] + 0.5 * st[2], snap)
            return st
        return step

    def make_cond(limit):
        def cond(st):
            return jnp.logical_and(st[0] < limit, jnp.min(st[6]) < 0.5)
        return cond

    inf_row = jnp.full((1, SEL_TQ), jnp.inf, F32)
    all_in = navail <= kf
    above = jnp.logical_and(jnp.logical_not(all_in), gt0 >= kf)
    at_zero = jnp.logical_and(jnp.logical_not(all_in), jnp.logical_and(gt0 < kf, ge0 >= kf))
    below = jnp.logical_and(jnp.logical_not(all_in), ge0 < kf)
    lo = jnp.where(above, minpos, jnp.where(at_zero, 0.0, rmin))
    c_lo = jnp.where(above, gt0, jnp.where(at_zero, ge0, navail))
    hi = jnp.where(at_zero, minpos, jnp.where(below, 0.0, inf_row))
    c_hi = jnp.where(at_zero, gt0, jnp.where(below, ge0, 0.0))
    hi_dn = jnp.where(above, rmax, inf_row)
    done = jnp.where(all_in, 1.0, jnp.where(at_zero, 1.0, 0.0))
    st = (jnp.int32(0), lo, hi, hi_dn, c_lo, c_hi, done)
    st = advance(st, jnp.where(above, rmax, 0.5 * lo + 0.5 * hi), False)
    st = lax.while_loop(make_cond(SEL_FAST_ITERS), make_step(False, repeat=2), st)
    st = lax.while_loop(make_cond(SEL_MAX_ITERS), make_step(True), st)
    _, lo, hi, _, c_lo, c_hi, _ = st
    need = kf - c_hi
    ties = jnp.max(jnp.where(navail > kf, c_lo, kf)) > kf

    def fill_body(c, carry):
        base = pl.multiple_of(c * SEL_MC, SEL_MC)
        o_ref[0, pl.ds(base, SEL_MC), :] = jnp.full((SEL_MC, SEL_TQ), NEG, o_ref.dtype)
        return carry
    lax.fori_loop(n_mc, o_ref.shape[1] // SEL_MC, fill_body, 0)

    @pl.when(jnp.logical_not(ties))
    def _():
        def mask_body(c, carry):
            base = pl.multiple_of(c * SEL_MC, SEL_MC)
            x = sc_ref[pl.ds(base, SEL_MC), :]
            o_ref[0, pl.ds(base, SEL_MC), :] = jnp.where(x >= lo, 0.0, NEG).astype(o_ref.dtype)
            return carry
        lax.fori_loop(0, n_mc, mask_body, 0)

    @pl.when(ties)
    def _():
        lmat = lmat_ref[...]

        def mask_body(c, run):
            base = pl.multiple_of(c * SEL_MC, SEL_MC)
            x = sc_ref[pl.ds(base, SEL_MC), :]
            ge_lo = x >= lo
            lt_hi = x < hi
            bnd = jnp.where(ge_lo, jnp.where(lt_hi, 1.0, 0.0), 0.0)
            incl = _dot(lmat, bnd.astype(BF16)) + run
            inside = jnp.where(lt_hi, jnp.where(incl <= need, 0.0, NEG), 0.0)
            o_ref[0, pl.ds(base, SEL_MC), :] = jnp.where(ge_lo, inside, NEG).astype(o_ref.dtype)
            return incl[SEL_MC - 1:SEL_MC, :]
        lax.fori_loop(0, n_mc, mask_body, jnp.zeros((1, SEL_TQ), F32))


def _sel_b(z, st, lmat, batch, seq):
    nb = seq // SEL_TQ
    topk = min(TOPK_MAX, seq // 4)
    return pl.pallas_call(
        functools.partial(_sel_b_kernel, topk=topk),
        grid=(batch, nb),
        in_specs=[pl.BlockSpec((SEL_TQ, LANES), lambda b, i: (b * nb + i, OFF_BIQ // LANES)),
                  pl.BlockSpec((seq, LANES), lambda b, i: (b, OFF_BIK // LANES)),
                  pl.BlockSpec((8, SEL_TQ), lambda b, i: (0, b * nb + i)),
                  pl.BlockSpec((SEL_MC, SEL_MC), lambda b, i: (0, 0))],
        out_specs=pl.BlockSpec((1, seq, SEL_TQ), lambda b, i: (b, 0, i)),
        out_shape=jax.ShapeDtypeStruct((batch, seq, seq), BF16),
        scratch_shapes=[pltpu.VMEM((seq, SEL_TQ), F32)],
        compiler_params=_params(("arbitrary", "arbitrary"), 48),
        name="sel_b",
    )(z, z, st, lmat)


ATT_T = 256
B_NEAR = 3
FAR_TILES = 2


def _far_remainder(n_tiles, tile):
    done = (n_tiles // FAR_TILES) * FAR_TILES
    rem = n_tiles - done
    size = FAR_TILES // 2
    while size >= 1:
        @pl.when((rem // size) % 2 == 1)
        def _(size=size, start=done + (rem // (2 * size)) * (2 * size)):
            tile(start, size * ATT_T)
        size //= 2


def _mix_b_kernel(q_ref, k_ref, vt_ref, mask_ref, gate_ref, tbl_ref, o_ref, m_ref, acc_ref, s_ref):
    i = pl.program_id(1)
    _init_state(m_ref, acc_ref)
    qs = [_head_qt(q_ref, h) for h in range(N_HEADS)]

    def tile_logits(j, size, near=None):
        base = pl.multiple_of(j * ATT_T, ATT_T)
        madd = mask_ref[0, pl.ds(base, size), :].astype(F32)

        def logits(h):
            kv = h // KV_GROUP
            s = _dot(k_ref[pl.ds(base, size), kv * LANES:(kv + 1) * LANES], qs[h]) + madd
            return s if near is None else s + tbl_ref[h, near]
        return logits

    def tile_values(j, size):
        base = pl.multiple_of(j * ATT_T, ATT_T)

        def values(h):
            kv = h // KV_GROUP
            return vt_ref[kv * HEAD_DIM:(kv + 1) * HEAD_DIM, pl.ds(base, size)]
        return values

    def tile(j, size, near):
        _heads_pipelined(tile_logits(j, size, near), tile_values(j, size), m_ref, acc_ref)

    n_far = jnp.maximum(i - (B_NEAR - 1), 0)
    far_size = FAR_TILES * ATT_T
    _tiles_skewed(n_far // FAR_TILES, lambda t: tile_logits(t * FAR_TILES, far_size),
                  lambda t: tile_values(t * FAR_TILES, far_size), m_ref, acc_ref, s_ref)
    _far_remainder(n_far, lambda j, size: tile(j, size, None))

    for d in range(B_NEAR - 1, -1, -1):
        @pl.when(i >= d)
        def _(d=d):
            tile(i - d, ATT_T, d)
    for p in range(N_PAIRS):
        sl = slice(p * LANES, (p + 1) * LANES)
        o_ref[:, sl] = _gated_pair(p, gate_ref[:, sl], acc_ref)


def _mix_b(qb, kb, z, vt, maskadd, tbl, batch, seq):
    n = qb.shape[0]
    t = ATT_T
    nq = seq // t
    return pl.pallas_call(
        _mix_b_kernel,
        grid=(batch, nq),
        in_specs=[pl.BlockSpec((t, 512), lambda b, i: (b * nq + i, 0)),
                  pl.BlockSpec((seq, 256), lambda b, i: (b, 0), pipeline_mode=pl.Buffered(1)),
                  pl.BlockSpec((LANES, seq), lambda b, i: (VT_B // LANES, b), pipeline_mode=pl.Buffered(1)),
                  pl.BlockSpec((1, seq, t), lambda b, i: (b, 0, i)),
                  pl.BlockSpec((t, 512), lambda b, i: (b * nq + i, OFF_BG // 512)),
                  pl.BlockSpec((N_HEADS, B_NEAR, t, t), lambda b, i: (0, 0, 0, 0),
                               pipeline_mode=pl.Buffered(1))],
        out_specs=pl.BlockSpec((t, 512), lambda b, i: (b * nq + i, 0)),
        out_shape=jax.ShapeDtypeStruct((n, 512), BF16),
        scratch_shapes=[pltpu.VMEM((N_HEADS, 8, t), F32), pltpu.VMEM((N_HEADS, ACC_ROWS, t), F32),
                        pltpu.VMEM((2, N_HEADS, FAR_TILES * t, t), F32)],
        compiler_params=_params(("arbitrary", "arbitrary"), 56),
        name="mix_b",
    )(qb, kb, vt, maskadd, z, tbl)


def _mix_c_kernel(q_ref, k_ref, vt_ref, ck_ref, cq_ref, gate_ref, o_ref, m_ref, acc_ref, s_ref):
    i = pl.program_id(1)
    _init_state(m_ref, acc_ref)
    qs = [_head_qt(q_ref, h) for h in range(N_HEADS)]
    q0 = pl.multiple_of(i * ATT_T, ATT_T)
    cq = cq_ref[0, :, pl.ds(q0, ATT_T)]
    krow = lax.broadcasted_iota(jnp.int32, (ATT_T, ATT_T), 0)
    qcol = lax.broadcasted_iota(jnp.int32, (ATT_T, ATT_T), 1)

    def tile_logits(j, size, diag=False):
        base = pl.multiple_of(j * ATT_T, ATT_T)
        ck_all = ck_ref[pl.ds(base, size), :]

        def logits(h):
            p = h // 2
            ck = ck_all[:, MISC_CF + h:MISC_CF + h + 1]
            s = _dot(k_ref[pl.ds(base, size), p * LANES:(p + 1) * LANES], qs[h]) - ck
            return jnp.where(krow <= qcol, s, NEG) if diag else s
        return logits

    def tile_values(j, size):
        base = pl.multiple_of(j * ATT_T, ATT_T)
        return lambda h: vt_ref[h * HEAD_DIM:(h + 1) * HEAD_DIM, pl.ds(base, size)]

    def qbias(h):
        return cq[h:h + 1, :]

    def tile(j, size, diag):
        _heads_pipelined(tile_logits(j, size, diag), tile_values(j, size), m_ref, acc_ref, qbias=qbias)

    far_size = FAR_TILES * ATT_T
    _tiles_skewed(i // FAR_TILES, lambda t: tile_logits(t * FAR_TILES, far_size),
                  lambda t: tile_values(t * FAR_TILES, far_size), m_ref, acc_ref, s_ref, qbias=qbias)
    _far_remainder(i, lambda j, size: tile(j, size, False))
    tile(i, ATT_T, True)
    for p in range(N_PAIRS):
        sl = slice(p * LANES, (p + 1) * LANES)
        o_ref[:, sl] = _gated_pair(p, gate_ref[:, sl], acc_ref)


def _mix_c(qc, kc, z, vt, cum, cum_t, batch, seq):
    n = qc.shape[0]
    t = ATT_T
    nq = seq // t
    return pl.pallas_call(
        _mix_c_kernel,
        grid=(batch, nq),
        in_specs=[pl.BlockSpec((t, 512), lambda b, i: (b * nq + i, 0)),
                  pl.BlockSpec((seq, 512), lambda b, i: (b, 0), pipeline_mode=pl.Buffered(1)),
                  pl.BlockSpec((512, seq), lambda b, i: (VT_C // 512, b), pipeline_mode=pl.Buffered(1)),
                  pl.BlockSpec((seq, LANES), lambda b, i: (b, 0), pipeline_mode=pl.Buffered(1)),
                  pl.BlockSpec((1, N_HEADS, seq), lambda b, i: (b, 0, 0)),
                  pl.BlockSpec((t, 512), lambda b, i: (b * nq + i, OFF_CG // 512))],
        out_specs=pl.BlockSpec((t, 512), lambda b, i: (b * nq + i, 0)),
        out_shape=jax.ShapeDtypeStruct((n, 512), BF16),
        scratch_shapes=[pltpu.VMEM((N_HEADS, 8, t), F32), pltpu.VMEM((N_HEADS, ACC_ROWS, t), F32),
                        pltpu.VMEM((2, N_HEADS, FAR_TILES * t, t), F32)],
        compiler_params=_params(("arbitrary", "arbitrary"), 56),
        name="mix_c",
    )(qc, kc, vt, cum, cum_t, z)


def _merge_kernel(oa, ob, oc, ga, gb, gc, x_ref, wb_ref, wo_ref, o_ref):
    merged = None
    for br, (o, g) in enumerate(((oa, ga), (ob, gb), (oc, gc))):
        y = _dot(o[...], wb_ref[br])
        term = jax.nn.sigmoid(g[...].astype(F32)) * y
        merged = term if merged is None else merged + term
    o_ref[...] = x_ref[...] + _dot(merged.astype(BF16), wo_ref[...])


def _merge(oa, ob, oc, z, x2d, wb, wo):
    n, d = x2d.shape
    tm = 512

    def ospec():
        return pl.BlockSpec((tm, 512), lambda i: (i, 0))

    def gspec(k):
        return pl.BlockSpec((tm, d), lambda i, _k=k: (i, OFF_MERGE // d + _k))

    return pl.pallas_call(
        _merge_kernel,
        grid=(n // tm,),
        in_specs=[ospec(), ospec(), ospec(), gspec(0), gspec(1), gspec(2),
                  pl.BlockSpec((tm, d), lambda i: (i, 0)),
                  pl.BlockSpec((3, 512, d), lambda i: (0, 0, 0)),
                  pl.BlockSpec((d, d), lambda i: (0, 0))],
        out_specs=pl.BlockSpec((tm, d), lambda i: (i, 0)),
        out_shape=jax.ShapeDtypeStruct((n, d), F32),
        compiler_params=_params(("arbitrary",), 48),
        name="merge",
    )(oa, ob, oc, z, z, z, x2d, wb, wo)


def _bucket_lookup(table, bucket):
    out = jnp.zeros(bucket.shape + (table.shape[1],), F32)
    for b in range(N_BUCKETS):
        out = jnp.where((bucket == b)[..., None], table[b].astype(F32), out)
    return out


def _window_table(bias_a):
    ki = jnp.arange(2 * WINDOW)[:, None]
    qi = jnp.arange(WINDOW)[None, :]
    delta = qi + WINDOW - ki
    band = (delta >= 0) & (delta < WINDOW)
    bias = _bucket_lookup(bias_a, _t5_bucket(delta)).transpose(2, 0, 1) * LOG2E
    return jnp.where(band[None], bias, NEG)


def _near_table(bias_b):
    ki = jnp.arange(ATT_T)[:, None]
    qi = jnp.arange(ATT_T)[None, :]
    d = jnp.arange(B_NEAR)[:, None, None]
    delta = d * ATT_T + qi[None] - ki[None]
    bias = _bucket_lookup(bias_b, _t5_bucket(delta))
    far = bias_b[N_BUCKETS - 1].astype(F32)
    return ((bias - far) * LOG2E).transpose(3, 0, 1, 2)


def kernel(x, norm_gain, w_in, b_forget, qk_gain, sinks, w_branch, w_out, rel_bias):
    batch, seq, d = x.shape
    depth = norm_gain.shape[0]
    assert w_in.shape[-1] == IN_COLS and seq % 1024 == 0 and d == 1024
    assert (B_NEAR - 1) * ATT_T - (ATT_T - 1) < T5_FAR <= B_NEAR * ATT_T - (ATT_T - 1)

    w_perm = _gather_cols(w_in, _COLS).astype(BF16)
    w_vt = jnp.swapaxes(_gather_cols(w_in, _VT_COLS), 1, 2).astype(BF16)
    w_st = jnp.swapaxes(_gather_cols(w_in, _ST_COLS), 1, 2).astype(BF16)
    wb = w_branch.astype(BF16)
    wo = w_out.astype(BF16)

    gmat = jnp.asarray(np.kron(np.eye(N_HEADS), np.ones((HEAD_DIM, HEAD_DIM))), BF16)
    prep_t = 512
    lmat = jnp.asarray(np.tril(np.ones((prep_t, prep_t))), BF16)
    lmat_sel = lmat[:SEL_MC, :SEL_MC]
    tbl_a = _window_table(rel_bias[:, :N_HEADS])
    tbl_b = _near_table(rel_bias[:, N_HEADS:])

    x2d = x.reshape(batch * seq, d)
    for layer in range(depth):
        g = qk_gain[layer]
        qscale = ATTN_SCALE * LOG2E
        gains = jnp.zeros((8, 512), F32)
        gains = gains.at[0].set(jnp.tile(g[0, 0], N_HEADS) * qscale)
        gains = gains.at[1, :256].set(jnp.tile(g[0, 1], 4))
        gains = gains.at[2].set(jnp.tile(g[1, 0], N_HEADS) * qscale)
        gains = gains.at[3, :256].set(jnp.tile(g[1, 1], 4))
        gains = gains.at[4].set(jnp.tile(g[2, 0], N_HEADS) * qscale)
        gains = gains.at[5].set(jnp.tile(g[2, 1], N_HEADS))
        gains = gains.at[6, MISC_CF:MISC_CF + N_HEADS].set(b_forget[layer])

        z, misc, vt, st = _proj(x2d, norm_gain[layer][None, :], w_perm[layer], w_vt[layer], w_st[layer])
        qa, ka, qb, kb, qc, kc, cum = _prep(z, misc, gains, gmat, lmat, batch, seq)
        cum_t = cum.reshape(batch, seq, LANES)[:, :, MISC_CF:MISC_CF + N_HEADS].transpose(0, 2, 1)
        oa = _mix_a(qa, ka, z, vt, tbl_a, sinks[layer], batch, seq)
        maskadd = _sel_b(z, st, lmat_sel, batch, seq)
        ob = _mix_b(qb, kb, z, vt, maskadd, tbl_b, batch, seq)
        oc = _mix_c(qc, kc, z, vt, cum, cum_t, batch, seq)
        x2d = _merge(oa, ob, oc, z, x2d, wb[layer], wo[layer])
    return x2d.reshape(batch, seq, d)
```

```python
import functools
import math

import numpy as np
import jax
import jax.numpy as jnp
from jax import lax
from jax.experimental import pallas as pl
from jax.experimental.pallas import tpu as pltpu

F32 = jnp.float32
BF16 = jnp.bfloat16

HEAD_DIM = 64
LANES = 128
N_HEADS = 8
N_PAIRS = N_HEADS // 2
KV_GROUP = 4
WINDOW = 128
IDX_HEADS = 4
IDX_DIM = 32
TOPK_MAX = 256
N_BUCKETS = 32
MAX_DISTANCE = 512
EPS = 1e-6
ATTN_SCALE = HEAD_DIM ** -0.5
LOG2E = math.log2(math.e)
NEG = -1e30
T5_FAR = 413

_SIZES = dict(a_q=512, a_k=128, a_v=128, a_gate=512,
              b_q=512, b_k=128, b_v=128, b_iq=128, b_ik=32, b_iw=4, b_gate=512,
              c_q=512, c_k=512, c_v=512, c_f=8, c_gate=512, merge=3072)
_ORIG = {}
_o = 0
for _k, _v in _SIZES.items():
    _ORIG[_k] = _o
    _o += _v
IN_COLS = _o

OFF_MERGE = 0
OFF_AQ, OFF_AG, OFF_BQ, OFF_BG = 3072, 3584, 4096, 4608
OFF_CQ, OFF_CK, OFF_CG = 5120, 5632, 6144
OFF_AK, OFF_BK = 6656, 6912
OFF_BIQ, OFF_BIK, OFF_MISC = 7168, 7296, 7424
NP_COLS = 7680
PROJ_TN = 1536
MISC_CF = 8
VT_C, VT_A, VT_B, VT_ROWS = 0, 512, 640, 768


def _orig(name, idx=None):
    src = np.arange(_SIZES[name]) + _ORIG[name]
    return src if idx is None else src[idx]


def _column_map():
    cols = np.full((NP_COLS,), -1, np.int64)

    def put(off, src):
        cols[off:off + len(src)] = src

    for off, name in ((OFF_MERGE, "merge"), (OFF_AQ, "a_q"), (OFF_AG, "a_gate"), (OFF_BQ, "b_q"),
                      (OFF_BG, "b_gate"), (OFF_CQ, "c_q"), (OFF_CK, "c_k"), (OFF_CG, "c_gate"),
                      (OFF_BIQ, "b_iq")):
        put(off, _orig(name))
    dup = np.concatenate([np.arange(64), np.arange(64), np.arange(64, 128), np.arange(64, 128)])
    put(OFF_AK, _orig("a_k", dup))
    put(OFF_BK, _orig("b_k", dup))
    put(OFF_BIK, _orig("b_ik", np.tile(np.arange(IDX_DIM), IDX_HEADS)))
    put(OFF_MISC + MISC_CF, _orig("c_f"))
    return cols


def _gather_cols(w, cols):
    parts = []
    start = 0
    while start < len(cols):
        end = start + 1
        if cols[start] < 0:
            while end < len(cols) and cols[end] < 0:
                end += 1
            parts.append(jnp.zeros(w.shape[:-1] + (end - start,), w.dtype))
        else:
            while end < len(cols) and cols[end] == cols[end - 1] + 1:
                end += 1
            parts.append(w[..., int(cols[start]):int(cols[end - 1]) + 1])
        start = end
    return jnp.concatenate(parts, axis=-1)


_COLS = _column_map()
_VT_COLS = np.concatenate([_orig("c_v"), _orig("a_v"), _orig("b_v")])
_ST_COLS = np.concatenate([_orig("b_iw"), np.full((4,), -1, np.int64)])


def _params(sem, vmem_mb):
    return pltpu.CompilerParams(dimension_semantics=sem, vmem_limit_bytes=vmem_mb * 1024 * 1024)


def _dot(a, b):
    return jnp.dot(a, b, preferred_element_type=F32)


def _dot_nt(a, b):
    return lax.dot_general(a, b, (((1,), (1,)), ((), ())), preferred_element_type=F32)


def _t5_bucket(delta):
    n = jnp.maximum(delta, 0)
    max_exact = N_BUCKETS // 2
    nf = jnp.maximum(n, 1).astype(F32)
    large = max_exact + (jnp.log(nf / max_exact) / math.log(MAX_DISTANCE / max_exact)
                         * (N_BUCKETS - max_exact)).astype(jnp.int32)
    large = jnp.minimum(large, N_BUCKETS - 1)
    return jnp.where(n < max_exact, n, large)


def _proj_kernel(x_ref, g_ref, w_ref, wvt_ref, wst_ref, z_ref, misc_ref, vt_ref, st_ref, h_ref,
                 *, misc_tile, misc_local):
    j = pl.program_id(1)

    @pl.when(j == 0)
    def _():
        x = x_ref[...]
        ms = jnp.mean(x * x, axis=-1, keepdims=True)
        h = (x * lax.rsqrt(ms + EPS) * g_ref[...]).astype(BF16)
        h_ref[...] = h
        vt_ref[...] = _dot_nt(wvt_ref[...], h).astype(BF16)
        st_ref[...] = _dot_nt(wst_ref[...], h)

    acc = _dot(h_ref[...], w_ref[...])
    z_ref[...] = acc.astype(BF16)

    @pl.when(j == misc_tile)
    def _():
        misc_ref[...] = acc[:, misc_local:misc_local + LANES]


def _proj(x2d, gain, w, wvt, wst):
    n, d = x2d.shape
    tm = 1024
    return pl.pallas_call(
        functools.partial(_proj_kernel, misc_tile=OFF_MISC // PROJ_TN, misc_local=OFF_MISC % PROJ_TN),
        grid=(n // tm, NP_COLS // PROJ_TN),
        in_specs=[pl.BlockSpec((tm, d), lambda i, j: (i, 0)),
                  pl.BlockSpec((1, d), lambda i, j: (0, 0)),
                  pl.BlockSpec((d, PROJ_TN), lambda i, j: (0, j)),
                  pl.BlockSpec((VT_ROWS, d), lambda i, j: (0, 0)),
                  pl.BlockSpec((8, d), lambda i, j: (0, 0))],
        out_specs=[pl.BlockSpec((tm, PROJ_TN), lambda i, j: (i, j)),
                   pl.BlockSpec((tm, LANES), lambda i, j: (i, 0)),
                   pl.BlockSpec((VT_ROWS, tm), lambda i, j: (0, i)),
                   pl.BlockSpec((8, tm), lambda i, j: (0, i))],
        out_shape=[jax.ShapeDtypeStruct((n, NP_COLS), BF16),
                   jax.ShapeDtypeStruct((n, LANES), F32),
                   jax.ShapeDtypeStruct((VT_ROWS, n), BF16),
                   jax.ShapeDtypeStruct((8, n), F32)],
        scratch_shapes=[pltpu.VMEM((tm, d), BF16)],
        compiler_params=_params(("arbitrary", "arbitrary"), 56),
        name="proj",
    )(x2d, gain, w, wvt, wst)


def _group_rms(x_bf16, g_ref, gain_row):
    x = x_bf16.astype(F32)
    width = x.shape[-1]
    ss = _dot((x * x).astype(BF16), g_ref[:width, :width])
    return (x * lax.rsqrt(ss * (1.0 / HEAD_DIM) + EPS) * gain_row).astype(BF16)


def _prep_kernel(aq, ak, bq, bk, cq, ck, misc, gains, gmat, lmat,
                 qa_o, ka_o, qb_o, kb_o, qc_o, kc_o, cum_o, carry):
    t = pl.program_id(1)
    qa_o[...] = _group_rms(aq[...], gmat, gains[0:1, :])
    ka_o[...] = _group_rms(ak[...], gmat, gains[1:2, :256])
    qb_o[...] = _group_rms(bq[...], gmat, gains[2:3, :])
    kb_o[...] = _group_rms(bk[...], gmat, gains[3:4, :256])
    qc_o[...] = _group_rms(cq[...], gmat, gains[4:5, :])
    kc_o[...] = _group_rms(ck[...], gmat, gains[5:6, :])

    @pl.when(t == 0)
    def _():
        carry[...] = jnp.zeros_like(carry)

    xm = misc[...] + gains[6:7, :LANES]
    logf = jnp.minimum(xm, 0.0) - jnp.log(1.0 + jnp.exp(-jnp.abs(xm)))
    hi = logf.astype(BF16)
    r1 = logf - hi.astype(F32)
    mid = r1.astype(BF16)
    lo = (r1 - mid.astype(F32)).astype(BF16)
    lm = lmat[...]
    c = _dot(lm, hi) + _dot(lm, mid) + _dot(lm, lo) + carry[...]
    cum_o[...] = c * LOG2E
    rows = c.shape[0]
    carry[...] = c[rows - 1:rows, :]


def _prep(z, misc, gains, gmat, lmat, batch, seq):
    n = z.shape[0]
    tp = lmat.shape[0]
    nt = seq // tp

    def zspec(width, off):
        return pl.BlockSpec((tp, width), lambda b, t, _c=off // width: (b * nt + t, _c))

    def ospec(width):
        return pl.BlockSpec((tp, width), lambda b, t: (b * nt + t, 0))

    return pl.pallas_call(
        _prep_kernel,
        grid=(batch, nt),
        in_specs=[zspec(512, OFF_AQ), zspec(256, OFF_AK), zspec(512, OFF_BQ), zspec(256, OFF_BK),
                  zspec(512, OFF_CQ), zspec(512, OFF_CK), ospec(LANES),
                  pl.BlockSpec((8, 512), lambda b, t: (0, 0)),
                  pl.BlockSpec((512, 512), lambda b, t: (0, 0)),
                  pl.BlockSpec((tp, tp), lambda b, t: (0, 0))],
        out_specs=[ospec(512), ospec(256), ospec(512), ospec(256), ospec(512), ospec(512), ospec(LANES)],
        out_shape=[jax.ShapeDtypeStruct((n, 512), BF16), jax.ShapeDtypeStruct((n, 256), BF16),
                   jax.ShapeDtypeStruct((n, 512), BF16), jax.ShapeDtypeStruct((n, 256), BF16),
                   jax.ShapeDtypeStruct((n, 512), BF16), jax.ShapeDtypeStruct((n, 512), BF16),
                   jax.ShapeDtypeStruct((n, LANES), F32)],
        scratch_shapes=[pltpu.VMEM((1, LANES), F32)],
        compiler_params=_params(("arbitrary", "arbitrary"), 48),
        name="prep",
    )(z, z, z, z, z, z, misc, gains, gmat, lmat)


def _head_qt(q_ref, h):
    p = h // 2
    qt = q_ref[:, p * LANES:(p + 1) * LANES].astype(F32).T
    row = lax.broadcasted_iota(jnp.int32, qt.shape, 0)
    own = (row < HEAD_DIM) if h % 2 == 0 else (row >= HEAD_DIM)
    return jnp.where(own, qt, 0.0).astype(BF16)


DEN_ROWS = 16
ACC_ROWS = HEAD_DIM + DEN_ROWS


def _with_ones(vt):
    return jnp.concatenate([vt, jnp.ones((DEN_ROWS, vt.shape[1]), vt.dtype)], axis=0)


def _softmax_step(h, s, vt, m_ref, acc_ref, qbias=None):
    m_prev = m_ref[h]
    s_of = s if callable(s) else (lambda: s)
    s_max = jnp.max(s_of(), axis=0, keepdims=True)
    m_new = jnp.maximum(m_prev, s_max if qbias is None else s_max + qbias)
    alpha = jnp.exp2(m_prev - m_new)
    shift = m_new[0:1, :] if qbias is None else m_new[0:1, :] - qbias
    p = jnp.exp2((s_of() - shift).astype(BF16))
    acc_ref[h] = alpha[0:1, :] * acc_ref[h] + _dot(_with_ones(vt), p)
    m_ref[h] = m_new


HEADS_AHEAD = 4


def _heads_pipelined(logits, values, m_ref, acc_ref, qbias=None):
    ahead = [logits(h) for h in range(HEADS_AHEAD)]
    for h in range(N_HEADS):
        if h + HEADS_AHEAD < N_HEADS:
            ahead.append(logits(h + HEADS_AHEAD))
        _softmax_step(h, ahead.pop(0), values(h), m_ref, acc_ref,
                      None if qbias is None else qbias(h))


SKEW_AHEAD = 1


def _tiles_skewed(n_tiles, tile_logits, tile_values, m_ref, acc_ref, s_ref, qbias=None):
    bias = (lambda h: None) if qbias is None else qbias

    def produce(t, buf):
        logits = tile_logits(t)
        for h in range(N_HEADS):
            s_ref[buf, h] = logits(h)

    def consume(t, buf):
        values = tile_values(t)
        for h in range(N_HEADS):
            _softmax_step(h, lambda h=h: s_ref[buf, h], values(h), m_ref, acc_ref, bias(h))

    def both(t, buf):
        logits, values = tile_logits(t + 1), tile_values(t)
        for h in range(SKEW_AHEAD):
            s_ref[1 - buf, h] = logits(h)
        for h in range(N_HEADS):
            if h + SKEW_AHEAD < N_HEADS:
                s_ref[1 - buf, h + SKEW_AHEAD] = logits(h + SKEW_AHEAD)
            _softmax_step(h, lambda h=h: s_ref[buf, h], values(h), m_ref, acc_ref, bias(h))

    pairs = jnp.maximum(n_tiles - 1, 0) // 2
    rem = n_tiles - 2 * pairs

    @pl.when(n_tiles > 0)
    def _():
        produce(0, 0)

    def trip(p, carry):
        both(2 * p, 0)
        both(2 * p + 1, 1)
        return carry

    lax.fori_loop(0, pairs, trip, 0)

    @pl.when(rem == 2)
    def _():
        both(2 * pairs, 0)
        consume(2 * pairs + 1, 1)

    @pl.when(rem == 1)
    def _():
        consume(2 * pairs, 0)


def _init_state(m_ref, acc_ref):
    m_ref[...] = jnp.full(m_ref.shape, NEG, F32)
    acc_ref[...] = jnp.zeros(acc_ref.shape, F32)


def _gated_pair(p, gate, acc_ref):
    halves = []
    for h in (2 * p, 2 * p + 1):
        a = acc_ref[h]
        halves.append(a[:HEAD_DIM] / a[HEAD_DIM:HEAD_DIM + 1])
    ot = jnp.concatenate(halves, axis=0)
    g = gate.astype(F32)
    return (ot.T * (g * jax.nn.sigmoid(g))).astype(BF16)


def _mix_a_kernel(sink_ref, q_ref, kp_ref, kc_ref, vp_ref, vc_ref, gate_ref, tbl_ref, o_ref, acc_ref):
    n = pl.program_id(1)
    pad_pen = jnp.where(n > 0, 0.0, NEG).astype(F32)
    den_rows = lax.broadcasted_iota(jnp.int32, (ACC_ROWS, WINDOW), 0) >= HEAD_DIM

    def logits(h):
        qt = _head_qt(q_ref, h)
        ksl = slice((h // KV_GROUP) * LANES, (h // KV_GROUP + 1) * LANES)
        return (_dot(kp_ref[:, ksl], qt) + tbl_ref[h, :WINDOW, :] + pad_pen,
                _dot(kc_ref[:, ksl], qt) + tbl_ref[h, WINDOW:, :])

    all_logits = [logits(h) for h in range(N_HEADS)]
    for h in range(N_HEADS):
        kv = h // KV_GROUP
        vsl = slice(kv * HEAD_DIM, (kv + 1) * HEAD_DIM)
        s_prev, s_cur = all_logits[h]
        sink = sink_ref[h] * LOG2E
        m = jnp.maximum(jnp.maximum(jnp.max(s_prev, axis=0, keepdims=True),
                                    jnp.max(s_cur, axis=0, keepdims=True)), sink)
        p_prev = jnp.exp2((s_prev - m).astype(BF16))
        p_cur = jnp.exp2((s_cur - m).astype(BF16))
        acc = _dot(_with_ones(vp_ref[vsl, :]), p_prev) + _dot(_with_ones(vc_ref[vsl, :]), p_cur)
        acc_ref[h] = acc + jnp.where(den_rows, jnp.exp2(sink - m), 0.0)
    for p in range(N_PAIRS):
        sl = slice(p * LANES, (p + 1) * LANES)
        o_ref[:, sl] = _gated_pair(p, gate_ref[:, sl], acc_ref)


def _mix_a(qa, ka, z, vt, tbl, sinks, batch, seq):
    n = qa.shape[0]
    t = WINDOW
    nb = seq // t
    return pl.pallas_call(
        _mix_a_kernel,
        grid=(batch, nb),
        in_specs=[pl.BlockSpec(memory_space=pltpu.SMEM),
                  pl.BlockSpec((t, 512), lambda b, i: (b * nb + i, 0)),
                  pl.BlockSpec((t, 256), lambda b, i: (b * nb + jnp.maximum(i - 1, 0), 0)),
                  pl.BlockSpec((t, 256), lambda b, i: (b * nb + i, 0)),
                  pl.BlockSpec((LANES, t), lambda b, i: (VT_A // LANES, b * nb + jnp.maximum(i - 1, 0))),
                  pl.BlockSpec((LANES, t), lambda b, i: (VT_A // LANES, b * nb + i)),
                  pl.BlockSpec((t, 512), lambda b, i: (b * nb + i, OFF_AG // 512)),
                  pl.BlockSpec((N_HEADS, 2 * t, t), lambda b, i: (0, 0, 0))],
        out_specs=pl.BlockSpec((t, 512), lambda b, i: (b * nb + i, 0)),
        out_shape=jax.ShapeDtypeStruct((n, 512), BF16),
        scratch_shapes=[pltpu.VMEM((N_HEADS, ACC_ROWS, t), F32)],
        compiler_params=_params(("arbitrary", "arbitrary"), 32),
        name="mix_a",
    )(sinks, qa, ka, ka, vt, vt, z, tbl)


SEL_TQ = 256
SEL_SC = 512
SEL_SUB = 64
SEL_MC = 256
SEL_FAST_ITERS = 24
SEL_MAX_ITERS = 600


def _sel_b_kernel(iq_ref, ik_ref, wt_ref, lmat_ref, o_ref, sc_ref, raw_ref, *, topk):
    n = pl.program_id(1)
    t0 = n * SEL_TQ
    n_full = n // (SEL_SC // SEL_TQ)
    n_sc = n_full + 1
    n_mc = n + 1
    kf = float(topk)

    iqt = iq_ref[...].astype(F32).T
    row = lax.broadcasted_iota(jnp.int32, iqt.shape, 0)
    qst = jnp.concatenate(
        [jnp.where((row >= h * IDX_DIM) & (row < (h + 1) * IDX_DIM), iqt, 0.0)
         for h in range(IDX_HEADS)], axis=1).astype(BF16)
    wscale = (IDX_HEADS ** -0.5) * (IDX_DIM ** -0.5)
    w = [wt_ref[h:h + 1, :] * wscale for h in range(IDX_HEADS)]
    qpos = t0 + lax.broadcasted_iota(jnp.int32, (SEL_SUB, SEL_TQ), 1)
    kiota = lax.broadcasted_iota(jnp.int32, (SEL_SUB, SEL_TQ), 0)

    def raw_into(c, buf):
        base = pl.multiple_of(c * SEL_SC, SEL_SC)
        raw_ref[buf] = _dot(ik_ref[pl.ds(base, SEL_SC), :], qst)

    def score_chunk(c, buf, carry, diagonal):
        rmax, rmin, ge0, gt0, minpos = carry
        base = pl.multiple_of(c * SEL_SC, SEL_SC)
        for u in range(SEL_SC // SEL_SUB):
            rows = slice(u * SEL_SUB, (u + 1) * SEL_SUB)
            score = w[0] * jnp.maximum(raw_ref[buf, rows, 0:SEL_TQ], 0.0)
            for h in range(1, IDX_HEADS):
                score = score + w[h] * jnp.maximum(raw_ref[buf, rows, h * SEL_TQ:(h + 1) * SEL_TQ], 0.0)
            rmin = jnp.minimum(rmin, score)
            if diagonal:
                score = jnp.where((kiota + (base + u * SEL_SUB)) <= qpos, score, -jnp.inf)
            sc_ref[pl.ds(base + u * SEL_SUB, SEL_SUB), :] = score
            rmax = jnp.maximum(rmax, score)
            pos = score > 0.0
            ge0 = ge0 + jnp.where(score >= 0.0, 1.0, 0.0)
            gt0 = gt0 + jnp.where(pos, 1.0, 0.0)
            minpos = jnp.minimum(minpos, jnp.where(pos, score, jnp.inf))
        return rmax, rmin, ge0, gt0, minpos

    def slab(v):
        return jnp.full((SEL_SUB, SEL_TQ), v, F32)

    def score_trip(p, carry):
        raw_into(2 * p + 1, 1)
        carry = score_chunk(2 * p, 0, carry, False)
        raw_into(2 * p + 2, 0)
        return score_chunk(2 * p + 1, 1, carry, False)

    def tail_two(carry):
        raw_into(n_full, 1)
        carry = score_chunk(n_full - 1, 0, carry, False)
        return score_chunk(n_full, 1, carry, True)

    def tail_one(carry):
        return score_chunk(n_full, 0, carry, True)

    raw_into(0, 0)
    carry = lax.fori_loop(0, n_full // 2, score_trip,
                          (slab(-jnp.inf), slab(jnp.inf), slab(0.0), slab(0.0), slab(jnp.inf)))
    rmax, rmin, ge0, gt0, minpos = lax.cond(n_full % 2 == 1, tail_two, tail_one, carry)
    rmax = jnp.max(rmax, axis=0, keepdims=True)
    rmin = jnp.min(rmin, axis=0, keepdims=True)
    ge0 = jnp.sum(ge0, axis=0, keepdims=True)
    gt0 = jnp.sum(gt0, axis=0, keepdims=True)
    minpos = jnp.min(minpos, axis=0, keepdims=True)

    navail = (t0 + 1 + lax.broadcasted_iota(jnp.int32, (1, SEL_TQ), 1)).astype(F32)

    def sweep(mid, snap):
        midb = jnp.broadcast_to(mid, (SEL_SUB, SEL_TQ))

        def body(c, carry):
            base = pl.multiple_of(c * SEL_SC, SEL_SC)
            cnt, up, dn = carry
            for u in range(SEL_SC // SEL_SUB):
                x = sc_ref[pl.ds(base + u * SEL_SUB, SEL_SUB), :]
                ge = x >= midb
                cnt = cnt + jnp.where(ge, 1.0, 0.0)
                if snap:
                    up = jnp.minimum(up, jnp.where(ge, x, jnp.inf))
                    dn = jnp.maximum(dn, jnp.where(ge, -jnp.inf, x))
            return cnt, up, dn

        init = (jnp.zeros((SEL_SUB, SEL_TQ), F32), jnp.full((SEL_SUB, SEL_TQ), jnp.inf, F32),
                jnp.full((SEL_SUB, SEL_TQ), -jnp.inf, F32))
        cnt, up, dn = lax.fori_loop(0, n_sc, body, init)
        return (jnp.sum(cnt, axis=0, keepdims=True), jnp.min(up, axis=0, keepdims=True),
                jnp.max(dn, axis=0, keepdims=True))

    def advance(st, mid, snap):
        it, lo, hi, hi_dn, c_lo, c_hi, done = st
        cnt, up, dn = sweep(mid, snap)
        live = jnp.where(done > 0.5, 0.0, jnp.where(mid <= lo, 0.0, jnp.where(mid >= hi, 0.0, 1.0)))
        ge = jnp.where(cnt >= kf, live, 0.0) > 0.5
        lt = jnp.where(cnt >= kf, 0.0, live) > 0.5
        lo = jnp.where(ge, up if snap else mid, lo)
        c_lo = jnp.where(ge, cnt, c_lo)
        hi = jnp.where(lt, mid, hi)
        hi_dn = jnp.where(lt, dn if snap else jnp.inf, hi_dn)
        c_hi = jnp.where(lt, cnt, c_hi)
        done = jnp.where(live < 0.5, 1.0, jnp.where(cnt == kf, 1.0, jnp.where(hi_dn <= lo, 1.0, 0.0)))
        return it + 1, lo, hi, hi_dn, c_lo, c_hi, done

    def make_step(snap, repeat=1):
        def step(st):
            for _ in range(repeat):
                st = advance(st, 0.5 * st[1] + 0.5 * st[2], snap)
            return st
        return step

    def make_cond(limit):
        def cond(st):
            return jnp.logical_and(st[0] < limit, jnp.min(st[6]) < 0.5)
        return cond

    inf_row = jnp.full((1, SEL_TQ), jnp.inf, F32)
    all_in = navail <= kf
    above = jnp.logical_and(jnp.logical_not(all_in), gt0 >= kf)
    at_zero = jnp.logical_and(jnp.logical_not(all_in), jnp.logical_and(gt0 < kf, ge0 >= kf))
    below = jnp.logical_and(jnp.logical_not(all_in), ge0 < kf)
    lo = jnp.where(above, minpos, jnp.where(at_zero, 0.0, rmin))
    c_lo = jnp.where(above, gt0, jnp.where(at_zero, ge0, navail))
    hi = jnp.where(at_zero, minpos, jnp.where(below, 0.0, inf_row))
    c_hi = jnp.where(at_zero, gt0, jnp.where(below, ge0, 0.0))
    hi_dn = jnp.where(above, rmax, inf_row)
    done = jnp.where(all_in, 1.0, jnp.where(at_zero, 1.0, 0.0))
    st = (jnp.int32(0), lo, hi, hi_dn, c_lo, c_hi, done)
    st = advance(st, jnp.where(above, rmax, 0.5 * lo + 0.5 * hi), False)
    st = lax.while_loop(make_cond(SEL_FAST_ITERS), make_step(False, repeat=2), st)
    st = lax.while_loop(make_cond(SEL_MAX_ITERS), make_step(True), st)
    _, lo, hi, _, c_lo, c_hi, _ = st
    need = kf - c_hi
    ties = jnp.max(jnp.where(navail > kf, c_lo, kf)) > kf

    def fill_body(c, carry):
        base = pl.multiple_of(c * SEL_MC, SEL_MC)
        o_ref[0, pl.ds(base, SEL_MC), :] = jnp.full((SEL_MC, SEL_TQ), NEG, o_ref.dtype)
        return carry
    lax.fori_loop(n_mc, o_ref.shape[1] // SEL_MC, fill_body, 0)

    @pl.when(jnp.logical_not(ties))
    def _():
        def mask_body(c, carry):
            base = pl.multiple_of(c * SEL_MC, SEL_MC)
            x = sc_ref[pl.ds(base, SEL_MC), :]
            o_ref[0, pl.ds(base, SEL_MC), :] = jnp.where(x >= lo, 0.0, NEG).astype(o_ref.dtype)
            return carry
        lax.fori_loop(0, n_mc, mask_body, 0)

    @pl.when(ties)
    def _():
        lmat = lmat_ref[...]

        def mask_body(c, run):
            base = pl.multiple_of(c * SEL_MC, SEL_MC)
            x = sc_ref[pl.ds(base, SEL_MC), :]
            ge_lo = x >= lo
            lt_hi = x < hi
            bnd = jnp.where(ge_lo, jnp.where(lt_hi, 1.0, 0.0), 0.0)
            incl = _dot(lmat, bnd.astype(BF16)) + run
            inside = jnp.where(lt_hi, jnp.where(incl <= need, 0.0, NEG), 0.0)
            o_ref[0, pl.ds(base, SEL_MC), :] = jnp.where(ge_lo, inside, NEG).astype(o_ref.dtype)
            return incl[SEL_MC - 1:SEL_MC, :]
        lax.fori_loop(0, n_mc, mask_body, jnp.zeros((1, SEL_TQ), F32))


def _sel_b(z, st, lmat, batch, seq):
    nb = seq // SEL_TQ
    topk = min(TOPK_MAX, seq // 4)
    return pl.pallas_call(
        functools.partial(_sel_b_kernel, topk=topk),
        grid=(batch, nb),
        in_specs=[pl.BlockSpec((SEL_TQ, LANES), lambda b, i: (b * nb + i, OFF_BIQ // LANES)),
                  pl.BlockSpec((seq, LANES), lambda b, i: (b, OFF_BIK // LANES)),
                  pl.BlockSpec((8, SEL_TQ), lambda b, i: (0, b * nb + i)),
                  pl.BlockSpec((SEL_MC, SEL_MC), lambda b, i: (0, 0))],
        out_specs=pl.BlockSpec((1, seq, SEL_TQ), lambda b, i: (b, 0, i)),
        out_shape=jax.ShapeDtypeStruct((batch, seq, seq), BF16),
        scratch_shapes=[pltpu.VMEM((seq, SEL_TQ), F32),
                        pltpu.VMEM((2, SEL_SC, IDX_HEADS * SEL_TQ), F32)],
        compiler_params=_params(("arbitrary", "arbitrary"), 48),
        name="sel_b",
    )(z, z, st, lmat)


ATT_T = 256
B_NEAR = 3
FAR_TILES = 2


def _far_remainder(n_tiles, tile):
    done = (n_tiles // FAR_TILES) * FAR_TILES
    rem = n_tiles - done
    size = FAR_TILES // 2
    while size >= 1:
        @pl.when((rem // size) % 2 == 1)
        def _(size=size, start=done + (rem // (2 * size)) * (2 * size)):
            tile(start, size * ATT_T)
        size //= 2


def _mix_b_kernel(q_ref, k_ref, vt_ref, mask_ref, gate_ref, tbl_ref, o_ref, m_ref, acc_ref, s_ref):
    i = pl.program_id(1)
    _init_state(m_ref, acc_ref)
    qs = [_head_qt(q_ref, h) for h in range(N_HEADS)]

    def tile_logits(j, size, near=None):
        base = pl.multiple_of(j * ATT_T, ATT_T)
        madd = mask_ref[0, pl.ds(base, size), :].astype(F32)

        def logits(h):
            kv = h // KV_GROUP
            s = _dot(k_ref[pl.ds(base, size), kv * LANES:(kv + 1) * LANES], qs[h]) + madd
            return s if near is None else s + tbl_ref[h, near]
        return logits

    def tile_values(j, size):
        base = pl.multiple_of(j * ATT_T, ATT_T)

        def values(h):
            kv = h // KV_GROUP
            return vt_ref[kv * HEAD_DIM:(kv + 1) * HEAD_DIM, pl.ds(base, size)]
        return values

    def tile(j, size, near):
        _heads_pipelined(tile_logits(j, size, near), tile_values(j, size), m_ref, acc_ref)

    n_far = jnp.maximum(i - (B_NEAR - 1), 0)
    far_size = FAR_TILES * ATT_T
    _tiles_skewed(n_far // FAR_TILES, lambda t: tile_logits(t * FAR_TILES, far_size),
                  lambda t: tile_values(t * FAR_TILES, far_size), m_ref, acc_ref, s_ref)
    _far_remainder(n_far, lambda j, size: tile(j, size, None))

    for d in range(B_NEAR - 1, -1, -1):
        @pl.when(i >= d)
        def _(d=d):
            tile(i - d, ATT_T, d)
    for p in range(N_PAIRS):
        sl = slice(p * LANES, (p + 1) * LANES)
        o_ref[:, sl] = _gated_pair(p, gate_ref[:, sl], acc_ref)


def _mix_b(qb, kb, z, vt, maskadd, tbl, batch, seq):
    n = qb.shape[0]
    t = ATT_T
    nq = seq // t
    return pl.pallas_call(
        _mix_b_kernel,
        grid=(batch, nq),
        in_specs=[pl.BlockSpec((t, 512), lambda b, i: (b * nq + i, 0)),
                  pl.BlockSpec((seq, 256), lambda b, i: (b, 0), pipeline_mode=pl.Buffered(1)),
                  pl.BlockSpec((LANES, seq), lambda b, i: (VT_B // LANES, b), pipeline_mode=pl.Buffered(1)),
                  pl.BlockSpec((1, seq, t), lambda b, i: (b, 0, i)),
                  pl.BlockSpec((t, 512), lambda b, i: (b * nq + i, OFF_BG // 512)),
                  pl.BlockSpec((N_HEADS, B_NEAR, t, t), lambda b, i: (0, 0, 0, 0),
                               pipeline_mode=pl.Buffered(1))],
        out_specs=pl.BlockSpec((t, 512), lambda b, i: (b * nq + i, 0)),
        out_shape=jax.ShapeDtypeStruct((n, 512), BF16),
        scratch_shapes=[pltpu.VMEM((N_HEADS, 8, t), F32), pltpu.VMEM((N_HEADS, ACC_ROWS, t), F32),
                        pltpu.VMEM((2, N_HEADS, FAR_TILES * t, t), F32)],
        compiler_params=_params(("arbitrary", "arbitrary"), 56),
        name="mix_b",
    )(qb, kb, vt, maskadd, z, tbl)


def _mix_c_kernel(q_ref, k_ref, vt_ref, ck_ref, cq_ref, gate_ref, o_ref, m_ref, acc_ref, s_ref):
    i = pl.program_id(1)
    _init_state(m_ref, acc_ref)
    qs = [_head_qt(q_ref, h) for h in range(N_HEADS)]
    q0 = pl.multiple_of(i * ATT_T, ATT_T)
    cq = cq_ref[0, :, pl.ds(q0, ATT_T)]
    krow = lax.broadcasted_iota(jnp.int32, (ATT_T, ATT_T), 0)
    qcol = lax.broadcasted_iota(jnp.int32, (ATT_T, ATT_T), 1)

    def tile_logits(j, size, diag=False):
        base = pl.multiple_of(j * ATT_T, ATT_T)
        ck_all = ck_ref[pl.ds(base, size), :]

        def logits(h):
            p = h // 2
            ck = ck_all[:, MISC_CF + h:MISC_CF + h + 1]
            s = _dot(k_ref[pl.ds(base, size), p * LANES:(p + 1) * LANES], qs[h]) - ck
            return jnp.where(krow <= qcol, s, NEG) if diag else s
        return logits

    def tile_values(j, size):
        base = pl.multiple_of(j * ATT_T, ATT_T)
        return lambda h: vt_ref[h * HEAD_DIM:(h + 1) * HEAD_DIM, pl.ds(base, size)]

    def qbias(h):
        return cq[h:h + 1, :]

    def tile(j, size, diag):
        _heads_pipelined(tile_logits(j, size, diag), tile_values(j, size), m_ref, acc_ref, qbias=qbias)

    far_size = FAR_TILES * ATT_T
    _tiles_skewed(i // FAR_TILES, lambda t: tile_logits(t * FAR_TILES, far_size),
                  lambda t: tile_values(t * FAR_TILES, far_size), m_ref, acc_ref, s_ref, qbias=qbias)
    _far_remainder(i, lambda j, size: tile(j, size, False))
    tile(i, ATT_T, True)
    for p in range(N_PAIRS):
        sl = slice(p * LANES, (p + 1) * LANES)
        o_ref[:, sl] = _gated_pair(p, gate_ref[:, sl], acc_ref)


def _mix_c(qc, kc, z, vt, cum, cum_t, batch, seq):
    n = qc.shape[0]
    t = ATT_T
    nq = seq // t
    return pl.pallas_call(
        _mix_c_kernel,
        grid=(batch, nq),
        in_specs=[pl.BlockSpec((t, 512), lambda b, i: (b * nq + i, 0)),
                  pl.BlockSpec((seq, 512), lambda b, i: (b, 0), pipeline_mode=pl.Buffered(1)),
                  pl.BlockSpec((512, seq), lambda b, i: (VT_C // 512, b), pipeline_mode=pl.Buffered(1)),
                  pl.BlockSpec((seq, LANES), lambda b, i: (b, 0), pipeline_mode=pl.Buffered(1)),
                  pl.BlockSpec((1, N_HEADS, seq), lambda b, i: (b, 0, 0)),
                  pl.BlockSpec((t, 512), lambda b, i: (b * nq + i, OFF_CG // 512))],
        out_specs=pl.BlockSpec((t, 512), lambda b, i: (b * nq + i, 0)),
        out_shape=jax.ShapeDtypeStruct((n, 512), BF16),
        scratch_shapes=[pltpu.VMEM((N_HEADS, 8, t), F32), pltpu.VMEM((N_HEADS, ACC_ROWS, t), F32),
                        pltpu.VMEM((2, N_HEADS, FAR_TILES * t, t), F32)],
        compiler_params=_params(("arbitrary", "arbitrary"), 56),
        name="mix_c",
    )(qc, kc, vt, cum, cum_t, z)


def _merge_kernel(oa, ob, oc, ga, gb, gc, x_ref, wb_ref, wo_ref, o_ref):
    merged = None
    for br, (o, g) in enumerate(((oa, ga), (ob, gb), (oc, gc))):
        y = _dot(o[...], wb_ref[br])
        term = jax.nn.sigmoid(g[...].astype(F32)) * y
        merged = term if merged is None else merged + term
    o_ref[...] = x_ref[...] + _dot(merged.astype(BF16), wo_ref[...])


def _merge(oa, ob, oc, z, x2d, wb, wo):
    n, d = x2d.shape
    tm = 512

    def ospec():
        return pl.BlockSpec((tm, 512), lambda i: (i, 0))

    def gspec(k):
        return pl.BlockSpec((tm, d), lambda i, _k=k: (i, OFF_MERGE // d + _k))

    return pl.pallas_call(
        _merge_kernel,
        grid=(n // tm,),
        in_specs=[ospec(), ospec(), ospec(), gspec(0), gspec(1), gspec(2),
                  pl.BlockSpec((tm, d), lambda i: (i, 0)),
                  pl.BlockSpec((3, 512, d), lambda i: (0, 0, 0)),
                  pl.BlockSpec((d, d), lambda i: (0, 0))],
        out_specs=pl.BlockSpec((tm, d), lambda i: (i, 0)),
        out_shape=jax.ShapeDtypeStruct((n, d), F32),
        compiler_params=_params(("arbitrary",), 48),
        name="merge",
    )(oa, ob, oc, z, z, z, x2d, wb, wo)


def _bucket_lookup(table, bucket):
    out = jnp.zeros(bucket.shape + (table.shape[1],), F32)
    for b in range(N_BUCKETS):
        out = jnp.where((bucket == b)[..., None], table[b].astype(F32), out)
    return out


def _window_table(bias_a):
    ki = jnp.arange(2 * WINDOW)[:, None]
    qi = jnp.arange(WINDOW)[None, :]
    delta = qi + WINDOW - ki
    band = (delta >= 0) & (delta < WINDOW)
    bias = _bucket_lookup(bias_a, _t5_bucket(delta)).transpose(2, 0, 1) * LOG2E
    return jnp.where(band[None], bias, NEG)


def _near_table(bias_b):
    ki = jnp.arange(ATT_T)[:, None]
    qi = jnp.arange(ATT_T)[None, :]
    d = jnp.arange(B_NEAR)[:, None, None]
    delta = d * ATT_T + qi[None] - ki[None]
    bias = _bucket_lookup(bias_b, _t5_bucket(delta))
    far = bias_b[N_BUCKETS - 1].astype(F32)
    return ((bias - far) * LOG2E).transpose(3, 0, 1, 2)


def kernel(x, norm_gain, w_in, b_forget, qk_gain, sinks, w_branch, w_out, rel_bias):
    batch, seq, d = x.shape
    depth = norm_gain.shape[0]
    assert w_in.shape[-1] == IN_COLS and seq % 1024 == 0 and d == 1024
    assert (B_NEAR - 1) * ATT_T - (ATT_T - 1) < T5_FAR <= B_NEAR * ATT_T - (ATT_T - 1)

    w_perm = _gather_cols(w_in, _COLS).astype(BF16)
    w_vt = jnp.swapaxes(_gather_cols(w_in, _VT_COLS), 1, 2).astype(BF16)
    w_st = jnp.swapaxes(_gather_cols(w_in, _ST_COLS), 1, 2).astype(BF16)
    wb = w_branch.astype(BF16)
    wo = w_out.astype(BF16)

    gmat = jnp.asarray(np.kron(np.eye(N_HEADS), np.ones((HEAD_DIM, HEAD_DIM))), BF16)
    prep_t = 512
    lmat = jnp.asarray(np.tril(np.ones((prep_t, prep_t))), BF16)
    lmat_sel = lmat[:SEL_MC, :SEL_MC]
    tbl_a = _window_table(rel_bias[:, :N_HEADS])
    tbl_b = _near_table(rel_bias[:, N_HEADS:])

    x2d = x.reshape(batch * seq, d)
    for layer in range(depth):
        g = qk_gain[layer]
        qscale = ATTN_SCALE * LOG2E
        gains = jnp.zeros((8, 512), F32)
        gains = gains.at[0].set(jnp.tile(g[0, 0], N_HEADS) * qscale)
        gains = gains.at[1, :256].set(jnp.tile(g[0, 1], 4))
        gains = gains.at[2].set(jnp.tile(g[1, 0], N_HEADS) * qscale)
        gains = gains.at[3, :256].set(jnp.tile(g[1, 1], 4))
        gains = gains.at[4].set(jnp.tile(g[2, 0], N_HEADS) * qscale)
        gains = gains.at[5].set(jnp.tile(g[2, 1], N_HEADS))
        gains = gains.at[6, MISC_CF:MISC_CF + N_HEADS].set(b_forget[layer])

        z, misc, vt, st = _proj(x2d, norm_gain[layer][None, :], w_perm[layer], w_vt[layer], w_st[layer])
        qa, ka, qb, kb, qc, kc, cum = _prep(z, misc, gains, gmat, lmat, batch, seq)
        cum_t = cum.reshape(batch, seq, LANES)[:, :, MISC_CF:MISC_CF + N_HEADS].transpose(0, 2, 1)
        oa = _mix_a(qa, ka, z, vt, tbl_a, sinks[layer], batch, seq)
        maskadd = _sel_b(z, st, lmat_sel, batch, seq)
        ob = _mix_b(qb, kb, z, vt, maskadd, tbl_b, batch, seq)
        oc = _mix_c(qc, kc, z, vt, cum, cum_t, batch, seq)
        x2d = _merge(oa, ob, oc, z, x2d, wb[layer], wo[layer])
    return x2d.reshape(batch, seq, d)
```

```python
import functools
import math

import numpy as np
import jax
import jax.numpy as jnp
from jax import lax
from jax.experimental import pallas as pl
from jax.experimental.pallas import tpu as pltpu

F32 = jnp.float32
BF16 = jnp.bfloat16

HEAD_DIM = 64
LANES = 128
N_HEADS = 8
N_PAIRS = N_HEADS // 2
KV_GROUP = 4
WINDOW = 128
IDX_HEADS = 4
IDX_DIM = 32
TOPK_MAX = 256
N_BUCKETS = 32
MAX_DISTANCE = 512
EPS = 1e-6
ATTN_SCALE = HEAD_DIM ** -0.5
LOG2E = math.log2(math.e)
NEG = -1e30
T5_FAR = 413

_SIZES = dict(a_q=512, a_k=128, a_v=128, a_gate=512,
              b_q=512, b_k=128, b_v=128, b_iq=128, b_ik=32, b_iw=4, b_gate=512,
              c_q=512, c_k=512, c_v=512, c_f=8, c_gate=512, merge=3072)
_ORIG = {}
_o = 0
for _k, _v in _SIZES.items():
    _ORIG[_k] = _o
    _o += _v
IN_COLS = _o

OFF_MERGE = 0
OFF_AQ, OFF_AG, OFF_BQ, OFF_BG = 3072, 3584, 4096, 4608
OFF_CQ, OFF_CK, OFF_CG = 5120, 5632, 6144
OFF_AK, OFF_BK = 6656, 6912
OFF_BIQ, OFF_BIK, OFF_MISC = 7168, 7296, 7424
NP_COLS = 7680
PROJ_TN = 1536
MISC_CF = 8
VT_C, VT_A, VT_B, VT_ROWS = 0, 512, 640, 768


def _orig(name, idx=None):
    src = np.arange(_SIZES[name]) + _ORIG[name]
    return src if idx is None else src[idx]


def _column_map():
    cols = np.full((NP_COLS,), -1, np.int64)

    def put(off, src):
        cols[off:off + len(src)] = src

    for off, name in ((OFF_MERGE, "merge"), (OFF_AQ, "a_q"), (OFF_AG, "a_gate"), (OFF_BQ, "b_q"),
                      (OFF_BG, "b_gate"), (OFF_CQ, "c_q"), (OFF_CK, "c_k"), (OFF_CG, "c_gate"),
                      (OFF_BIQ, "b_iq")):
        put(off, _orig(name))
    dup = np.concatenate([np.arange(64), np.arange(64), np.arange(64, 128), np.arange(64, 128)])
    put(OFF_AK, _orig("a_k", dup))
    put(OFF_BK, _orig("b_k", dup))
    put(OFF_BIK, _orig("b_ik", np.tile(np.arange(IDX_DIM), IDX_HEADS)))
    put(OFF_MISC + MISC_CF, _orig("c_f"))
    return cols


def _gather_cols(w, cols):
    parts = []
    start = 0
    while start < len(cols):
        end = start + 1
        if cols[start] < 0:
            while end < len(cols) and cols[end] < 0:
                end += 1
            parts.append(jnp.zeros(w.shape[:-1] + (end - start,), w.dtype))
        else:
            while end < len(cols) and cols[end] == cols[end - 1] + 1:
                end += 1
            parts.append(w[..., int(cols[start]):int(cols[end - 1]) + 1])
        start = end
    return jnp.concatenate(parts, axis=-1)


_COLS = _column_map()
_VT_COLS = np.concatenate([_orig("c_v"), _orig("a_v"), _orig("b_v")])
_ST_COLS = np.concatenate([_orig("b_iw"), np.full((4,), -1, np.int64)])


def _params(sem, vmem_mb):
    return pltpu.CompilerParams(dimension_semantics=sem, vmem_limit_bytes=vmem_mb * 1024 * 1024)


def _dot(a, b):
    return jnp.dot(a, b, preferred_element_type=F32)


def _dot_nt(a, b):
    return lax.dot_general(a, b, (((1,), (1,)), ((), ())), preferred_element_type=F32)


def _t5_bucket(delta):
    n = jnp.maximum(delta, 0)
    max_exact = N_BUCKETS // 2
    nf = jnp.maximum(n, 1).astype(F32)
    large = max_exact + (jnp.log(nf / max_exact) / math.log(MAX_DISTANCE / max_exact)
                         * (N_BUCKETS - max_exact)).astype(jnp.int32)
    large = jnp.minimum(large, N_BUCKETS - 1)
    return jnp.where(n < max_exact, n, large)


def _proj_kernel(x_ref, g_ref, w_ref, wvt_ref, wst_ref, z_ref, misc_ref, vt_ref, st_ref, h_ref,
                 *, misc_tile, misc_local):
    j = pl.program_id(1)

    @pl.when(j == 0)
    def _():
        x = x_ref[...]
        ms = jnp.mean(x * x, axis=-1, keepdims=True)
        h = (x * lax.rsqrt(ms + EPS) * g_ref[...]).astype(BF16)
        h_ref[...] = h
        vt_ref[...] = _dot_nt(wvt_ref[...], h).astype(BF16)
        st_ref[...] = _dot_nt(wst_ref[...], h)

    acc = _dot(h_ref[...], w_ref[...])
    z_ref[...] = acc.astype(BF16)

    @pl.when(j == misc_tile)
    def _():
        misc_ref[...] = acc[:, misc_local:misc_local + LANES]


def _proj(x2d, gain, w, wvt, wst):
    n, d = x2d.shape
    tm = 1024
    return pl.pallas_call(
        functools.partial(_proj_kernel, misc_tile=OFF_MISC // PROJ_TN, misc_local=OFF_MISC % PROJ_TN),
        grid=(n // tm, NP_COLS // PROJ_TN),
        in_specs=[pl.BlockSpec((tm, d), lambda i, j: (i, 0)),
                  pl.BlockSpec((1, d), lambda i, j: (0, 0)),
                  pl.BlockSpec((d, PROJ_TN), lambda i, j: (0, j)),
                  pl.BlockSpec((VT_ROWS, d), lambda i, j: (0, 0)),
                  pl.BlockSpec((8, d), lambda i, j: (0, 0))],
        out_specs=[pl.BlockSpec((tm, PROJ_TN), lambda i, j: (i, j)),
                   pl.BlockSpec((tm, LANES), lambda i, j: (i, 0)),
                   pl.BlockSpec((VT_ROWS, tm), lambda i, j: (0, i)),
                   pl.BlockSpec((8, tm), lambda i, j: (0, i))],
        out_shape=[jax.ShapeDtypeStruct((n, NP_COLS), BF16),
                   jax.ShapeDtypeStruct((n, LANES), F32),
                   jax.ShapeDtypeStruct((VT_ROWS, n), BF16),
                   jax.ShapeDtypeStruct((8, n), F32)],
        scratch_shapes=[pltpu.VMEM((tm, d), BF16)],
        compiler_params=_params(("arbitrary", "arbitrary"), 56),
        name="proj",
    )(x2d, gain, w, wvt, wst)


def _group_rms(x_bf16, g_ref, gain_row):
    x = x_bf16.astype(F32)
    width = x.shape[-1]
    ss = _dot((x * x).astype(BF16), g_ref[:width, :width])
    return (x * lax.rsqrt(ss * (1.0 / HEAD_DIM) + EPS) * gain_row).astype(BF16)


def _prep_kernel(aq, ak, bq, bk, cq, ck, misc, gains, gmat, lmat,
                 qa_o, ka_o, qb_o, kb_o, qc_o, kc_o, cum_o, carry):
    t = pl.program_id(1)
    qa_o[...] = _group_rms(aq[...], gmat, gains[0:1, :])
    ka_o[...] = _group_rms(ak[...], gmat, gains[1:2, :256])
    qb_o[...] = _group_rms(bq[...], gmat, gains[2:3, :])
    kb_o[...] = _group_rms(bk[...], gmat, gains[3:4, :256])
    qc_o[...] = _group_rms(cq[...], gmat, gains[4:5, :])
    kc_o[...] = _group_rms(ck[...], gmat, gains[5:6, :])

    @pl.when(t == 0)
    def _():
        carry[...] = jnp.zeros_like(carry)

    xm = misc[...] + gains[6:7, :LANES]
    logf = jnp.minimum(xm, 0.0) - jnp.log(1.0 + jnp.exp(-jnp.abs(xm)))
    hi = logf.astype(BF16)
    r1 = logf - hi.astype(F32)
    mid = r1.astype(BF16)
    lo = (r1 - mid.astype(F32)).astype(BF16)
    lm = lmat[...]
    c = _dot(lm, hi) + _dot(lm, mid) + _dot(lm, lo) + carry[...]
    cum_o[...] = c * LOG2E
    rows = c.shape[0]
    carry[...] = c[rows - 1:rows, :]


def _prep(z, misc, gains, gmat, lmat, batch, seq):
    n = z.shape[0]
    tp = lmat.shape[0]
    nt = seq // tp

    def zspec(width, off):
        return pl.BlockSpec((tp, width), lambda b, t, _c=off // width: (b * nt + t, _c))

    def ospec(width):
        return pl.BlockSpec((tp, width), lambda b, t: (b * nt + t, 0))

    return pl.pallas_call(
        _prep_kernel,
        grid=(batch, nt),
        in_specs=[zspec(512, OFF_AQ), zspec(256, OFF_AK), zspec(512, OFF_BQ), zspec(256, OFF_BK),
                  zspec(512, OFF_CQ), zspec(512, OFF_CK), ospec(LANES),
                  pl.BlockSpec((8, 512), lambda b, t: (0, 0)),
                  pl.BlockSpec((512, 512), lambda b, t: (0, 0)),
                  pl.BlockSpec((tp, tp), lambda b, t: (0, 0))],
        out_specs=[ospec(512), ospec(256), ospec(512), ospec(256), ospec(512), ospec(512), ospec(LANES)],
        out_shape=[jax.ShapeDtypeStruct((n, 512), BF16), jax.ShapeDtypeStruct((n, 256), BF16),
                   jax.ShapeDtypeStruct((n, 512), BF16), jax.ShapeDtypeStruct((n, 256), BF16),
                   jax.ShapeDtypeStruct((n, 512), BF16), jax.ShapeDtypeStruct((n, 512), BF16),
                   jax.ShapeDtypeStruct((n, LANES), F32)],
        scratch_shapes=[pltpu.VMEM((1, LANES), F32)],
        compiler_params=_params(("arbitrary", "arbitrary"), 48),
        name="prep",
    )(z, z, z, z, z, z, misc, gains, gmat, lmat)


def _head_qt(q_ref, h):
    p = h // 2
    qt = q_ref[:, p * LANES:(p + 1) * LANES].astype(F32).T
    row = lax.broadcasted_iota(jnp.int32, qt.shape, 0)
    own = (row < HEAD_DIM) if h % 2 == 0 else (row >= HEAD_DIM)
    return jnp.where(own, qt, 0.0).astype(BF16)


DEN_ROWS = 16
ACC_ROWS = HEAD_DIM + DEN_ROWS


def _with_ones(vt):
    return jnp.concatenate([vt, jnp.ones((DEN_ROWS, vt.shape[1]), vt.dtype)], axis=0)


def _softmax_step(h, s, vt, m_ref, acc_ref, qbias=None):
    m_prev = m_ref[h]
    s_of = s if callable(s) else (lambda: s)
    s_max = jnp.max(s_of(), axis=0, keepdims=True)
    m_new = jnp.maximum(m_prev, s_max if qbias is None else s_max + qbias)
    alpha = jnp.exp2(m_prev - m_new)
    shift = m_new[0:1, :] if qbias is None else m_new[0:1, :] - qbias
    p = jnp.exp2((s_of() - shift).astype(BF16))
    acc_ref[h] = alpha[0:1, :] * acc_ref[h] + _dot(_with_ones(vt), p)
    m_ref[h] = m_new


HEADS_AHEAD = 4


def _heads_pipelined(logits, values, m_ref, acc_ref, qbias=None):
    ahead = [logits(h) for h in range(HEADS_AHEAD)]
    for h in range(N_HEADS):
        if h + HEADS_AHEAD < N_HEADS:
            ahead.append(logits(h + HEADS_AHEAD))
        _softmax_step(h, ahead.pop(0), values(h), m_ref, acc_ref,
                      None if qbias is None else qbias(h))


SKEW_AHEAD = 1


def _tiles_skewed(n_tiles, tile_logits, tile_values, m_ref, acc_ref, s_ref, qbias=None):
    bias = (lambda h: None) if qbias is None else qbias

    def produce(t, buf):
        logits = tile_logits(t)
        for h in range(N_HEADS):
            s_ref[buf, h] = logits(h)

    def consume(t, buf):
        values = tile_values(t)
        for h in range(N_HEADS):
            _softmax_step(h, lambda h=h: s_ref[buf, h], values(h), m_ref, acc_ref, bias(h))

    def both(t, buf):
        logits, values = tile_logits(t + 1), tile_values(t)
        for h in range(SKEW_AHEAD):
            s_ref[1 - buf, h] = logits(h)
        for h in range(N_HEADS):
            if h + SKEW_AHEAD < N_HEADS:
                s_ref[1 - buf, h + SKEW_AHEAD] = logits(h + SKEW_AHEAD)
            _softmax_step(h, lambda h=h: s_ref[buf, h], values(h), m_ref, acc_ref, bias(h))

    pairs = jnp.maximum(n_tiles - 1, 0) // 2
    rem = n_tiles - 2 * pairs

    @pl.when(n_tiles > 0)
    def _():
        produce(0, 0)

    def trip(p, carry):
        both(2 * p, 0)
        both(2 * p + 1, 1)
        return carry

    lax.fori_loop(0, pairs, trip, 0)

    @pl.when(rem == 2)
    def _():
        both(2 * pairs, 0)
        consume(2 * pairs + 1, 1)

    @pl.when(rem == 1)
    def _():
        consume(2 * pairs, 0)


def _init_state(m_ref, acc_ref):
    m_ref[...] = jnp.full(m_ref.shape, NEG, F32)
    acc_ref[...] = jnp.zeros(acc_ref.shape, F32)


def _gated_pair(p, gate, acc_ref):
    halves = []
    for h in (2 * p, 2 * p + 1):
        a = acc_ref[h]
        halves.append(a[:HEAD_DIM] / a[HEAD_DIM:HEAD_DIM + 1])
    ot = jnp.concatenate(halves, axis=0)
    g = gate.astype(F32)
    return (ot.T * (g * jax.nn.sigmoid(g))).astype(BF16)


def _mix_a_kernel(sink_ref, q_ref, kp_ref, kc_ref, vp_ref, vc_ref, gate_ref, tbl_ref, o_ref, acc_ref):
    n = pl.program_id(1)
    pad_pen = jnp.where(n > 0, 0.0, NEG).astype(F32)
    den_rows = lax.broadcasted_iota(jnp.int32, (ACC_ROWS, WINDOW), 0) >= HEAD_DIM

    def logits(h):
        qt = _head_qt(q_ref, h)
        ksl = slice((h // KV_GROUP) * LANES, (h // KV_GROUP + 1) * LANES)
        return (_dot(kp_ref[:, ksl], qt) + tbl_ref[h, :WINDOW, :] + pad_pen,
                _dot(kc_ref[:, ksl], qt) + tbl_ref[h, WINDOW:, :])

    all_logits = [logits(h) for h in range(N_HEADS)]
    for h in range(N_HEADS):
        kv = h // KV_GROUP
        vsl = slice(kv * HEAD_DIM, (kv + 1) * HEAD_DIM)
        s_prev, s_cur = all_logits[h]
        sink = sink_ref[h] * LOG2E
        m = jnp.maximum(jnp.maximum(jnp.max(s_prev, axis=0, keepdims=True),
                                    jnp.max(s_cur, axis=0, keepdims=True)), sink)
        p_prev = jnp.exp2((s_prev - m).astype(BF16))
        p_cur = jnp.exp2((s_cur - m).astype(BF16))
        acc = _dot(_with_ones(vp_ref[vsl, :]), p_prev) + _dot(_with_ones(vc_ref[vsl, :]), p_cur)
        acc_ref[h] = acc + jnp.where(den_rows, jnp.exp2(sink - m), 0.0)
    for p in range(N_PAIRS):
        sl = slice(p * LANES, (p + 1) * LANES)
        o_ref[:, sl] = _gated_pair(p, gate_ref[:, sl], acc_ref)


def _mix_a(qa, ka, z, vt, tbl, sinks, batch, seq):
    n = qa.shape[0]
    t = WINDOW
    nb = seq // t
    return pl.pallas_call(
        _mix_a_kernel,
        grid=(batch, nb),
        in_specs=[pl.BlockSpec(memory_space=pltpu.SMEM),
                  pl.BlockSpec((t, 512), lambda b, i: (b * nb + i, 0)),
                  pl.BlockSpec((t, 256), lambda b, i: (b * nb + jnp.maximum(i - 1, 0), 0)),
                  pl.BlockSpec((t, 256), lambda b, i: (b * nb + i, 0)),
                  pl.BlockSpec((LANES, t), lambda b, i: (VT_A // LANES, b * nb + jnp.maximum(i - 1, 0))),
                  pl.BlockSpec((LANES, t), lambda b, i: (VT_A // LANES, b * nb + i)),
                  pl.BlockSpec((t, 512), lambda b, i: (b * nb + i, OFF_AG // 512)),
                  pl.BlockSpec((N_HEADS, 2 * t, t), lambda b, i: (0, 0, 0))],
        out_specs=pl.BlockSpec((t, 512), lambda b, i: (b * nb + i, 0)),
        out_shape=jax.ShapeDtypeStruct((n, 512), BF16),
        scratch_shapes=[pltpu.VMEM((N_HEADS, ACC_ROWS, t), F32)],
        compiler_params=_params(("arbitrary", "arbitrary"), 32),
        name="mix_a",
    )(sinks, qa, ka, ka, vt, vt, z, tbl)


SEL_TQ = 256
SEL_SC = 512
SEL_SUB = 64
SEL_MC = 256
SEL_FAST_ITERS = 24
SEL_MAX_ITERS = 600


def _sel_b_kernel(iq_ref, ik_ref, wt_ref, lmat_ref, o_ref, sc_ref, raw_ref, *, topk):
    n = pl.program_id(1)
    t0 = n * SEL_TQ
    n_full = n // (SEL_SC // SEL_TQ)
    n_sc = n_full + 1
    n_mc = n + 1
    kf = float(topk)

    iqt = iq_ref[...].astype(F32).T
    row = lax.broadcasted_iota(jnp.int32, iqt.shape, 0)
    qst = jnp.concatenate(
        [jnp.where((row >= h * IDX_DIM) & (row < (h + 1) * IDX_DIM), iqt, 0.0)
         for h in range(IDX_HEADS)], axis=1).astype(BF16)
    wscale = (IDX_HEADS ** -0.5) * (IDX_DIM ** -0.5)
    w = [wt_ref[h:h + 1, :] * wscale for h in range(IDX_HEADS)]
    qpos = t0 + lax.broadcasted_iota(jnp.int32, (SEL_SUB, SEL_TQ), 1)
    kiota = lax.broadcasted_iota(jnp.int32, (SEL_SUB, SEL_TQ), 0)

    def raw_into(c, buf):
        base = pl.multiple_of(c * SEL_SC, SEL_SC)
        raw_ref[buf] = _dot(ik_ref[pl.ds(base, SEL_SC), :], qst)

    def score_chunk(c, buf, carry, diagonal):
        rmax, rmin, ge0, gt0, minpos = carry
        base = pl.multiple_of(c * SEL_SC, SEL_SC)
        for u in range(SEL_SC // SEL_SUB):
            rows = slice(u * SEL_SUB, (u + 1) * SEL_SUB)
            score = w[0] * jnp.maximum(raw_ref[buf, rows, 0:SEL_TQ], 0.0)
            for h in range(1, IDX_HEADS):
                score = score + w[h] * jnp.maximum(raw_ref[buf, rows, h * SEL_TQ:(h + 1) * SEL_TQ], 0.0)
            rmin = jnp.minimum(rmin, score)
            if diagonal:
                score = jnp.where((kiota + (base + u * SEL_SUB)) <= qpos, score, -jnp.inf)
            sc_ref[pl.ds(base + u * SEL_SUB, SEL_SUB), :] = score
            rmax = jnp.maximum(rmax, score)
            pos = score > 0.0
            ge0 = ge0 + jnp.where(score >= 0.0, 1.0, 0.0)
            gt0 = gt0 + jnp.where(pos, 1.0, 0.0)
            minpos = jnp.minimum(minpos, jnp.where(pos, score, jnp.inf))
        return rmax, rmin, ge0, gt0, minpos

    def slab(v):
        return jnp.full((SEL_SUB, SEL_TQ), v, F32)

    def score_trip(p, carry):
        raw_into(2 * p + 1, 1)
        carry = score_chunk(2 * p, 0, carry, False)
        raw_into(2 * p + 2, 0)
        return score_chunk(2 * p + 1, 1, carry, False)

    def tail_two(carry):
        raw_into(n_full, 1)
        carry = score_chunk(n_full - 1, 0, carry, False)
        return score_chunk(n_full, 1, carry, True)

    def tail_one(carry):
        return score_chunk(n_full, 0, carry, True)

    raw_into(0, 0)
    carry = lax.fori_loop(0, n_full // 2, score_trip,
                          (slab(-jnp.inf), slab(jnp.inf), slab(0.0), slab(0.0), slab(jnp.inf)))
    rmax, rmin, ge0, gt0, minpos = lax.cond(n_full % 2 == 1, tail_two, tail_one, carry)
    rmax = jnp.max(rmax, axis=0, keepdims=True)
    rmin = jnp.min(rmin, axis=0, keepdims=True)
    ge0 = jnp.sum(ge0, axis=0, keepdims=True)
    gt0 = jnp.sum(gt0, axis=0, keepdims=True)
    minpos = jnp.min(minpos, axis=0, keepdims=True)

    navail = (t0 + 1 + lax.broadcasted_iota(jnp.int32, (1, SEL_TQ), 1)).astype(F32)

    def sweep(mid, snap):
        midb = jnp.broadcast_to(mid, (SEL_SUB, SEL_TQ))

        def body(c, carry):
            base = pl.multiple_of(c * SEL_SC, SEL_SC)
            cnt, up, dn = carry
            for u in range(SEL_SC // SEL_SUB):
                x = sc_ref[pl.ds(base + u * SEL_SUB, SEL_SUB), :]
                ge = x >= midb
                cnt = cnt + jnp.where(ge, 1.0, 0.0)
                if snap:
                    up = jnp.minimum(up, jnp.where(ge, x, jnp.inf))
                    dn = jnp.maximum(dn, jnp.where(ge, -jnp.inf, x))
            return cnt, up, dn

        init = (jnp.zeros((SEL_SUB, SEL_TQ), F32), jnp.full((SEL_SUB, SEL_TQ), jnp.inf, F32),
                jnp.full((SEL_SUB, SEL_TQ), -jnp.inf, F32))
        cnt, up, dn = lax.fori_loop(0, n_sc, body, init)
        return (jnp.sum(cnt, axis=0, keepdims=True), jnp.min(up, axis=0, keepdims=True),
                jnp.max(dn, axis=0, keepdims=True))

    def advance(st, mid, snap):
        it, lo, hi, hi_dn, c_lo, c_hi, done = st
        cnt, up, dn = sweep(mid, snap)
        live = jnp.where(done > 0.5, 0.0, jnp.where(mid <= lo, 0.0, jnp.where(mid >= hi, 0.0, 1.0)))
        ge = jnp.where(cnt >= kf, live, 0.0) > 0.5
        lt = jnp.where(cnt >= kf, 0.0, live) > 0.5
        lo = jnp.where(ge, up if snap else mid, lo)
        c_lo = jnp.where(ge, cnt, c_lo)
        hi = jnp.where(lt, mid, hi)
        hi_dn = jnp.where(lt, dn if snap else jnp.inf, hi_dn)
        c_hi = jnp.where(lt, cnt, c_hi)
        done = jnp.where(live < 0.5, 1.0, jnp.where(cnt == kf, 1.0, jnp.where(hi_dn <= lo, 1.0, 0.0)))
        return it + 1, lo, hi, hi_dn, c_lo, c_hi, done

    def make_step(snap, repeat=1):
        def step(st):
            for _ in range(repeat):
                st = advance(st, 0.5 * st[1] + 0.5 * st[2], snap)
            return st
        return step

    def make_cond(limit):
        def cond(st):
            return jnp.logical_and(st[0] < limit, jnp.min(st[6]) < 0.5)
        return cond

    inf_row = jnp.full((1, SEL_TQ), jnp.inf, F32)
    all_in = navail <= kf
    above = jnp.logical_and(jnp.logical_not(all_in), gt0 >= kf)
    at_zero = jnp.logical_and(jnp.logical_not(all_in), jnp.logical_and(gt0 < kf, ge0 >= kf))
    below = jnp.logical_and(jnp.logical_not(all_in), ge0 < kf)
    lo = jnp.where(above, minpos, jnp.where(at_zero, 0.0, rmin))
    c_lo = jnp.where(above, gt0, jnp.where(at_zero, ge0, navail))
    hi = jnp.where(at_zero, minpos, jnp.where(below, 0.0, inf_row))
    c_hi = jnp.where(at_zero, gt0, jnp.where(below, ge0, 0.0))
    hi_dn = jnp.where(above, rmax, inf_row)
    done = jnp.where(all_in, 1.0, jnp.where(at_zero, 1.0, 0.0))
    st = (jnp.int32(0), lo, hi, hi_dn, c_lo, c_hi, done)
    st = advance(st, jnp.where(above, rmax, 0.5 * lo + 0.5 * hi), False)
    st = lax.while_loop(make_cond(SEL_FAST_ITERS), make_step(False, repeat=2), st)
    st = lax.while_loop(make_cond(SEL_MAX_ITERS), make_step(True), st)
    _, lo, hi, _, c_lo, c_hi, _ = st
    need = kf - c_hi
    ties = jnp.max(jnp.where(navail > kf, c_lo, kf)) > kf

    def fill_body(c, carry):
        base = pl.multiple_of(c * SEL_MC, SEL_MC)
        o_ref[0, pl.ds(base, SEL_MC), :] = jnp.full((SEL_MC, SEL_TQ), NEG, o_ref.dtype)
        return carry
    lax.fori_loop(n_mc, o_ref.shape[1] // SEL_MC, fill_body, 0)

    @pl.when(jnp.logical_not(ties))
    def _():
        def mask_body(c, carry):
            base = pl.multiple_of(c * SEL_MC, SEL_MC)
            x = sc_ref[pl.ds(base, SEL_MC), :]
            o_ref[0, pl.ds(base, SEL_MC), :] = jnp.where(x >= lo, 0.0, NEG).astype(o_ref.dtype)
            return carry
        lax.fori_loop(0, n_mc, mask_body, 0)

    @pl.when(ties)
    def _():
        lmat = lmat_ref[...]

        def mask_body(c, run):
            base = pl.multiple_of(c * SEL_SC, SEL_SC)
            xs, ge_los, lt_his, counts = [], [], [], []
            for u in range(SEL_SC // SEL_MC):
                x = sc_ref[pl.ds(base + u * SEL_MC, SEL_MC), :]
                ge_lo = x >= lo
                lt_hi = x < hi
                bnd = jnp.where(ge_lo, jnp.where(lt_hi, 1.0, 0.0), 0.0)
                xs.append(x)
                ge_los.append(ge_lo)
                lt_his.append(lt_hi)
                counts.append(_dot(lmat, bnd.astype(BF16)))
            for u in range(SEL_SC // SEL_MC):
                incl = counts[u] + run
                inside = jnp.where(lt_his[u], jnp.where(incl <= need, 0.0, NEG), 0.0)
                o_ref[0, pl.ds(base + u * SEL_MC, SEL_MC), :] = (
                    jnp.where(ge_los[u], inside, NEG).astype(o_ref.dtype))
                run = incl[SEL_MC - 1:SEL_MC, :]
            return run
        lax.fori_loop(0, n_sc, mask_body, jnp.zeros((1, SEL_TQ), F32))


def _sel_b(z, st, lmat, batch, seq):
    nb = seq // SEL_TQ
    topk = min(TOPK_MAX, seq // 4)
    return pl.pallas_call(
        functools.partial(_sel_b_kernel, topk=topk),
        grid=(batch, nb),
        in_specs=[pl.BlockSpec((SEL_TQ, LANES), lambda b, i: (b * nb + i, OFF_BIQ // LANES)),
                  pl.BlockSpec((seq, LANES), lambda b, i: (b, OFF_BIK // LANES)),
                  pl.BlockSpec((8, SEL_TQ), lambda b, i: (0, b * nb + i)),
                  pl.BlockSpec((SEL_MC, SEL_MC), lambda b, i: (0, 0))],
        out_specs=pl.BlockSpec((1, seq, SEL_TQ), lambda b, i: (b, 0, i)),
        out_shape=jax.ShapeDtypeStruct((batch, seq, seq), BF16),
        scratch_shapes=[pltpu.VMEM((seq, SEL_TQ), F32),
                        pltpu.VMEM((2, SEL_SC, IDX_HEADS * SEL_TQ), F32)],
        compiler_params=_params(("arbitrary", "arbitrary"), 48),
        name="sel_b",
    )(z, z, st, lmat)


ATT_T = 256
B_NEAR = 3
FAR_TILES = 2


def _far_remainder(n_tiles, tile):
    done = (n_tiles // FAR_TILES) * FAR_TILES
    rem = n_tiles - done
    size = FAR_TILES // 2
    while size >= 1:
        @pl.when((rem // size) % 2 == 1)
        def _(size=size, start=done + (rem // (2 * size)) * (2 * size)):
            tile(start, size * ATT_T)
        size //= 2


def _mix_b_kernel(q_ref, k_ref, vt_ref, mask_ref, gate_ref, tbl_ref, o_ref, m_ref, acc_ref, s_ref):
    i = pl.program_id(1)
    _init_state(m_ref, acc_ref)
    qs = [_head_qt(q_ref, h) for h in range(N_HEADS)]

    def tile_logits(j, size, near=None):
        base = pl.multiple_of(j * ATT_T, ATT_T)
        madd = mask_ref[0, pl.ds(base, size), :].astype(F32)

        def logits(h):
            kv = h // KV_GROUP
            s = _dot(k_ref[pl.ds(base, size), kv * LANES:(kv + 1) * LANES], qs[h]) + madd
            return s if near is None else s + tbl_ref[h, near]
        return logits

    def tile_values(j, size):
        base = pl.multiple_of(j * ATT_T, ATT_T)

        def values(h):
            kv = h // KV_GROUP
            return vt_ref[kv * HEAD_DIM:(kv + 1) * HEAD_DIM, pl.ds(base, size)]
        return values

    def tile(j, size, near):
        _heads_pipelined(tile_logits(j, size, near), tile_values(j, size), m_ref, acc_ref)

    n_far = jnp.maximum(i - (B_NEAR - 1), 0)
    far_size = FAR_TILES * ATT_T
    _tiles_skewed(n_far // FAR_TILES, lambda t: tile_logits(t * FAR_TILES, far_size),
                  lambda t: tile_values(t * FAR_TILES, far_size), m_ref, acc_ref, s_ref)
    _far_remainder(n_far, lambda j, size: tile(j, size, None))

    for d in range(B_NEAR - 1, -1, -1):
        @pl.when(i >= d)
        def _(d=d):
            tile(i - d, ATT_T, d)
    for p in range(N_PAIRS):
        sl = slice(p * LANES, (p + 1) * LANES)
        o_ref[:, sl] = _gated_pair(p, gate_ref[:, sl], acc_ref)


def _mix_b(qb, kb, z, vt, maskadd, tbl, batch, seq):
    n = qb.shape[0]
    t = ATT_T
    nq = seq // t
    return pl.pallas_call(
        _mix_b_kernel,
        grid=(batch, nq),
        in_specs=[pl.BlockSpec((t, 512), lambda b, i: (b * nq + i, 0)),
                  pl.BlockSpec((seq, 256), lambda b, i: (b, 0), pipeline_mode=pl.Buffered(1)),
                  pl.BlockSpec((LANES, seq), lambda b, i: (VT_B // LANES, b), pipeline_mode=pl.Buffered(1)),
                  pl.BlockSpec((1, seq, t), lambda b, i: (b, 0, i)),
                  pl.BlockSpec((t, 512), lambda b, i: (b * nq + i, OFF_BG // 512)),
                  pl.BlockSpec((N_HEADS, B_NEAR, t, t), lambda b, i: (0, 0, 0, 0),
                               pipeline_mode=pl.Buffered(1))],
        out_specs=pl.BlockSpec((t, 512), lambda b, i: (b * nq + i, 0)),
        out_shape=jax.ShapeDtypeStruct((n, 512), BF16),
        scratch_shapes=[pltpu.VMEM((N_HEADS, 8, t), F32), pltpu.VMEM((N_HEADS, ACC_ROWS, t), F32),
                        pltpu.VMEM((2, N_HEADS, FAR_TILES * t, t), F32)],
        compiler_params=_params(("arbitrary", "arbitrary"), 56),
        name="mix_b",
    )(qb, kb, vt, maskadd, z, tbl)


def _mix_c_kernel(q_ref, k_ref, vt_ref, ck_ref, cq_ref, gate_ref, o_ref, m_ref, acc_ref, s_ref):
    i = pl.program_id(1)
    _init_state(m_ref, acc_ref)
    qs = [_head_qt(q_ref, h) for h in range(N_HEADS)]
    q0 = pl.multiple_of(i * ATT_T, ATT_T)
    cq = cq_ref[0, :, pl.ds(q0, ATT_T)]
    krow = lax.broadcasted_iota(jnp.int32, (ATT_T, ATT_T), 0)
    qcol = lax.broadcasted_iota(jnp.int32, (ATT_T, ATT_T), 1)

    def tile_logits(j, size, diag=False):
        base = pl.multiple_of(j * ATT_T, ATT_T)
        ck_all = ck_ref[pl.ds(base, size), :]

        def logits(h):
            p = h // 2
            ck = ck_all[:, MISC_CF + h:MISC_CF + h + 1]
            s = _dot(k_ref[pl.ds(base, size), p * LANES:(p + 1) * LANES], qs[h]) - ck
            return jnp.where(krow <= qcol, s, NEG) if diag else s
        return logits

    def tile_values(j, size):
        base = pl.multiple_of(j * ATT_T, ATT_T)
        return lambda h: vt_ref[h * HEAD_DIM:(h + 1) * HEAD_DIM, pl.ds(base, size)]

    def qbias(h):
        return cq[h:h + 1, :]

    def tile(j, size, diag):
        _heads_pipelined(tile_logits(j, size, diag), tile_values(j, size), m_ref, acc_ref, qbias=qbias)

    far_size = FAR_TILES * ATT_T
    _tiles_skewed(i // FAR_TILES, lambda t: tile_logits(t * FAR_TILES, far_size),
                  lambda t: tile_values(t * FAR_TILES, far_size), m_ref, acc_ref, s_ref, qbias=qbias)
    _far_remainder(i, lambda j, size: tile(j, size, False))
    tile(i, ATT_T, True)
    for p in range(N_PAIRS):
        sl = slice(p * LANES, (p + 1) * LANES)
        o_ref[:, sl] = _gated_pair(p, gate_ref[:, sl], acc_ref)


def _mix_c(qc, kc, z, vt, cum, cum_t, batch, seq):
    n = qc.shape[0]
    t = ATT_T
    nq = seq // t
    return pl.pallas_call(
        _mix_c_kernel,
        grid=(batch, nq),
        in_specs=[pl.BlockSpec((t, 512), lambda b, i: (b * nq + i, 0)),
                  pl.BlockSpec((seq, 512), lambda b, i: (b, 0), pipeline_mode=pl.Buffered(1)),
                  pl.BlockSpec((512, seq), lambda b, i: (VT_C // 512, b), pipeline_mode=pl.Buffered(1)),
                  pl.BlockSpec((seq, LANES), lambda b, i: (b, 0), pipeline_mode=pl.Buffered(1)),
                  pl.BlockSpec((1, N_HEADS, seq), lambda b, i: (b, 0, 0)),
                  pl.BlockSpec((t, 512), lambda b, i: (b * nq + i, OFF_CG // 512))],
        out_specs=pl.BlockSpec((t, 512), lambda b, i: (b * nq + i, 0)),
        out_shape=jax.ShapeDtypeStruct((n, 512), BF16),
        scratch_shapes=[pltpu.VMEM((N_HEADS, 8, t), F32), pltpu.VMEM((N_HEADS, ACC_ROWS, t), F32),
                        pltpu.VMEM((2, N_HEADS, FAR_TILES * t, t), F32)],
        compiler_params=_params(("arbitrary", "arbitrary"), 56),
        name="mix_c",
    )(qc, kc, vt, cum, cum_t, z)


def _merge_kernel(oa, ob, oc, ga, gb, gc, x_ref, wb_ref, wo_ref, o_ref):
    merged = None
    for br, (o, g) in enumerate(((oa, ga), (ob, gb), (oc, gc))):
        y = _dot(o[...], wb_ref[br])
        term = jax.nn.sigmoid(g[...].astype(F32)) * y
        merged = term if merged is None else merged + term
    o_ref[...] = x_ref[...] + _dot(merged.astype(BF16), wo_ref[...])


def _merge(oa, ob, oc, z, x2d, wb, wo):
    n, d = x2d.shape
    tm = 512

    def ospec():
        return pl.BlockSpec((tm, 512), lambda i: (i, 0))

    def gspec(k):
        return pl.BlockSpec((tm, d), lambda i, _k=k: (i, OFF_MERGE // d + _k))

    return pl.pallas_call(
        _merge_kernel,
        grid=(n // tm,),
        in_specs=[ospec(), ospec(), ospec(), gspec(0), gspec(1), gspec(2),
                  pl.BlockSpec((tm, d), lambda i: (i, 0)),
                  pl.BlockSpec((3, 512, d), lambda i: (0, 0, 0)),
                  pl.BlockSpec((d, d), lambda i: (0, 0))],
        out_specs=pl.BlockSpec((tm, d), lambda i: (i, 0)),
        out_shape=jax.ShapeDtypeStruct((n, d), F32),
        compiler_params=_params(("arbitrary",), 48),
        name="merge",
    )(oa, ob, oc, z, z, z, x2d, wb, wo)


def _bucket_lookup(table, bucket):
    out = jnp.zeros(bucket.shape + (table.shape[1],), F32)
    for b in range(N_BUCKETS):
        out = jnp.where((bucket == b)[..., None], table[b].astype(F32), out)
    return out


def _window_table(bias_a):
    ki = jnp.arange(2 * WINDOW)[:, None]
    qi = jnp.arange(WINDOW)[None, :]
    delta = qi + WINDOW - ki
    band = (delta >= 0) & (delta < WINDOW)
    bias = _bucket_lookup(bias_a, _t5_bucket(delta)).transpose(2, 0, 1) * LOG2E
    return jnp.where(band[None], bias, NEG)


def _near_table(bias_b):
    ki = jnp.arange(ATT_T)[:, None]
    qi = jnp.arange(ATT_T)[None, :]
    d = jnp.arange(B_NEAR)[:, None, None]
    delta = d * ATT_T + qi[None] - ki[None]
    bias = _bucket_lookup(bias_b, _t5_bucket(delta))
    far = bias_b[N_BUCKETS - 1].astype(F32)
    return ((bias - far) * LOG2E).transpose(3, 0, 1, 2)


def kernel(x, norm_gain, w_in, b_forget, qk_gain, sinks, w_branch, w_out, rel_bias):
    batch, seq, d = x.shape
    depth = norm_gain.shape[0]
    assert w_in.shape[-1] == IN_COLS and seq % 1024 == 0 and d == 1024
    assert (B_NEAR - 1) * ATT_T - (ATT_T - 1) < T5_FAR <= B_NEAR * ATT_T - (ATT_T - 1)

    w_perm = _gather_cols(w_in, _COLS).astype(BF16)
    w_vt = jnp.swapaxes(_gather_cols(w_in, _VT_COLS), 1, 2).astype(BF16)
    w_st = jnp.swapaxes(_gather_cols(w_in, _ST_COLS), 1, 2).astype(BF16)
    wb = w_branch.astype(BF16)
    wo = w_out.astype(BF16)

    gmat = jnp.asarray(np.kron(np.eye(N_HEADS), np.ones((HEAD_DIM, HEAD_DIM))), BF16)
    prep_t = 512
    lmat = jnp.asarray(np.tril(np.ones((prep_t, prep_t))), BF16)
    lmat_sel = lmat[:SEL_MC, :SEL_MC]
    tbl_a = _window_table(rel_bias[:, :N_HEADS])
    tbl_b = _near_table(rel_bias[:, N_HEADS:])

    x2d = x.reshape(batch * seq, d)
    for layer in range(depth):
        g = qk_gain[layer]
        qscale = ATTN_SCALE * LOG2E
        gains = jnp.zeros((8, 512), F32)
        gains = gains.at[0].set(jnp.tile(g[0, 0], N_HEADS) * qscale)
        gains = gains.at[1, :256].set(jnp.tile(g[0, 1], 4))
        gains = gains.at[2].set(jnp.tile(g[1, 0], N_HEADS) * qscale)
        gains = gains.at[3, :256].set(jnp.tile(g[1, 1], 4))
        gains = gains.at[4].set(jnp.tile(g[2, 0], N_HEADS) * qscale)
        gains = gains.at[5].set(jnp.tile(g[2, 1], N_HEADS))
        gains = gains.at[6, MISC_CF:MISC_CF + N_HEADS].set(b_forget[layer])

        z, misc, vt, st = _proj(x2d, norm_gain[layer][None, :], w_perm[layer], w_vt[layer], w_st[layer])
        qa, ka, qb, kb, qc, kc, cum = _prep(z, misc, gains, gmat, lmat, batch, seq)
        cum_t = cum.reshape(batch, seq, LANES)[:, :, MISC_CF:MISC_CF + N_HEADS].transpose(0, 2, 1)
        oa = _mix_a(qa, ka, z, vt, tbl_a, sinks[layer], batch, seq)
        maskadd = _sel_b(z, st, lmat_sel, batch, seq)
        ob = _mix_b(qb, kb, z, vt, maskadd, tbl_b, batch, seq)
        oc = _mix_c(qc, kc, z, vt, cum, cum_t, batch, seq)
        x2d = _merge(oa, ob, oc, z, x2d, wb[layer], wo[layer])
    return x2d.reshape(batch, seq, d)
```

```python
import functools
import math

import numpy as np
import jax
import jax.numpy as jnp
from jax import lax
from jax.experimental import pallas as pl
from jax.experimental.pallas import tpu as pltpu

F32 = jnp.float32
BF16 = jnp.bfloat16

HEAD_DIM = 64
LANES = 128
N_HEADS = 8
N_PAIRS = N_HEADS // 2
KV_GROUP = 4
WINDOW = 128
IDX_HEADS = 4
IDX_DIM = 32
TOPK_MAX = 256
N_BUCKETS = 32
MAX_DISTANCE = 512
EPS = 1e-6
ATTN_SCALE = HEAD_DIM ** -0.5
LOG2E = math.log2(math.e)
NEG = -1e30
T5_FAR = 413

_SIZES = dict(a_q=512, a_k=128, a_v=128, a_gate=512,
              b_q=512, b_k=128, b_v=128, b_iq=128, b_ik=32, b_iw=4, b_gate=512,
              c_q=512, c_k=512, c_v=512, c_f=8, c_gate=512, merge=3072)
_ORIG = {}
_o = 0
for _k, _v in _SIZES.items():
    _ORIG[_k] = _o
    _o += _v
IN_COLS = _o

OFF_MERGE = 0
OFF_AQ, OFF_AG, OFF_BQ, OFF_BG = 3072, 3584, 4096, 4608
OFF_CQ, OFF_CK, OFF_CG = 5120, 5632, 6144
OFF_AK, OFF_BK = 6656, 6912
OFF_BIQ, OFF_BIK, OFF_MISC = 7168, 7296, 7424
NP_COLS = 7680
PROJ_TN = 1536
MISC_CF = 8
VT_C, VT_A, VT_B, VT_ROWS = 0, 512, 640, 768


def _orig(name, idx=None):
    src = np.arange(_SIZES[name]) + _ORIG[name]
    return src if idx is None else src[idx]


def _column_map():
    cols = np.full((NP_COLS,), -1, np.int64)

    def put(off, src):
        cols[off:off + len(src)] = src

    for off, name in ((OFF_MERGE, "merge"), (OFF_AQ, "a_q"), (OFF_AG, "a_gate"), (OFF_BQ, "b_q"),
                      (OFF_BG, "b_gate"), (OFF_CQ, "c_q"), (OFF_CK, "c_k"), (OFF_CG, "c_gate"),
                      (OFF_BIQ, "b_iq")):
        put(off, _orig(name))
    dup = np.concatenate([np.arange(64), np.arange(64), np.arange(64, 128), np.arange(64, 128)])
    put(OFF_AK, _orig("a_k", dup))
    put(OFF_BK, _orig("b_k", dup))
    put(OFF_BIK, _orig("b_ik", np.tile(np.arange(IDX_DIM), IDX_HEADS)))
    put(OFF_MISC + MISC_CF, _orig("c_f"))
    return cols


def _gather_cols(w, cols):
    parts = []
    start = 0
    while start < len(cols):
        end = start + 1
        if cols[start] < 0:
            while end < len(cols) and cols[end] < 0:
                end += 1
            parts.append(jnp.zeros(w.shape[:-1] + (end - start,), w.dtype))
        else:
            while end < len(cols) and cols[end] == cols[end - 1] + 1:
                end += 1
            parts.append(w[..., int(cols[start]):int(cols[end - 1]) + 1])
        start = end
    return jnp.concatenate(parts, axis=-1)


_COLS = _column_map()
_VT_COLS = np.concatenate([_orig("c_v"), _orig("a_v"), _orig("b_v")])
_ST_COLS = np.concatenate([_orig("b_iw"), np.full((4,), -1, np.int64)])


def _params(sem, vmem_mb):
    return pltpu.CompilerParams(dimension_semantics=sem, vmem_limit_bytes=vmem_mb * 1024 * 1024)


def _dot(a, b):
    return jnp.dot(a, b, preferred_element_type=F32)


def _dot_nt(a, b):
    return lax.dot_general(a, b, (((1,), (1,)), ((), ())), preferred_element_type=F32)


def _t5_bucket(delta):
    n = jnp.maximum(delta, 0)
    max_exact = N_BUCKETS // 2
    nf = jnp.maximum(n, 1).astype(F32)
    large = max_exact + (jnp.log(nf / max_exact) / math.log(MAX_DISTANCE / max_exact)
                         * (N_BUCKETS - max_exact)).astype(jnp.int32)
    large = jnp.minimum(large, N_BUCKETS - 1)
    return jnp.where(n < max_exact, n, large)


def _proj_kernel(x_ref, g_ref, w_ref, wvt_ref, wst_ref, z_ref, misc_ref, vt_ref, st_ref, h_ref,
                 *, misc_tile, misc_local):
    j = pl.program_id(1)

    @pl.when(j == 0)
    def _():
        x = x_ref[...]
        ms = jnp.mean(x * x, axis=-1, keepdims=True)
        h = (x * lax.rsqrt(ms + EPS) * g_ref[...]).astype(BF16)
        h_ref[...] = h
        vt_ref[...] = _dot_nt(wvt_ref[...], h).astype(BF16)
        st_ref[...] = _dot_nt(wst_ref[...], h)

    acc = _dot(h_ref[...], w_ref[...])
    z_ref[...] = acc.astype(BF16)

    @pl.when(j == misc_tile)
    def _():
        misc_ref[...] = acc[:, misc_local:misc_local + LANES]


def _proj(x2d, gain, w, wvt, wst):
    n, d = x2d.shape
    tm = 1024
    return pl.pallas_call(
        functools.partial(_proj_kernel, misc_tile=OFF_MISC // PROJ_TN, misc_local=OFF_MISC % PROJ_TN),
        grid=(n // tm, NP_COLS // PROJ_TN),
        in_specs=[pl.BlockSpec((tm, d), lambda i, j: (i, 0)),
                  pl.BlockSpec((1, d), lambda i, j: (0, 0)),
                  pl.BlockSpec((d, PROJ_TN), lambda i, j: (0, j)),
                  pl.BlockSpec((VT_ROWS, d), lambda i, j: (0, 0)),
                  pl.BlockSpec((8, d), lambda i, j: (0, 0))],
        out_specs=[pl.BlockSpec((tm, PROJ_TN), lambda i, j: (i, j)),
                   pl.BlockSpec((tm, LANES), lambda i, j: (i, 0)),
                   pl.BlockSpec((VT_ROWS, tm), lambda i, j: (0, i)),
                   pl.BlockSpec((8, tm), lambda i, j: (0, i))],
        out_shape=[jax.ShapeDtypeStruct((n, NP_COLS), BF16),
                   jax.ShapeDtypeStruct((n, LANES), F32),
                   jax.ShapeDtypeStruct((VT_ROWS, n), BF16),
                   jax.ShapeDtypeStruct((8, n), F32)],
        scratch_shapes=[pltpu.VMEM((tm, d), BF16)],
        compiler_params=_params(("arbitrary", "arbitrary"), 56),
        name="proj",
    )(x2d, gain, w, wvt, wst)


def _group_rms(x_bf16, g_ref, gain_row):
    x = x_bf16.astype(F32)
    width = x.shape[-1]
    ss = _dot((x * x).astype(BF16), g_ref[:width, :width])
    return (x * lax.rsqrt(ss * (1.0 / HEAD_DIM) + EPS) * gain_row).astype(BF16)


def _prep_kernel(aq, ak, bq, bk, cq, ck, misc, gains, gmat, lmat,
                 qa_o, ka_o, qb_o, kb_o, qc_o, kc_o, cum_o, carry):
    t = pl.program_id(1)
    qa_o[...] = _group_rms(aq[...], gmat, gains[0:1, :])
    ka_o[...] = _group_rms(ak[...], gmat, gains[1:2, :256])
    qb_o[...] = _group_rms(bq[...], gmat, gains[2:3, :])
    kb_o[...] = _group_rms(bk[...], gmat, gains[3:4, :256])
    qc_o[...] = _group_rms(cq[...], gmat, gains[4:5, :])
    kc_o[...] = _group_rms(ck[...], gmat, gains[5:6, :])

    @pl.when(t == 0)
    def _():
        carry[...] = jnp.zeros_like(carry)

    xm = misc[...] + gains[6:7, :LANES]
    logf = jnp.minimum(xm, 0.0) - jnp.log(1.0 + jnp.exp(-jnp.abs(xm)))
    hi = logf.astype(BF16)
    r1 = logf - hi.astype(F32)
    mid = r1.astype(BF16)
    lo = (r1 - mid.astype(F32)).astype(BF16)
    lm = lmat[...]
    c = _dot(lm, hi) + _dot(lm, mid) + _dot(lm, lo) + carry[...]
    cum_o[...] = c * LOG2E
    rows = c.shape[0]
    carry[...] = c[rows - 1:rows, :]


def _prep(z, misc, gains, gmat, lmat, batch, seq):
    n = z.shape[0]
    tp = lmat.shape[0]
    nt = seq // tp

    def zspec(width, off):
        return pl.BlockSpec((tp, width), lambda b, t, _c=off // width: (b * nt + t, _c))

    def ospec(width):
        return pl.BlockSpec((tp, width), lambda b, t: (b * nt + t, 0))

    return pl.pallas_call(
        _prep_kernel,
        grid=(batch, nt),
        in_specs=[zspec(512, OFF_AQ), zspec(256, OFF_AK), zspec(512, OFF_BQ), zspec(256, OFF_BK),
                  zspec(512, OFF_CQ), zspec(512, OFF_CK), ospec(LANES),
                  pl.BlockSpec((8, 512), lambda b, t: (0, 0)),
                  pl.BlockSpec((512, 512), lambda b, t: (0, 0)),
                  pl.BlockSpec((tp, tp), lambda b, t: (0, 0))],
        out_specs=[ospec(512), ospec(256), ospec(512), ospec(256), ospec(512), ospec(512), ospec(LANES)],
        out_shape=[jax.ShapeDtypeStruct((n, 512), BF16), jax.ShapeDtypeStruct((n, 256), BF16),
                   jax.ShapeDtypeStruct((n, 512), BF16), jax.ShapeDtypeStruct((n, 256), BF16),
                   jax.ShapeDtypeStruct((n, 512), BF16), jax.ShapeDtypeStruct((n, 512), BF16),
                   jax.ShapeDtypeStruct((n, LANES), F32)],
        scratch_shapes=[pltpu.VMEM((1, LANES), F32)],
        compiler_params=_params(("arbitrary", "arbitrary"), 48),
        name="prep",
    )(z, z, z, z, z, z, misc, gains, gmat, lmat)


def _head_qt(q_ref, h):
    p = h // 2
    qt = q_ref[:, p * LANES:(p + 1) * LANES].astype(F32).T
    row = lax.broadcasted_iota(jnp.int32, qt.shape, 0)
    own = (row < HEAD_DIM) if h % 2 == 0 else (row >= HEAD_DIM)
    return jnp.where(own, qt, 0.0).astype(BF16)


DEN_ROWS = 16
ACC_ROWS = HEAD_DIM + DEN_ROWS


def _with_ones(vt):
    return jnp.concatenate([vt, jnp.ones((DEN_ROWS, vt.shape[1]), vt.dtype)], axis=0)


def _softmax_step(h, s, vt, m_ref, acc_ref, qbias=None):
    m_prev = m_ref[h]
    s_of = s if callable(s) else (lambda: s)
    s_max = jnp.max(s_of(), axis=0, keepdims=True)
    m_new = jnp.maximum(m_prev, s_max if qbias is None else s_max + qbias)
    alpha = jnp.exp2(m_prev - m_new)
    shift = m_new[0:1, :] if qbias is None else m_new[0:1, :] - qbias
    p = jnp.exp2((s_of() - shift).astype(BF16))
    acc_ref[h] = alpha[0:1, :] * acc_ref[h] + _dot(_with_ones(vt), p)
    m_ref[h] = m_new


HEADS_AHEAD = 4


def _heads_pipelined(logits, values, m_ref, acc_ref, qbias=None):
    ahead = [logits(h) for h in range(HEADS_AHEAD)]
    for h in range(N_HEADS):
        if h + HEADS_AHEAD < N_HEADS:
            ahead.append(logits(h + HEADS_AHEAD))
        _softmax_step(h, ahead.pop(0), values(h), m_ref, acc_ref,
                      None if qbias is None else qbias(h))


SKEW_AHEAD = 1


def _tiles_skewed(n_tiles, tile_logits, tile_values, m_ref, acc_ref, s_ref, qbias=None):
    bias = (lambda h: None) if qbias is None else qbias

    def produce(t, buf):
        logits = tile_logits(t)
        for h in range(N_HEADS):
            s_ref[buf, h] = logits(h)

    def consume(t, buf):
        values = tile_values(t)
        for h in range(N_HEADS):
            _softmax_step(h, lambda h=h: s_ref[buf, h], values(h), m_ref, acc_ref, bias(h))

    def both(t, buf):
        logits, values = tile_logits(t + 1), tile_values(t)
        for h in range(SKEW_AHEAD):
            s_ref[1 - buf, h] = logits(h)
        for h in range(N_HEADS):
            if h + SKEW_AHEAD < N_HEADS:
                s_ref[1 - buf, h + SKEW_AHEAD] = logits(h + SKEW_AHEAD)
            _softmax_step(h, lambda h=h: s_ref[buf, h], values(h), m_ref, acc_ref, bias(h))

    pairs = jnp.maximum(n_tiles - 1, 0) // 2
    rem = n_tiles - 2 * pairs

    @pl.when(n_tiles > 0)
    def _():
        produce(0, 0)

    def trip(p, carry):
        both(2 * p, 0)
        both(2 * p + 1, 1)
        return carry

    lax.fori_loop(0, pairs, trip, 0)

    @pl.when(rem == 2)
    def _():
        both(2 * pairs, 0)
        consume(2 * pairs + 1, 1)

    @pl.when(rem == 1)
    def _():
        consume(2 * pairs, 0)


def _init_state(m_ref, acc_ref):
    m_ref[...] = jnp.full(m_ref.shape, NEG, F32)
    acc_ref[...] = jnp.zeros(acc_ref.shape, F32)


def _gated_pair(p, gate, acc_ref):
    halves = []
    for h in (2 * p, 2 * p + 1):
        a = acc_ref[h]
        halves.append(a[:HEAD_DIM] / a[HEAD_DIM:HEAD_DIM + 1])
    ot = jnp.concatenate(halves, axis=0)
    g = gate.astype(F32)
    return (ot.T * (g * jax.nn.sigmoid(g))).astype(BF16)


def _mix_a_kernel(sink_ref, q_ref, kp_ref, kc_ref, vp_ref, vc_ref, gate_ref, tbl_ref, o_ref, acc_ref):
    n = pl.program_id(1)
    pad_pen = jnp.where(n > 0, 0.0, NEG).astype(F32)
    den_rows = lax.broadcasted_iota(jnp.int32, (ACC_ROWS, WINDOW), 0) >= HEAD_DIM

    def logits(h):
        qt = _head_qt(q_ref, h)
        ksl = slice((h // KV_GROUP) * LANES, (h // KV_GROUP + 1) * LANES)
        return (_dot(kp_ref[:, ksl], qt) + tbl_ref[h, :WINDOW, :] + pad_pen,
                _dot(kc_ref[:, ksl], qt) + tbl_ref[h, WINDOW:, :])

    all_logits = [logits(h) for h in range(N_HEADS)]
    for h in range(N_HEADS):
        kv = h // KV_GROUP
        vsl = slice(kv * HEAD_DIM, (kv + 1) * HEAD_DIM)
        s_prev, s_cur = all_logits[h]
        sink = sink_ref[h] * LOG2E
        m = jnp.maximum(jnp.maximum(jnp.max(s_prev, axis=0, keepdims=True),
                                    jnp.max(s_cur, axis=0, keepdims=True)), sink)
        p_prev = jnp.exp2((s_prev - m).astype(BF16))
        p_cur = jnp.exp2((s_cur - m).astype(BF16))
        acc = _dot(_with_ones(vp_ref[vsl, :]), p_prev) + _dot(_with_ones(vc_ref[vsl, :]), p_cur)
        acc_ref[h] = acc + jnp.where(den_rows, jnp.exp2(sink - m), 0.0)
    for p in range(N_PAIRS):
        sl = slice(p * LANES, (p + 1) * LANES)
        o_ref[:, sl] = _gated_pair(p, gate_ref[:, sl], acc_ref)


def _mix_a(qa, ka, z, vt, tbl, sinks, batch, seq):
    n = qa.shape[0]
    t = WINDOW
    nb = seq // t
    return pl.pallas_call(
        _mix_a_kernel,
        grid=(batch, nb),
        in_specs=[pl.BlockSpec(memory_space=pltpu.SMEM),
                  pl.BlockSpec((t, 512), lambda b, i: (b * nb + i, 0)),
                  pl.BlockSpec((t, 256), lambda b, i: (b * nb + jnp.maximum(i - 1, 0), 0)),
                  pl.BlockSpec((t, 256), lambda b, i: (b * nb + i, 0)),
                  pl.BlockSpec((LANES, t), lambda b, i: (VT_A // LANES, b * nb + jnp.maximum(i - 1, 0))),
                  pl.BlockSpec((LANES, t), lambda b, i: (VT_A // LANES, b * nb + i)),
                  pl.BlockSpec((t, 512), lambda b, i: (b * nb + i, OFF_AG // 512)),
                  pl.BlockSpec((N_HEADS, 2 * t, t), lambda b, i: (0, 0, 0))],
        out_specs=pl.BlockSpec((t, 512), lambda b, i: (b * nb + i, 0)),
        out_shape=jax.ShapeDtypeStruct((n, 512), BF16),
        scratch_shapes=[pltpu.VMEM((N_HEADS, ACC_ROWS, t), F32)],
        compiler_params=_params(("arbitrary", "arbitrary"), 32),
        name="mix_a",
    )(sinks, qa, ka, ka, vt, vt, z, tbl)


SEL_TQ = 256
SEL_SC = 512
SEL_SUB = 64
SEL_MC = 256
SEL_FAST_ITERS = 24
SEL_MAX_ITERS = 600


def _sel_b_kernel(iq_ref, ik_ref, wt_ref, lmat_ref, o_ref, sc_ref, raw_ref, *, topk):
    n = pl.program_id(1)
    t0 = n * SEL_TQ
    n_full = n // (SEL_SC // SEL_TQ)
    n_sc = n_full + 1
    n_mc = n + 1
    kf = float(topk)

    iqt = iq_ref[...].astype(F32).T
    row = lax.broadcasted_iota(jnp.int32, iqt.shape, 0)
    qst = jnp.concatenate(
        [jnp.where((row >= h * IDX_DIM) & (row < (h + 1) * IDX_DIM), iqt, 0.0)
         for h in range(IDX_HEADS)], axis=1).astype(BF16)
    wscale = (IDX_HEADS ** -0.5) * (IDX_DIM ** -0.5)
    w = [wt_ref[h:h + 1, :] * wscale for h in range(IDX_HEADS)]
    qpos = t0 + lax.broadcasted_iota(jnp.int32, (SEL_SUB, SEL_TQ), 1)
    kiota = lax.broadcasted_iota(jnp.int32, (SEL_SUB, SEL_TQ), 0)

    def raw_into(c, buf):
        base = pl.multiple_of(c * SEL_SC, SEL_SC)
        raw_ref[buf] = _dot(ik_ref[pl.ds(base, SEL_SC), :], qst)

    def score_chunk(c, buf, carry, diagonal):
        rmax, rmin, ge0, gt0, minpos = carry
        base = pl.multiple_of(c * SEL_SC, SEL_SC)
        for u in range(SEL_SC // SEL_SUB):
            rows = slice(u * SEL_SUB, (u + 1) * SEL_SUB)
            score = w[0] * jnp.maximum(raw_ref[buf, rows, 0:SEL_TQ], 0.0)
            for h in range(1, IDX_HEADS):
                score = score + w[h] * jnp.maximum(raw_ref[buf, rows, h * SEL_TQ:(h + 1) * SEL_TQ], 0.0)
            rmin = jnp.minimum(rmin, score)
            if diagonal:
                score = jnp.where((kiota + (base + u * SEL_SUB)) <= qpos, score, -jnp.inf)
            sc_ref[pl.ds(base + u * SEL_SUB, SEL_SUB), :] = score
            rmax = jnp.maximum(rmax, score)
            pos = score > 0.0
            ge0 = ge0 + jnp.where(score >= 0.0, 1.0, 0.0)
            gt0 = gt0 + jnp.where(pos, 1.0, 0.0)
            minpos = jnp.minimum(minpos, jnp.where(pos, score, jnp.inf))
        return rmax, rmin, ge0, gt0, minpos

    def slab(v):
        return jnp.full((SEL_SUB, SEL_TQ), v, F32)

    def score_trip(p, carry):
        raw_into(2 * p + 1, 1)
        carry = score_chunk(2 * p, 0, carry, False)
        raw_into(2 * p + 2, 0)
        return score_chunk(2 * p + 1, 1, carry, False)

    def tail_two(carry):
        raw_into(n_full, 1)
        carry = score_chunk(n_full - 1, 0, carry, False)
        return score_chunk(n_full, 1, carry, True)

    def tail_one(carry):
        return score_chunk(n_full, 0, carry, True)

    raw_into(0, 0)
    carry = lax.fori_loop(0, n_full // 2, score_trip,
                          (slab(-jnp.inf), slab(jnp.inf), slab(0.0), slab(0.0), slab(jnp.inf)))
    rmax, rmin, ge0, gt0, minpos = lax.cond(n_full % 2 == 1, tail_two, tail_one, carry)
    rmax = jnp.max(rmax, axis=0, keepdims=True)
    rmin = jnp.min(rmin, axis=0, keepdims=True)
    ge0 = jnp.sum(ge0, axis=0, keepdims=True)
    gt0 = jnp.sum(gt0, axis=0, keepdims=True)
    minpos = jnp.min(minpos, axis=0, keepdims=True)

    navail = (t0 + 1 + lax.broadcasted_iota(jnp.int32, (1, SEL_TQ), 1)).astype(F32)

    def sweep(mid, snap):
        midb = jnp.broadcast_to(mid, (SEL_SUB, SEL_TQ))

        def body(c, carry):
            base = pl.multiple_of(c * SEL_SC, SEL_SC)
            cnt, up, dn = carry
            for u in range(SEL_SC // SEL_SUB):
                x = sc_ref[pl.ds(base + u * SEL_SUB, SEL_SUB), :]
                ge = x >= midb
                cnt = cnt + jnp.where(ge, 1.0, 0.0)
                if snap:
                    up = jnp.minimum(up, jnp.where(ge, x, jnp.inf))
                    dn = jnp.maximum(dn, jnp.where(ge, -jnp.inf, x))
            return cnt, up, dn

        init = (jnp.zeros((SEL_SUB, SEL_TQ), F32), jnp.full((SEL_SUB, SEL_TQ), jnp.inf, F32),
                jnp.full((SEL_SUB, SEL_TQ), -jnp.inf, F32))
        cnt, up, dn = lax.fori_loop(0, n_sc, body, init)
        return (jnp.sum(cnt, axis=0, keepdims=True), jnp.min(up, axis=0, keepdims=True),
                jnp.max(dn, axis=0, keepdims=True))

    def advance(st, mid, snap):
        it, lo, hi, hi_dn, c_lo, c_hi, done = st
        cnt, up, dn = sweep(mid, snap)
        live = jnp.where(done > 0.5, 0.0, jnp.where(mid <= lo, 0.0, jnp.where(mid >= hi, 0.0, 1.0)))
        ge = jnp.where(cnt >= kf, live, 0.0) > 0.5
        lt = jnp.where(cnt >= kf, 0.0, live) > 0.5
        lo = jnp.where(ge, up if snap else mid, lo)
        c_lo = jnp.where(ge, cnt, c_lo)
        hi = jnp.where(lt, mid, hi)
        hi_dn = jnp.where(lt, dn if snap else jnp.inf, hi_dn)
        c_hi = jnp.where(lt, cnt, c_hi)
        done = jnp.where(live < 0.5, 1.0, jnp.where(cnt == kf, 1.0, jnp.where(hi_dn <= lo, 1.0, 0.0)))
        return it + 1, lo, hi, hi_dn, c_lo, c_hi, done

    def make_step(snap, repeat=1):
        def step(st):
            for _ in range(repeat):
                st = advance(st, 0.5 * st[1] + 0.5 * st[2], snap)
            return st
        return step

    def make_cond(limit):
        def cond(st):
            return jnp.logical_and(st[0] < limit, jnp.min(st[6]) < 0.5)
        return cond

    inf_row = jnp.full((1, SEL_TQ), jnp.inf, F32)
    all_in = navail <= kf
    above = jnp.logical_and(jnp.logical_not(all_in), gt0 >= kf)
    at_zero = jnp.logical_and(jnp.logical_not(all_in), jnp.logical_and(gt0 < kf, ge0 >= kf))
    below = jnp.logical_and(jnp.logical_not(all_in), ge0 < kf)
    lo = jnp.where(above, minpos, jnp.where(at_zero, 0.0, rmin))
    c_lo = jnp.where(above, gt0, jnp.where(at_zero, ge0, navail))
    hi = jnp.where(at_zero, minpos, jnp.where(below, 0.0, inf_row))
    c_hi = jnp.where(at_zero, gt0, jnp.where(below, ge0, 0.0))
    hi_dn = jnp.where(above, rmax, inf_row)
    done = jnp.where(all_in, 1.0, jnp.where(at_zero, 1.0, 0.0))
    st = (jnp.int32(0), lo, hi, hi_dn, c_lo, c_hi, done)
    st = advance(st, jnp.where(above, rmax, 0.5 * lo + 0.5 * hi), False)
    st = lax.while_loop(make_cond(SEL_FAST_ITERS), make_step(False, repeat=2), st)
    st = lax.while_loop(make_cond(SEL_MAX_ITERS), make_step(True), st)
    _, lo, hi, _, c_lo, c_hi, _ = st
    need = kf - c_hi
    ties = jnp.max(jnp.where(navail > kf, c_lo, kf)) > kf

    def fill_body(c, carry):
        base = pl.multiple_of(c * SEL_MC, SEL_MC)
        o_ref[0, pl.ds(base, SEL_MC), :] = jnp.full((SEL_MC, SEL_TQ), NEG, o_ref.dtype)
        return carry
    lax.fori_loop(n_mc, o_ref.shape[1] // SEL_MC, fill_body, 0)

    @pl.when(jnp.logical_not(ties))
    def _():
        def mask_body(c, carry):
            base = pl.multiple_of(c * SEL_MC, SEL_MC)
            x = sc_ref[pl.ds(base, SEL_MC), :]
            o_ref[0, pl.ds(base, SEL_MC), :] = jnp.where(x >= lo, 0.0, NEG).astype(o_ref.dtype)
            return carry
        lax.fori_loop(0, n_mc, mask_body, 0)

    @pl.when(ties)
    def _():
        lmat = lmat_ref[...]

        def mask_body(c, run):
            base = pl.multiple_of(c * SEL_SC, SEL_SC)
            xs, ge_los, lt_his, counts = [], [], [], []
            for u in range(SEL_SC // SEL_MC):
                x = sc_ref[pl.ds(base + u * SEL_MC, SEL_MC), :]
                ge_lo = x >= lo
                lt_hi = x < hi
                bnd = jnp.where(ge_lo, jnp.where(lt_hi, 1.0, 0.0), 0.0)
                xs.append(x)
                ge_los.append(ge_lo)
                lt_his.append(lt_hi)
                counts.append(_dot(lmat, bnd.astype(BF16)))
            for u in range(SEL_SC // SEL_MC):
                incl = counts[u] + run
                inside = jnp.where(lt_his[u], jnp.where(incl <= need, 0.0, NEG), 0.0)
                o_ref[0, pl.ds(base + u * SEL_MC, SEL_MC), :] = (
                    jnp.where(ge_los[u], inside, NEG).astype(o_ref.dtype))
                run = incl[SEL_MC - 1:SEL_MC, :]
            return run
        lax.fori_loop(0, n_sc, mask_body, jnp.zeros((1, SEL_TQ), F32))


def _sel_b(z, st, lmat, batch, seq):
    nb = seq // SEL_TQ
    topk = min(TOPK_MAX, seq // 4)
    return pl.pallas_call(
        functools.partial(_sel_b_kernel, topk=topk),
        grid=(batch, nb),
        in_specs=[pl.BlockSpec((SEL_TQ, LANES), lambda b, i: (b * nb + i, OFF_BIQ // LANES)),
                  pl.BlockSpec((seq, LANES), lambda b, i: (b, OFF_BIK // LANES)),
                  pl.BlockSpec((8, SEL_TQ), lambda b, i: (0, b * nb + i)),
                  pl.BlockSpec((SEL_MC, SEL_MC), lambda b, i: (0, 0))],
        out_specs=pl.BlockSpec((1, seq, SEL_TQ), lambda b, i: (b, 0, i)),
        out_shape=jax.ShapeDtypeStruct((batch, seq, seq), BF16),
        scratch_shapes=[pltpu.VMEM((seq, SEL_TQ), F32),
                        pltpu.VMEM((2, SEL_SC, IDX_HEADS * SEL_TQ), F32)],
        compiler_params=_params(("arbitrary", "arbitrary"), 48),
        name="sel_b",
    )(z, z, st, lmat)


ATT_T = 256
B_NEAR = 3
FAR_TILES = 2


def _far_remainder(n_tiles, tile):
    done = (n_tiles // FAR_TILES) * FAR_TILES
    rem = n_tiles - done
    size = FAR_TILES // 2
    while size >= 1:
        @pl.when((rem // size) % 2 == 1)
        def _(size=size, start=done + (rem // (2 * size)) * (2 * size)):
            tile(start, size * ATT_T)
        size //= 2


def _mix_b_kernel(q_ref, k_ref, vt_ref, mask_ref, gate_ref, tbl_ref, o_ref, m_ref, acc_ref, s_ref):
    i = pl.program_id(1)
    _init_state(m_ref, acc_ref)
    qs = [_head_qt(q_ref, h) for h in range(N_HEADS)]

    def tile_logits(j, size, near=None):
        base = pl.multiple_of(j * ATT_T, ATT_T)
        madd = mask_ref[0, pl.ds(base, size), :].astype(F32)

        def logits(h):
            kv = h // KV_GROUP
            s = _dot(k_ref[pl.ds(base, size), kv * LANES:(kv + 1) * LANES], qs[h]) + madd
            if near is None:
                return s
            if near == "all":
                return s + jnp.concatenate([tbl_ref[h, d] for d in range(B_NEAR - 1, -1, -1)], axis=0)
            return s + tbl_ref[h, near]
        return logits

    def tile_values(j, size):
        base = pl.multiple_of(j * ATT_T, ATT_T)

        def values(h):
            kv = h // KV_GROUP
            return vt_ref[kv * HEAD_DIM:(kv + 1) * HEAD_DIM, pl.ds(base, size)]
        return values

    def tile(j, size, near):
        _heads_pipelined(tile_logits(j, size, near), tile_values(j, size), m_ref, acc_ref)

    n_far = jnp.maximum(i - (B_NEAR - 1), 0)
    far_size = FAR_TILES * ATT_T
    _tiles_skewed(n_far // FAR_TILES, lambda t: tile_logits(t * FAR_TILES, far_size),
                  lambda t: tile_values(t * FAR_TILES, far_size), m_ref, acc_ref, s_ref)
    _far_remainder(n_far, lambda j, size: tile(j, size, None))

    @pl.when(i >= B_NEAR - 1)
    def _():
        tile(i - (B_NEAR - 1), B_NEAR * ATT_T, "all")

    for d in range(B_NEAR - 2, -1, -1):
        @pl.when(jnp.logical_and(i < B_NEAR - 1, i >= d))
        def _(d=d):
            tile(i - d, ATT_T, d)
    for p in range(N_PAIRS):
        sl = slice(p * LANES, (p + 1) * LANES)
        o_ref[:, sl] = _gated_pair(p, gate_ref[:, sl], acc_ref)


def _mix_b(qb, kb, z, vt, maskadd, tbl, batch, seq):
    n = qb.shape[0]
    t = ATT_T
    nq = seq // t
    return pl.pallas_call(
        _mix_b_kernel,
        grid=(batch, nq),
        in_specs=[pl.BlockSpec((t, 512), lambda b, i: (b * nq + i, 0)),
                  pl.BlockSpec((seq, 256), lambda b, i: (b, 0), pipeline_mode=pl.Buffered(1)),
                  pl.BlockSpec((LANES, seq), lambda b, i: (VT_B // LANES, b), pipeline_mode=pl.Buffered(1)),
                  pl.BlockSpec((1, seq, t), lambda b, i: (b, 0, i)),
                  pl.BlockSpec((t, 512), lambda b, i: (b * nq + i, OFF_BG // 512)),
                  pl.BlockSpec((N_HEADS, B_NEAR, t, t), lambda b, i: (0, 0, 0, 0),
                               pipeline_mode=pl.Buffered(1))],
        out_specs=pl.BlockSpec((t, 512), lambda b, i: (b * nq + i, 0)),
        out_shape=jax.ShapeDtypeStruct((n, 512), BF16),
        scratch_shapes=[pltpu.VMEM((N_HEADS, 8, t), F32), pltpu.VMEM((N_HEADS, ACC_ROWS, t), F32),
                        pltpu.VMEM((2, N_HEADS, FAR_TILES * t, t), F32)],
        compiler_params=_params(("arbitrary", "arbitrary"), 56),
        name="mix_b",
    )(qb, kb, vt, maskadd, z, tbl)


def _mix_c_kernel(q_ref, k_ref, vt_ref, ck_ref, cq_ref, gate_ref, o_ref, m_ref, acc_ref, s_ref):
    i = pl.program_id(1)
    _init_state(m_ref, acc_ref)
    qs = [_head_qt(q_ref, h) for h in range(N_HEADS)]
    q0 = pl.multiple_of(i * ATT_T, ATT_T)
    cq = cq_ref[0, :, pl.ds(q0, ATT_T)]
    krow = lax.broadcasted_iota(jnp.int32, (ATT_T, ATT_T), 0)
    qcol = lax.broadcasted_iota(jnp.int32, (ATT_T, ATT_T), 1)

    def tile_logits(j, size, diag=False):
        base = pl.multiple_of(j * ATT_T, ATT_T)
        ck_all = ck_ref[pl.ds(base, size), :]

        def logits(h):
            p = h // 2
            ck = ck_all[:, MISC_CF + h:MISC_CF + h + 1]
            s = _dot(k_ref[pl.ds(base, size), p * LANES:(p + 1) * LANES], qs[h]) - ck
            return jnp.where(krow <= qcol, s, NEG) if diag else s
        return logits

    def tile_values(j, size):
        base = pl.multiple_of(j * ATT_T, ATT_T)
        return lambda h: vt_ref[h * HEAD_DIM:(h + 1) * HEAD_DIM, pl.ds(base, size)]

    def qbias(h):
        return cq[h:h + 1, :]

    def tile(j, size, diag):
        _heads_pipelined(tile_logits(j, size, diag), tile_values(j, size), m_ref, acc_ref, qbias=qbias)

    far_size = FAR_TILES * ATT_T
    _tiles_skewed(i // FAR_TILES, lambda t: tile_logits(t * FAR_TILES, far_size),
                  lambda t: tile_values(t * FAR_TILES, far_size), m_ref, acc_ref, s_ref, qbias=qbias)
    _far_remainder(i, lambda j, size: tile(j, size, False))
    tile(i, ATT_T, True)
    for p in range(N_PAIRS):
        sl = slice(p * LANES, (p + 1) * LANES)
        o_ref[:, sl] = _gated_pair(p, gate_ref[:, sl], acc_ref)


def _mix_c(qc, kc, z, vt, cum, cum_t, batch, seq):
    n = qc.shape[0]
    t = ATT_T
    nq = seq // t
    return pl.pallas_call(
        _mix_c_kernel,
        grid=(batch, nq),
        in_specs=[pl.BlockSpec((t, 512), lambda b, i: (b * nq + i, 0)),
                  pl.BlockSpec((seq, 512), lambda b, i: (b, 0), pipeline_mode=pl.Buffered(1)),
                  pl.BlockSpec((512, seq), lambda b, i: (VT_C // 512, b), pipeline_mode=pl.Buffered(1)),
                  pl.BlockSpec((seq, LANES), lambda b, i: (b, 0), pipeline_mode=pl.Buffered(1)),
                  pl.BlockSpec((1, N_HEADS, seq), lambda b, i: (b, 0, 0)),
                  pl.BlockSpec((t, 512), lambda b, i: (b * nq + i, OFF_CG // 512))],
        out_specs=pl.BlockSpec((t, 512), lambda b, i: (b * nq + i, 0)),
        out_shape=jax.ShapeDtypeStruct((n, 512), BF16),
        scratch_shapes=[pltpu.VMEM((N_HEADS, 8, t), F32), pltpu.VMEM((N_HEADS, ACC_ROWS, t), F32),
                        pltpu.VMEM((2, N_HEADS, FAR_TILES * t, t), F32)],
        compiler_params=_params(("arbitrary", "arbitrary"), 56),
        name="mix_c",
    )(qc, kc, vt, cum, cum_t, z)


def _merge_kernel(oa, ob, oc, ga, gb, gc, x_ref, wb_ref, wo_ref, o_ref):
    merged = None
    for br, (o, g) in enumerate(((oa, ga), (ob, gb), (oc, gc))):
        y = _dot(o[...], wb_ref[br])
        term = jax.nn.sigmoid(g[...].astype(F32)) * y
        merged = term if merged is None else merged + term
    o_ref[...] = x_ref[...] + _dot(merged.astype(BF16), wo_ref[...])


def _merge(oa, ob, oc, z, x2d, wb, wo):
    n, d = x2d.shape
    tm = 512

    def ospec():
        return pl.BlockSpec((tm, 512), lambda i: (i, 0))

    def gspec(k):
        return pl.BlockSpec((tm, d), lambda i, _k=k: (i, OFF_MERGE // d + _k))

    return pl.pallas_call(
        _merge_kernel,
        grid=(n // tm,),
        in_specs=[ospec(), ospec(), ospec(), gspec(0), gspec(1), gspec(2),
                  pl.BlockSpec((tm, d), lambda i: (i, 0)),
                  pl.BlockSpec((3, 512, d), lambda i: (0, 0, 0)),
                  pl.BlockSpec((d, d), lambda i: (0, 0))],
        out_specs=pl.BlockSpec((tm, d), lambda i: (i, 0)),
        out_shape=jax.ShapeDtypeStruct((n, d), F32),
        compiler_params=_params(("arbitrary",), 48),
        name="merge",
    )(oa, ob, oc, z, z, z, x2d, wb, wo)


def _bucket_lookup(table, bucket):
    out = jnp.zeros(bucket.shape + (table.shape[1],), F32)
    for b in range(N_BUCKETS):
        out = jnp.where((bucket == b)[..., None], table[b].astype(F32), out)
    return out


def _window_table(bias_a):
    ki = jnp.arange(2 * WINDOW)[:, None]
    qi = jnp.arange(WINDOW)[None, :]
    delta = qi + WINDOW - ki
    band = (delta >= 0) & (delta < WINDOW)
    bias = _bucket_lookup(bias_a, _t5_bucket(delta)).transpose(2, 0, 1) * LOG2E
    return jnp.where(band[None], bias, NEG)


def _near_table(bias_b):
    ki = jnp.arange(ATT_T)[:, None]
    qi = jnp.arange(ATT_T)[None, :]
    d = jnp.arange(B_NEAR)[:, None, None]
    delta = d * ATT_T + qi[None] - ki[None]
    bias = _bucket_lookup(bias_b, _t5_bucket(delta))
    far = bias_b[N_BUCKETS - 1].astype(F32)
    return ((bias - far) * LOG2E).transpose(3, 0, 1, 2)


def kernel(x, norm_gain, w_in, b_forget, qk_gain, sinks, w_branch, w_out, rel_bias):
    batch, seq, d = x.shape
    depth = norm_gain.shape[0]
    assert w_in.shape[-1] == IN_COLS and seq % 1024 == 0 and d == 1024
    assert (B_NEAR - 1) * ATT_T - (ATT_T - 1) < T5_FAR <= B_NEAR * ATT_T - (ATT_T - 1)

    w_perm = _gather_cols(w_in, _COLS).astype(BF16)
    w_vt = jnp.swapaxes(_gather_cols(w_in, _VT_COLS), 1, 2).astype(BF16)
    w_st = jnp.swapaxes(_gather_cols(w_in, _ST_COLS), 1, 2).astype(BF16)
    wb = w_branch.astype(BF16)
    wo = w_out.astype(BF16)

    gmat = jnp.asarray(np.kron(np.eye(N_HEADS), np.ones((HEAD_DIM, HEAD_DIM))), BF16)
    prep_t = 512
    lmat = jnp.asarray(np.tril(np.ones((prep_t, prep_t))), BF16)
    lmat_sel = lmat[:SEL_MC, :SEL_MC]
    tbl_a = _window_table(rel_bias[:, :N_HEADS])
    tbl_b = _near_table(rel_bias[:, N_HEADS:])

    x2d = x.reshape(batch * seq, d)
    for layer in range(depth):
        g = qk_gain[layer]
        qscale = ATTN_SCALE * LOG2E
        gains = jnp.zeros((8, 512), F32)
        gains = gains.at[0].set(jnp.tile(g[0, 0], N_HEADS) * qscale)
        gains = gains.at[1, :256].set(jnp.tile(g[0, 1], 4))
        gains = gains.at[2].set(jnp.tile(g[1, 0], N_HEADS) * qscale)
        gains = gains.at[3, :256].set(jnp.tile(g[1, 1], 4))
        gains = gains.at[4].set(jnp.tile(g[2, 0], N_HEADS) * qscale)
        gains = gains.at[5].set(jnp.tile(g[2, 1], N_HEADS))
        gains = gains.at[6, MISC_CF:MISC_CF + N_HEADS].set(b_forget[layer])

        z, misc, vt, st = _proj(x2d, norm_gain[layer][None, :], w_perm[layer], w_vt[layer], w_st[layer])
        qa, ka, qb, kb, qc, kc, cum = _prep(z, misc, gains, gmat, lmat, batch, seq)
        cum_t = cum.reshape(batch, seq, LANES)[:, :, MISC_CF:MISC_CF + N_HEADS].transpose(0, 2, 1)
        oa = _mix_a(qa, ka, z, vt, tbl_a, sinks[layer], batch, seq)
        maskadd = _sel_b(z, st, lmat_sel, batch, seq)
        ob = _mix_b(qb, kb, z, vt, maskadd, tbl_b, batch, seq)
        oc = _mix_c(qc, kc, z, vt, cum, cum_t, batch, seq)
        x2d = _merge(oa, ob, oc, z, x2d, wb[layer], wo[layer])
    return x2d.reshape(batch, seq, d)
```
